```python
import math
import jax, jax.numpy as jnp
from jax import lax
import numpy as np

D_MODEL = 1024
BATCH = 4
SEQ = 4096
DEPTH = 1

HEAD_DIM = 64
DIFF_HEADS = D_MODEL // (4 * HEAD_DIM)
DIFF_VDIM = 2 * HEAD_DIM
SB_HEADS = D_MODEL // (2 * HEAD_DIM)
DIFF_WIDTH = DIFF_HEADS * DIFF_VDIM
SB_WIDTH = SB_HEADS * HEAD_DIM
MIX_WIDTH = DIFF_WIDTH + SB_WIDTH
IN_SIZES = (DIFF_WIDTH, DIFF_WIDTH, DIFF_WIDTH, SB_WIDTH, SB_WIDTH, SB_WIDTH)
IN_COLS = sum(IN_SIZES)
ROPE_THETA = 10000.0
Q_BLOCK = 128
N_EXPERTS = 256
TOP_K = 8
N_GROUPS = 8
TOPK_GROUPS = 4
EXPERT_FF = 256
SHARED_FF = 256
ROUTED_SCALE = 2.5
EXPERT_ROWS = 128
NORM_EPS = 1e-6
N_MOD = 6

kernel_name = "hybrid_diffattn_stickbreak_moe_adaln"


def rms_norm(x, g):
    xf = x.astype(jnp.float32)
    y = xf * lax.rsqrt(jnp.mean(xf * xf, axis=-1, keepdims=True) + NORM_EPS)
    return y * g.astype(jnp.float32)


def rope_tables(positions):
    inv = ROPE_THETA ** (-jnp.arange(0, HEAD_DIM, 2, dtype=jnp.float32) / HEAD_DIM)
    ang = positions.astype(jnp.float32)[..., None] * inv
    ang = jnp.concatenate([ang, ang], axis=-1)
    return jnp.cos(ang), jnp.sin(ang)


def apply_rope(x, cos, sin):
    x1, x2 = jnp.split(x, 2, axis=-1)
    return x * cos + jnp.concatenate([-x2, x1], axis=-1) * sin


def diff_attention(q, k, v, lam, g_q, g_k, g_sub, cos, sin, lambda_init):
    B, S = q.shape[0], q.shape[1]
    cs = cos[:, :, None, None, :]
    sn = sin[:, :, None, None, :]
    q = apply_rope(rms_norm(q, g_q), cs, sn).transpose(0, 2, 3, 1, 4)
    k = apply_rope(rms_norm(k, g_k), cs, sn).transpose(0, 2, 3, 1, 4)
    v = v.astype(jnp.float32).transpose(0, 2, 1, 3)
    lam = lam.astype(jnp.float32)
    lmbda = jnp.exp(jnp.sum(lam[0] * lam[1])) - jnp.exp(jnp.sum(lam[2] * lam[3])) + lambda_init
    scale = HEAD_DIM ** -0.5
    outs = []
    for i in range(S // Q_BLOCK):
        lo, hi = i * Q_BLOCK, (i + 1) * Q_BLOCK
        s = jnp.einsum("bhmqd,bhmkd->bhmqk", q[:, :, :, lo:hi], k[:, :, :, :hi]) * scale
        causal = jnp.arange(hi)[None, :] <= jnp.arange(lo, hi)[:, None]
        p = jax.nn.softmax(jnp.where(causal, s, -jnp.inf), axis=-1)
        a = p[:, :, 0] - lmbda * p[:, :, 1]
        outs.append(jnp.einsum("bhqk,bhkd->bhqd", a, v[:, :, :hi]))
    o = jnp.concatenate(outs, axis=2)
    o = rms_norm(o, g_sub) * (1.0 - lambda_init)
    return o.transpose(0, 2, 1, 3).reshape(B, S, DIFF_WIDTH)


def stick_breaking(q, k, v):
    B, S = q.shape[0], q.shape[1]
    q = q.astype(jnp.float32).transpose(0, 2, 1, 3)
    k = k.astype(jnp.float32).transpose(0, 2, 1, 3)
    v = v.astype(jnp.float32).transpose(0, 2, 1, 3)
    scale = HEAD_DIM ** -0.5
    outs = []
    for i in range(S // Q_BLOCK):
        lo, hi = i * Q_BLOCK, (i + 1) * Q_BLOCK
        z = jnp.einsum("bhqd,bhkd->bhqk", q[:, :, lo:hi], k[:, :, :hi]) * scale
        strict = jnp.arange(hi)[None, :] < jnp.arange(lo, hi)[:, None]
        log_keep = jnp.where(strict, jax.nn.log_sigmoid(-z), 0.0)
        log_w = jax.nn.log_sigmoid(z) + lax.cumsum(log_keep, axis=3, reverse=True) - log_keep
        w = jnp.where(strict, jnp.exp(log_w), 0.0)
        outs.append(jnp.einsum("bhqk,bhkd->bhqd", w, v[:, :, :hi]))
    o = jnp.concatenate(outs, axis=2)
    return o.transpose(0, 2, 1, 3).reshape(B, S, SB_WIDTH)


def route(h, w_router, e_bias):
    T = h.shape[0]
    scores = jax.nn.sigmoid(h @ w_router.astype(jnp.float32))
    biased = scores + e_bias.astype(jnp.float32)
    grp = biased.reshape(T, N_GROUPS, N_EXPERTS // N_GROUPS)
    gscore = jnp.sum(lax.top_k(grp, 2)[0], axis=-1)
    _, gidx = lax.top_k(gscore, TOPK_GROUPS)
    gmask = jnp.any(gidx[..., None] == jnp.arange(N_GROUPS), axis=1)
    emask = jnp.repeat(gmask, N_EXPERTS // N_GROUPS, axis=1)
    _, idx = lax.top_k(jnp.where(emask, biased, -jnp.inf), TOP_K)
    w = jnp.take_along_axis(scores, idx, axis=1)
    w = w / jnp.sum(w, axis=-1, keepdims=True) * ROUTED_SCALE
    return idx, w


def swiglu(x, w_gu, w_down):
    gu = x @ w_gu.astype(jnp.float32)
    g, u = jnp.split(gu, 2, axis=-1)
    return (jax.nn.silu(g) * u) @ w_down.astype(jnp.float32)


def routed_experts(h, idx, wts, w_gu, w_down):
    T, D = h.shape
    K = idx.shape[1]
    A = T * K
    flat_e = idx.reshape(-1).astype(jnp.int32)
    flat_tok = jnp.repeat(jnp.arange(T, dtype=jnp.int32), K)
    flat_w = wts.reshape(-1)
    order = jnp.argsort(flat_e)
    se = flat_e[order]
    counts = jnp.bincount(flat_e, length=N_EXPERTS)
    padded = (counts + EXPERT_ROWS - 1) // EXPERT_ROWS * EXPERT_ROWS
    start = jnp.cumsum(counts) - counts
    pend = jnp.cumsum(padded)
    pstart = pend - padded
    dest = pstart[se] + (jnp.arange(A, dtype=jnp.int32) - start[se])
    n_blocks = -(-A // EXPERT_ROWS) + N_EXPERTS
    P = n_blocks * EXPERT_ROWS
    slot_tok = jnp.full((P,), T, jnp.int32).at[dest].set(flat_tok[order])
    slot_w = jnp.zeros((P,), jnp.float32).at[dest].set(flat_w[order])
    block_e = jnp.clip(jnp.searchsorted(pend, jnp.arange(n_blocks) * EXPERT_ROWS, side="right"),
                       0, N_EXPERTS - 1).astype(jnp.int32)
    h_pad = jnp.concatenate([h, jnp.zeros((1, D), h.dtype)], axis=0)

    def one_block(args):
        tok, e = args
        return swiglu(h_pad[tok], w_gu[e], w_down[e])

    y = lax.map(one_block, (slot_tok.reshape(n_blocks, EXPERT_ROWS), block_e))
    y = y.reshape(P, D) * slot_w[:, None]
    return jax.ops.segment_sum(y, slot_tok, num_segments=T + 1)[:T]


def setup_inputs(seed: int = 0) -> dict:
    key = jax.random.key(seed)
    ks = jax.random.split(key, 20)
    L, D, E, F = DEPTH, D_MODEL, N_EXPERTS, EXPERT_FF
    nrm = jax.random.normal
    x = nrm(ks[0], (BATCH, SEQ, D), jnp.float32)
    c = nrm(ks[1], (BATCH, D), jnp.float32)
    positions = (jnp.arange(SEQ, dtype=jnp.int32)[None, :]
                 + jax.random.randint(ks[2], (BATCH, 1), 0, 1024, dtype=jnp.int32))
    w_ada = nrm(ks[3], (L, D, N_MOD * D), jnp.float32) * (0.5 * D ** -0.5)
    b_ada = 0.02 * nrm(ks[4], (L, N_MOD * D), jnp.float32)
    g_mix = 1.0 + 0.02 * nrm(ks[5], (L, D), jnp.float32)
    w_in = nrm(ks[6], (L, D, IN_COLS), jnp.float32) * D ** -0.5
    g_q = 1.0 + 0.02 * nrm(ks[7], (L, HEAD_DIM), jnp.float32)
    g_k = 1.0 + 0.02 * nrm(ks[8], (L, HEAD_DIM), jnp.float32)
    lam = 0.1 * nrm(ks[9], (L, 4, HEAD_DIM), jnp.float32)
    g_sub = 1.0 + 0.02 * nrm(ks[10], (L, DIFF_VDIM), jnp.float32)
    w_out = nrm(ks[11], (L, MIX_WIDTH, D), jnp.float32) * MIX_WIDTH ** -0.5
    g_ffn = 1.0 + 0.02 * nrm(ks[12], (L, D), jnp.float32)
    w_router = nrm(ks[13], (L, D, E), jnp.float32) * D ** -0.5
    e_bias = 0.01 * nrm(ks[14], (L, E), jnp.float32)
    w_sh_gu = nrm(ks[15], (L, D, 2 * SHARED_FF), jnp.float32) * D ** -0.5
    w_sh_down = nrm(ks[16], (L, SHARED_FF, D), jnp.float32) * SHARED_FF ** -0.5
    w_gu = nrm(ks[17], (L, E, D, 2 * F), jnp.float32) * D ** -0.5
    w_down = nrm(ks[18], (L, E, F, D), jnp.float32) * F ** -0.5
    return {"x": x, "c": c, "positions": positions, "w_ada": w_ada, "b_ada": b_ada,
            "g_mix": g_mix, "w_in": w_in, "g_q": g_q, "g_k": g_k, "lam": lam, "g_sub": g_sub,
            "w_out": w_out, "g_ffn": g_ffn, "w_router": w_router, "e_bias": e_bias,
            "w_sh_gu": w_sh_gu, "w_sh_down": w_sh_down, "w_gu": w_gu, "w_down": w_down}


def reference(x, c, positions, w_ada, b_ada, g_mix, w_in, g_q, g_k, lam, g_sub, w_out,
              g_ffn, w_router, e_bias, w_sh_gu, w_sh_down, w_gu, w_down):
    B, S, D = x.shape
    cos, sin = rope_tables(positions)
    cond = jax.nn.silu(c.astype(jnp.float32))
    offs = [int(v) for v in np.cumsum(IN_SIZES)[:-1]]
    r = x.astype(jnp.float32)
    for l in range(DEPTH):
        lambda_init = 0.8 - 0.6 * math.exp(-0.3 * l)
        mod = cond @ w_ada[l].astype(jnp.float32) + b_ada[l].astype(jnp.float32)
        sh_a, sc_a, gt_a, sh_m, sc_m, gt_m = jnp.split(mod[:, None, :], N_MOD, axis=-1)
        h = rms_norm(r, g_mix[l]) * (1.0 + sc_a) + sh_a
        proj = h @ w_in[l].astype(jnp.float32)
        dq, dk, dv, sq, sk, sv = jnp.split(proj, offs, axis=-1)
        o_diff = diff_attention(dq.reshape(B, S, DIFF_HEADS, 2, HEAD_DIM),
                                dk.reshape(B, S, DIFF_HEADS, 2, HEAD_DIM),
                                dv.reshape(B, S, DIFF_HEADS, DIFF_VDIM),
                                lam[l], g_q[l], g_k[l], g_sub[l], cos, sin, lambda_init)
        o_sb = stick_breaking(sq.reshape(B, S, SB_HEADS, HEAD_DIM),
                              sk.reshape(B, S, SB_HEADS, HEAD_DIM),
                              sv.reshape(B, S, SB_HEADS, HEAD_DIM))
        mixed = jnp.concatenate([o_diff, o_sb], axis=-1)
        r = r + gt_a * (mixed @ w_out[l].astype(jnp.float32))
        h2 = rms_norm(r, g_ffn[l]) * (1.0 + sc_m) + sh_m
        hf = h2.reshape(B * S, D)
        idx, wts = route(hf, w_router[l], e_bias[l])
        y = swiglu(hf, w_sh_gu[l], w_sh_down[l]) + routed_experts(hf, idx, wts, w_gu[l], w_down[l])
        r = r + gt_m * y.reshape(B, S, D)
    return r.astype(x.dtype)
```

```python
import functools
import math

import jax
import jax.numpy as jnp
from jax import lax
from jax.experimental import pallas as pl
from jax.experimental.pallas import tpu as pltpu

F32 = jnp.float32
BF16 = jnp.bfloat16
U32 = jnp.uint32
I32 = jnp.int32

HEAD_DIM = 64
HALF = HEAD_DIM // 2
ROPE_THETA = 10000.0
N_EXPERTS = 256
TOP_K = 8
N_GROUPS = 8
GROUP_SIZE = N_EXPERTS // N_GROUPS
TOPK_GROUPS = 4
EXPERT_FF = 256
ROUTED_SCALE = 2.5
EXPERT_ROWS = 128
NORM_EPS = 1e-6
N_MOD = 6
LAMBDA_INIT = 0.8 - 0.6 * math.exp(-0.3 * 0)
Q_SCALE = HEAD_DIM ** -0.5

TS = 512
TB = 256
TT = 256
HI_MASK = 0xFFFF0000

VMEM_LIMIT = 48 * 1024 * 1024


def _cparams(sem):
    return pltpu.CompilerParams(dimension_semantics=sem, vmem_limit_bytes=VMEM_LIMIT)


def _split_bf16(a):
    hi = a.astype(BF16)
    lo = (a - hi.astype(F32)).astype(BF16)
    return hi, lo


def _dot(a, b):
    return jnp.dot(a, b, preferred_element_type=F32)


def _dot_nt(a, b):
    return lax.dot_general(a, b, (((1,), (1,)), ((), ())), preferred_element_type=F32)


def _silu(x):
    return x / (1.0 + jnp.exp(-x))


def _pack_bf16_pair(a, b):
    ab = pltpu.bitcast(a.astype(BF16).astype(F32), U32)
    bb = pltpu.bitcast(b.astype(BF16).astype(F32), U32)
    return (ab & jnp.uint32(HI_MASK)) | (bb >> 16)


def _unpack_bf16_pair(p):
    a = pltpu.bitcast(p & jnp.uint32(HI_MASK), F32)
    b = pltpu.bitcast(p << 16, F32)
    return a, b


def _ada_kernel(c_ref, w_ref, b_ref, o_ref):
    cond = _silu(c_ref[...])
    ch, cl = _split_bf16(cond)
    wh, wl = _split_bf16(w_ref[...])
    o_ref[...] = _dot(ch, wh) + _dot(ch, wl) + _dot(cl, wh) + b_ref[...]


def _ada(c_pad, w_ada, b_ada):
    d, n = w_ada.shape
    tn = n // 4
    return pl.pallas_call(
        _ada_kernel,
        out_shape=jax.ShapeDtypeStruct((c_pad.shape[0], n), F32),
        grid=(n // tn,),
        in_specs=[pl.BlockSpec((c_pad.shape[0], d), lambda j: (0, 0)),
                  pl.BlockSpec((d, tn), lambda j: (0, j)),
                  pl.BlockSpec((1, tn), lambda j: (0, j))],
        out_specs=pl.BlockSpec((c_pad.shape[0], tn), lambda j: (0, j)),
        compiler_params=_cparams(("arbitrary",)),
        name="ada_mod",
    )(c_pad, w_ada, b_ada)


def _norm_rope_t(t, g_col, cos_t, sin_t):
    outs = []
    for i in range(t.shape[0] // HEAD_DIM):
        blk = t[i * HEAD_DIM:(i + 1) * HEAD_DIM, :]
        ms = jnp.mean(blk * blk, axis=0, keepdims=True)
        y = blk * lax.rsqrt(ms + NORM_EPS) * g_col
        y1 = y[:HALF]
        y2 = y[HALF:]
        outs.append(y1 * cos_t - y2 * sin_t)
        outs.append(y2 * cos_t + y1 * sin_t)
    return jnp.concatenate(outs, axis=0)


def _inproj_kernel(x_ref, mod_ref, g_ref, pos_ref, inv_ref, gq_ref, gk_ref, wt_ref, wsk_ref,
                   qd_ref, kd_ref, vd_ref, qs_ref, ks_ref, vs_ref):
    x = x_ref[0]
    ms = jnp.mean(x * x, axis=-1, keepdims=True)
    sh = mod_ref[0, 0:1, :]
    sc = mod_ref[0, 1:2, :]
    h = x * lax.rsqrt(ms + NORM_EPS) * g_ref[...] * (1.0 + sc) + sh
    hb = h.astype(BF16)
    w = qd_ref.shape[1]

    def proj_t(sec):
        return _dot_nt(wt_ref[sec * w:(sec + 1) * w, :], hb)

    ang = inv_ref[...] * pos_ref[0].astype(F32)
    cos_t = jnp.cos(ang)
    sin_t = jnp.sin(ang)
    q_t = _norm_rope_t(proj_t(0), gq_ref[...], cos_t, sin_t) * Q_SCALE
    qd_ref[0] = q_t.astype(BF16)
    k_t = _norm_rope_t(proj_t(1), gk_ref[...], cos_t, sin_t)
    kd_ref[0] = k_t.T.astype(BF16)
    v_t = proj_t(2).astype(BF16)
    s_t = proj_t(4).astype(BF16)
    nh = vd_ref.shape[1]
    hw = vd_ref.shape[3]
    for hh in range(nh):
        for jj in range(vd_ref.shape[2]):
            vd_ref[0, hh, jj] = v_t[hh * hw:(hh + 1) * hw, jj * TB:(jj + 1) * TB]
            vs_ref[0, hh, jj] = s_t[hh * hw:(hh + 1) * hw, jj * TB:(jj + 1) * TB]
    qs_ref[0] = (proj_t(3) * Q_SCALE).astype(BF16)
    ks_ref[0] = _dot(hb, wsk_ref[...]).astype(BF16)


def _inproj(x, mod, g_mix, pos3, inv, g_q, g_k, w_t, w_sk):
    b, s, d = x.shape
    w = w_sk.shape[1]
    nh = w // 128
    nsb = TS // TB
    const = lambda shape: pl.BlockSpec(shape, lambda bi, si: (0,) * len(shape))
    t_spec = pl.BlockSpec((1, w, TS), lambda bi, si: (bi, 0, si))
    n_spec = pl.BlockSpec((1, TS, w), lambda bi, si: (bi, si, 0))
    v_spec = pl.BlockSpec((1, nh, nsb, 128, TB), lambda bi, si: (bi, 0, si, 0, 0))
    t_shape = jax.ShapeDtypeStruct((b, w, s), BF16)
    n_shape = jax.ShapeDtypeStruct((b, s, w), BF16)
    v_shape = jax.ShapeDtypeStruct((b, nh, s // TB, 128, TB), BF16)
    return pl.pallas_call(
        _inproj_kernel,
        out_shape=(t_shape, n_shape, v_shape, t_shape, n_shape, v_shape),
        grid=(b, s // TS),
        in_specs=[pl.BlockSpec((1, TS, d), lambda bi, si: (bi, si, 0)),
                  pl.BlockSpec((1, N_MOD, d), lambda bi, si: (bi, 0, 0)),
                  const((1, d)),
                  pl.BlockSpec((1, 1, TS), lambda bi, si: (bi, 0, si)),
                  const((HALF, 1)), const((HEAD_DIM, 1)), const((HEAD_DIM, 1)),
                  const(w_t.shape), const(w_sk.shape)],
        out_specs=(t_spec, n_spec, v_spec, t_spec, n_spec, v_spec),
        compiler_params=_cparams(("parallel", "parallel")),
        name="norm_inproj",
    )(x, mod, g_mix, pos3, inv, g_q, g_k, w_t, w_sk)


def _q_pair(q_t):
    row = lax.broadcasted_iota(I32, q_t.shape, 0)
    zero = jnp.zeros_like(q_t)
    return jnp.concatenate([jnp.where(row < HEAD_DIM, q_t, zero),
                            jnp.where(row >= HEAD_DIM, q_t, zero)], axis=1)


def _key_query_iotas(tk, tq):
    r = lax.broadcasted_iota(I32, (tk, 2 * tq), 0)
    c = lax.broadcasted_iota(I32, (tk, 2 * tq), 1)
    return r, jnp.where(c >= tq, c - tq, c)


def _diff_kernel(qt_ref, k_ref, vt_ref, lam_ref, gsub_ref, o_ref):
    qi = pl.program_id(2)
    tq = qt_ref.shape[2]
    qcat = _q_pair(qt_ref[0])

    def step(j, carry, diagonal):
        m, l, acc = carry
        kj = k_ref[0, pl.ds(pl.multiple_of(j * TB, TB), TB), :]
        s = _dot(kj, qcat)
        if diagonal:
            r, c = _key_query_iotas(TB, tq)
            s = jnp.where(r <= c, s, -jnp.inf)
        m_new = jnp.maximum(m, jnp.max(s, axis=0, keepdims=True))
        alpha = jnp.exp(m - m_new)
        p = jnp.exp(s - m_new)
        l = l * alpha + jnp.sum(p, axis=0, keepdims=True)
        acc = acc * alpha + _dot(vt_ref[0, 0, j], p.astype(BF16))
        return m_new, l, acc

    init = (jnp.full((1, 2 * tq), -jnp.inf, F32), jnp.zeros((1, 2 * tq), F32),
            jnp.zeros((2 * HEAD_DIM, 2 * tq), F32))
    carry = lax.fori_loop(0, qi, lambda j, c: step(j, c, False), init)
    _, l, acc = step(qi, carry, True)

    lam = lam_ref[...]
    l1 = jnp.sum(lam[0:1] * lam[1:2], axis=-1, keepdims=True)
    l2 = jnp.sum(lam[2:3] * lam[3:4], axis=-1, keepdims=True)
    lmbda = jnp.exp(l1) - jnp.exp(l2) + LAMBDA_INIT
    o = acc[:, :tq] / l[:, :tq] - lmbda * (acc[:, tq:] / l[:, tq:])
    ms = jnp.mean(o * o, axis=0, keepdims=True)
    y = o * lax.rsqrt(ms + NORM_EPS) * gsub_ref[...] * (1.0 - LAMBDA_INIT)
    o_ref[0] = y.T.astype(BF16)


def _diff_attention(q_t, k, v_t, lam, g_sub):
    b, w, s = q_t.shape
    nh = w // 128
    return pl.pallas_call(
        _diff_kernel,
        out_shape=jax.ShapeDtypeStruct((b, s, w), BF16),
        grid=(b, nh, s // TB),
        in_specs=[pl.BlockSpec((1, 128, TB), lambda bi, hi, qi: (bi, hi, qi)),
                  pl.BlockSpec((1, s, 128), lambda bi, hi, qi: (bi, 0, hi)),
                  pl.BlockSpec((1, 1, s // TB, 128, TB), lambda bi, hi, qi: (bi, hi, 0, 0, 0)),
                  pl.BlockSpec(lam.shape, lambda bi, hi, qi: (0, 0)),
                  pl.BlockSpec(g_sub.shape, lambda bi, hi, qi: (0, 0))],
        out_specs=pl.BlockSpec((1, TB, 128), lambda bi, hi, qi: (bi, qi, hi)),
        compiler_params=_cparams(("parallel", "parallel", "parallel")),
        name="diff_attention",
    )(q_t, k, v_t, lam, g_sub)


def _sb_kernel(qt_ref, k_ref, vt_ref, o_ref):
    qi = pl.program_id(2)
    tq = qt_ref.shape[2]
    qcat = _q_pair(qt_ref[0])
    ur = lax.broadcasted_iota(I32, (TB, 2 * TB), 0)
    uc = lax.broadcasted_iota(I32, (TB, 2 * TB), 1)
    uc = jnp.where(uc >= TB, uc - TB, uc)
    u2 = jnp.where(uc > ur, 1.0, 0.0).astype(BF16)

    def step(j, carry, diagonal):
        c_run, acc = carry
        kj = k_ref[0, pl.ds(pl.multiple_of(j * TB, TB), TB), :]
        z = _dot(kj, qcat)
        sp = jnp.maximum(z, 0.0) + jnp.log(1.0 + jnp.exp(-jnp.abs(z)))
        lk = -sp
        if diagonal:
            r, c = _key_query_iotas(TB, tq)
            valid = r < c
            lk = jnp.where(valid, lk, 0.0)
        hi, lo = _split_bf16(lk)
        later = _dot(u2, jnp.concatenate([hi, lo], axis=0))
        w = jnp.exp(z - sp + later + c_run)
        if diagonal:
            w = jnp.where(valid, w, 0.0)
        wb = w.astype(BF16)
        vj = vt_ref[0, 0, j]
        acc = acc + jnp.concatenate([_dot(vj[:HEAD_DIM], wb[:, :tq]),
                                     _dot(vj[HEAD_DIM:], wb[:, tq:])], axis=0)
        c_run = c_run + jnp.sum(lk, axis=0, keepdims=True)
        return c_run, acc

    init = (jnp.zeros((1, 2 * tq), F32), jnp.zeros((2 * HEAD_DIM, tq), F32))
    carry = step(qi, init, True)
    _, acc = lax.fori_loop(0, qi, lambda i, c: step(qi - 1 - i, c, False), carry)
    o_ref[0] = acc.T.astype(BF16)


def _sb_attention(q_t, k, v_t):
    b, w, s = q_t.shape
    nh = w // 128
    return pl.pallas_call(
        _sb_kernel,
        out_shape=jax.ShapeDtypeStruct((b, s, w), BF16),
        grid=(b, nh, s // TB),
        in_specs=[pl.BlockSpec((1, 128, TB), lambda bi, hi, qi: (bi, hi, qi)),
                  pl.BlockSpec((1, s, 128), lambda bi, hi, qi: (bi, 0, hi)),
                  pl.BlockSpec((1, 1, s // TB, 128, TB), lambda bi, hi, qi: (bi, hi, 0, 0, 0))],
        out_specs=pl.BlockSpec((1, TB, 128), lambda bi, hi, qi: (bi, qi, hi)),
        compiler_params=_cparams(("parallel", "parallel", "parallel")),
        name="sb_attention",
    )(q_t, k, v_t)


def _first_index_of_max(v, idx, size):
    m = jnp.max(v, axis=0, keepdims=True)
    first = jnp.min(jnp.where(v == m, idx, size), axis=0, keepdims=True)
    return m, first


def _route_kernel(md_ref, ms_ref, x_ref, mod_ref, wo_ref, g_ref, wrh_ref, wrl_ref, eb_ref,
                  r1_ref, h2p_ref, idx_ref, wn_ref, rank_ref, cnt_ref, base_ref):
    i = pl.program_id(0)

    @pl.when(i == 0)
    def _():
        base_ref[...] = jnp.zeros_like(base_ref)

    half = md_ref.shape[1]
    attn = _dot(md_ref[...], wo_ref[:half, :]) + _dot(ms_ref[...], wo_ref[half:, :])
    r1 = x_ref[...] + mod_ref[0, 2:3, :] * attn
    r1_ref[...] = r1
    ms = jnp.mean(r1 * r1, axis=-1, keepdims=True)
    h2 = r1 * lax.rsqrt(ms + NORM_EPS) * g_ref[...] * (1.0 + mod_ref[0, 4:5, :]) + mod_ref[0, 3:4, :]
    d = h2.shape[1]
    h2p_ref[...] = _pack_bf16_pair(h2[:, :d // 2], h2[:, d // 2:])

    hh, hl = _split_bf16(h2)
    wh = wrh_ref[...]
    logits = _dot_nt(wh, hh) + _dot_nt(wh, hl) + _dot_nt(wrl_ref[...], hh)
    scores = 1.0 / (1.0 + jnp.exp(-logits))
    biased = scores + eb_ref[...]
    tt = scores.shape[1]

    gi = lax.broadcasted_iota(I32, (GROUP_SIZE, tt), 0)
    gs = []
    for g in range(N_GROUPS):
        blk = biased[g * GROUP_SIZE:(g + 1) * GROUP_SIZE, :]
        m1, first = _first_index_of_max(blk, gi, GROUP_SIZE)
        m2 = jnp.max(jnp.where(gi == first, -jnp.inf, blk), axis=0, keepdims=True)
        gs.append(m1 + m2)
    gscore = jnp.concatenate(gs, axis=0)
    gidx = lax.broadcasted_iota(I32, (N_GROUPS, tt), 0)
    gsel = jnp.zeros((N_GROUPS, tt), F32)
    for _ in range(TOPK_GROUPS):
        _, first = _first_index_of_max(gscore, gidx, N_GROUPS)
        hit = gidx == first
        gsel = jnp.where(hit, 1.0, gsel)
        gscore = jnp.where(hit, -jnp.inf, gscore)
    cand = jnp.concatenate(
        [jnp.where(gsel[g:g + 1, :] > 0.0, biased[g * GROUP_SIZE:(g + 1) * GROUP_SIZE, :], -jnp.inf)
         for g in range(N_GROUPS)], axis=0)

    ei = lax.broadcasted_iota(I32, (N_EXPERTS, tt), 0)
    picked = jnp.zeros((N_EXPERTS, tt), F32)
    idxs, ws, hits = [], [], []
    for _ in range(TOP_K):
        _, first = _first_index_of_max(cand, ei, N_EXPERTS)
        hit = ei == first
        idxs.append(first)
        ws.append(jnp.sum(jnp.where(hit, scores, 0.0), axis=0, keepdims=True))
        hits.append(hit)
        picked = jnp.where(hit, 1.0, picked)
        cand = jnp.where(hit, -jnp.inf, cand)
    idx_ref[...] = jnp.concatenate(idxs, axis=0)
    w_t = jnp.concatenate(ws, axis=0)
    w_t = w_t / jnp.sum(w_t, axis=0, keepdims=True) * ROUTED_SCALE
    pad = jnp.zeros((128 - TOP_K, tt), F32)
    wn_ref[...] = jnp.concatenate([w_t, pad], axis=0).T[:, :TOP_K]

    tr = lax.broadcasted_iota(I32, (tt, tt), 0)
    tc = lax.broadcasted_iota(I32, (tt, tt), 1)
    before = jnp.where(tr < tc, 1.0, 0.0).astype(BF16)
    seen = _dot(picked.astype(BF16), before) + base_ref[...]
    rank_ref[...] = jnp.concatenate(
        [jnp.sum(jnp.where(h, seen, 0.0), axis=0, keepdims=True) for h in hits], axis=0).astype(I32)
    base_ref[...] += jnp.sum(picked, axis=1, keepdims=True)
    cnt_ref[...] = base_ref[...].astype(I32)


def _outproj_route(mixed_d, mixed_s, x2, mod, w_out, g_ffn, wr_hi, wr_lo, e_bias, s):
    t, d = x2.shape
    half = mixed_d.shape[1]
    per_b = s // TT
    const = lambda shape: pl.BlockSpec(shape, lambda i: (0,) * len(shape))
    tok = lambda wd: pl.BlockSpec((TT, wd), lambda i: (i, 0))
    lane = pl.BlockSpec((TOP_K, TT), lambda i: (0, i))
    return pl.pallas_call(
        _route_kernel,
        out_shape=(jax.ShapeDtypeStruct((t, d), F32),
                   jax.ShapeDtypeStruct((t, d // 2), U32),
                   jax.ShapeDtypeStruct((TOP_K, t), I32),
                   jax.ShapeDtypeStruct((t, TOP_K), F32),
                   jax.ShapeDtypeStruct((TOP_K, t), I32),
                   jax.ShapeDtypeStruct((N_EXPERTS, 1), I32)),
        grid=(t // TT,),
        in_specs=[tok(half), tok(half), tok(d),
                  pl.BlockSpec((1, N_MOD, d), lambda i: (i // per_b, 0, 0)),
                  const(w_out.shape), const((1, d)), const(wr_hi.shape), const(wr_lo.shape),
                  const((N_EXPERTS, 1))],
        out_specs=(tok(d), tok(d // 2), lane, tok(TOP_K), lane, const((N_EXPERTS, 1))),
        scratch_shapes=[pltpu.VMEM((N_EXPERTS, 1), F32)],
        compiler_params=_cparams(("arbitrary",)),
        name="outproj_route",
    )(mixed_d, mixed_s, x2, mod, w_out, g_ffn, wr_hi, wr_lo, e_bias)


def _pos_kernel(idx_ref, rank_ref, pstart_ref, pos_ref):
    tt = idx_ref.shape[1]
    ei = lax.broadcasted_iota(I32, (N_EXPERTS, tt), 0)
    ps = pstart_ref[...]
    rows = []
    for k in range(TOP_K):
        hit = ei == idx_ref[k:k + 1, :]
        rows.append(jnp.sum(jnp.where(hit, ps, 0), axis=0, keepdims=True))
    pos_ref[...] = jnp.concatenate(rows, axis=0) + rank_ref[...]


def _slot_positions(idx_t, rank_t, pstart):
    k, t = idx_t.shape
    tp = 1024
    lane = pl.BlockSpec((k, tp), lambda i: (0, i))
    return pl.pallas_call(
        _pos_kernel,
        out_shape=jax.ShapeDtypeStruct((k, t), I32),
        grid=(t // tp,),
        in_specs=[lane, lane, pl.BlockSpec((N_EXPERTS, 1), lambda i: (0, 0))],
        out_specs=lane,
        compiler_params=_cparams(("parallel",)),
        name="slot_positions",
    )(idx_t, rank_t, pstart)


def _expert_kernel(be_ref, nu_ref, x_ref, wgu_ref, wd_ref, y_ref, wgu_b, wd_b):
    b = pl.program_id(0)

    @pl.when(b < nu_ref[0])
    def _():
        prev = be_ref[jnp.maximum(b - 1, 0)]

        @pl.when((b == 0) | (be_ref[b] != prev))
        def _():
            wgu_b[...] = wgu_ref[0].astype(BF16)
            wd_b[...] = wd_ref[0].astype(BF16)

        xa, xb = _unpack_bf16_pair(x_ref[...])
        x = jnp.concatenate([xa, xb], axis=1).astype(BF16)
        gu = _dot(x, wgu_b[...])
        ff = gu.shape[1] // 2
        act = _silu(gu[:, :ff]) * gu[:, ff:]
        y = _dot(act.astype(BF16), wd_b[...])
        dh = y.shape[1] // 2
        y_ref[...] = _pack_bf16_pair(y[:, :dh], y[:, dh:])

    @pl.when(b >= nu_ref[0])
    def _():
        y_ref[...] = jnp.zeros_like(y_ref)


def _experts(block_e, n_used, xs, w_gu, w_down):
    p, dp = xs.shape
    n_blocks = p // EXPERT_ROWS
    _, d, f2 = w_gu.shape
    _, f, _ = w_down.shape
    grid_spec = pltpu.PrefetchScalarGridSpec(
        num_scalar_prefetch=2,
        grid=(n_blocks,),
        in_specs=[pl.BlockSpec((EXPERT_ROWS, dp), lambda b, be, nu: (jnp.minimum(b, nu[0] - 1), 0)),
                  pl.BlockSpec((1, d, f2), lambda b, be, nu: (be[b], 0, 0)),
                  pl.BlockSpec((1, f, d), lambda b, be, nu: (be[b], 0, 0))],
        out_specs=pl.BlockSpec((EXPERT_ROWS, dp), lambda b, be, nu: (b, 0)),
        scratch_shapes=[pltpu.VMEM((d, f2), BF16), pltpu.VMEM((f, d), BF16)],
    )
    return pl.pallas_call(
        _expert_kernel,
        out_shape=jax.ShapeDtypeStruct((p, dp), U32),
        grid_spec=grid_spec,
        compiler_params=_cparams(("arbitrary",)),
        name="routed_experts",
    )(block_e, n_used, xs, w_gu, w_down)


def _final_kernel(r1_ref, h2p_ref, yg_ref, wn_ref, mod_ref, wsg_ref, wsd_ref, o_ref):
    ha, hb = _unpack_bf16_pair(h2p_ref[...])
    h2 = jnp.concatenate([ha, hb], axis=1).astype(BF16)
    gu = _dot(h2, wsg_ref[...])
    ff = gu.shape[1] // 2
    y = _dot((_silu(gu[:, :ff]) * gu[:, ff:]).astype(BF16), wsd_ref[...])
    wn = wn_ref[...]
    for k in range(TOP_K):
        ya, yb = _unpack_bf16_pair(yg_ref[k])
        y = y + wn[:, k:k + 1] * jnp.concatenate([ya, yb], axis=1)
    o_ref[...] = r1_ref[...] + mod_ref[0, 5:6, :] * y


def _final(r1, h2p, yg, w_nat, mod, w_sh_gu, w_sh_down, s):
    t, d = r1.shape
    per_b = s // TT
    const = lambda shape: pl.BlockSpec(shape, lambda i: (0,) * len(shape))
    tok = lambda wd: pl.BlockSpec((TT, wd), lambda i: (i, 0))
    return pl.pallas_call(
        _final_kernel,
        out_shape=jax.ShapeDtypeStruct((t, d), F32),
        grid=(t // TT,),
        in_specs=[tok(d), tok(d // 2),
                  pl.BlockSpec((TOP_K, TT, d // 2), lambda i: (0, i, 0)),
                  tok(TOP_K),
                  pl.BlockSpec((1, N_MOD, d), lambda i: (i // per_b, 0, 0)),
                  const(w_sh_gu.shape), const(w_sh_down.shape)],
        out_specs=tok(d),
        compiler_params=_cparams(("parallel",)),
        name="shared_combine",
    )(r1, h2p, yg, w_nat, mod, w_sh_gu, w_sh_down)


def _dispatch(h2p, pos_t, p):
    t = h2p.shape[0]
    src = jnp.tile(h2p, (TOP_K, 1))
    return jnp.zeros((p, h2p.shape[1]), U32).at[pos_t.reshape(-1)].set(src)


def _gather_rows(y, pos_t):
    k, t = pos_t.shape
    return y[pos_t.reshape(-1)].reshape(k, t, y.shape[1])


def kernel(x, c, positions, w_ada, b_ada, g_mix, w_in, g_q, g_k, lam, g_sub, w_out, g_ffn, w_router,
           e_bias, w_sh_gu, w_sh_down, w_gu, w_down):
    b, s, d = x.shape
    assert w_ada.shape[0] == 1 and s % TS == 0 and d % 256 == 0
    t = b * s
    wdt = w_in.shape[2] // 6

    c_pad = jnp.zeros((8, d), F32).at[:b].set(c.astype(F32))
    mod = _ada(c_pad, w_ada[0], b_ada)[:b].reshape(b, N_MOD, d)

    sec = [w_in[0][:, i * wdt:(i + 1) * wdt] for i in range(6)]
    w_t = jnp.concatenate([sec[0], sec[1], sec[2], sec[3], sec[5]], axis=1).T.astype(BF16)
    w_sk = sec[4].astype(BF16)
    inv = (ROPE_THETA ** (-jnp.arange(0, HEAD_DIM, 2, dtype=F32) / HEAD_DIM)).reshape(HALF, 1)
    pos3 = positions.astype(I32).reshape(b, 1, s)

    qd, kd, vd, qs, ks, vs = _inproj(x, mod, g_mix, pos3, inv, g_q.reshape(HEAD_DIM, 1),
                                     g_k.reshape(HEAD_DIM, 1), w_t, w_sk)
    mixed_d = _diff_attention(qd, kd, vd, lam[0], g_sub.reshape(2 * HEAD_DIM, 1))
    mixed_s = _sb_attention(qs, ks, vs)

    wr_t = w_router[0].T
    wr_hi = wr_t.astype(BF16)
    wr_lo = (wr_t - wr_hi.astype(F32)).astype(BF16)
    r1, h2p, idx_t, w_nat, rank_t, counts = _outproj_route(
        mixed_d.reshape(t, wdt), mixed_s.reshape(t, wdt), x.reshape(t, d), mod,
        w_out[0].astype(BF16), g_ffn, wr_hi, wr_lo, e_bias.reshape(N_EXPERTS, 1), s)

    counts = counts.reshape(N_EXPERTS)
    padded = (counts + EXPERT_ROWS - 1) // EXPERT_ROWS * EXPERT_ROWS
    pend = jnp.cumsum(padded)
    pstart = (pend - padded).astype(I32)
    n_blocks = t * TOP_K // EXPERT_ROWS + N_EXPERTS
    block_e = jnp.clip(jnp.searchsorted(pend, jnp.arange(n_blocks) * EXPERT_ROWS, side="right"),
                       0, N_EXPERTS - 1).astype(I32)
    n_used = (pend[-1:] // EXPERT_ROWS).astype(I32)
    pos_t = _slot_positions(idx_t, rank_t, pstart.reshape(N_EXPERTS, 1))

    xs = _dispatch(h2p, pos_t, n_blocks * EXPERT_ROWS)
    y = _experts(block_e, n_used, xs, w_gu[0], w_down[0])
    yg = _gather_rows(y, pos_t)
    out = _final(r1, h2p, yg, w_nat, mod, w_sh_gu[0].astype(BF16), w_sh_down[0].astype(BF16), s)
    return out.reshape(b, s, d).astype(x.dtype)
```

```python
import functools
import math

import jax
import jax.numpy as jnp
from jax import lax
from jax.experimental import pallas as pl
from jax.experimental.pallas import tpu as pltpu
from jax.experimental.pallas import tpu_sc as plsc

F32 = jnp.float32
BF16 = jnp.bfloat16
U32 = jnp.uint32
I32 = jnp.int32

HEAD_DIM = 64
HALF = HEAD_DIM // 2
ROPE_THETA = 10000.0
N_EXPERTS = 256
TOP_K = 8
N_GROUPS = 8
GROUP_SIZE = N_EXPERTS // N_GROUPS
TOPK_GROUPS = 4
EXPERT_FF = 256
ROUTED_SCALE = 2.5
EXPERT_ROWS = 128
NORM_EPS = 1e-6
N_MOD = 6
LAMBDA_INIT = 0.8 - 0.6 * math.exp(-0.3 * 0)
Q_SCALE = HEAD_DIM ** -0.5

TS = 512
TB = 256
TT = 256
HI_MASK = 0xFFFF0000
SC_WINDOW = 128

VMEM_LIMIT = 48 * 1024 * 1024


def _cparams(sem):
    return pltpu.CompilerParams(dimension_semantics=sem, vmem_limit_bytes=VMEM_LIMIT)


def _split_bf16(a):
    hi = a.astype(BF16)
    lo = (a - hi.astype(F32)).astype(BF16)
    return hi, lo


def _dot(a, b):
    return jnp.dot(a, b, preferred_element_type=F32)


def _dot_nt(a, b):
    return lax.dot_general(a, b, (((1,), (1,)), ((), ())), preferred_element_type=F32)


def _silu(x):
    return x / (1.0 + jnp.exp(-x))


def _pack_bf16_pair(a, b):
    ab = pltpu.bitcast(a.astype(BF16).astype(F32), U32)
    bb = pltpu.bitcast(b.astype(BF16).astype(F32), U32)
    return (ab & jnp.uint32(HI_MASK)) | (bb >> 16)


def _unpack_bf16_pair(p):
    a = pltpu.bitcast(p & jnp.uint32(HI_MASK), F32)
    b = pltpu.bitcast(p << 16, F32)
    return a, b


def _pack_rows(v):
    q = v.shape[1] // 4
    return (_pack_bf16_pair(v[:, :q], v[:, q:2 * q]),
            _pack_bf16_pair(v[:, 2 * q:3 * q], v[:, 3 * q:]))


def _unpack_rows(pa, pb):
    return jnp.concatenate(_unpack_bf16_pair(pa) + _unpack_bf16_pair(pb), axis=1)


def _ada_kernel(c_ref, w_ref, b_ref, o_ref):
    cond = _silu(c_ref[...])
    ch, cl = _split_bf16(cond)
    wh, wl = _split_bf16(w_ref[...])
    o_ref[...] = _dot(ch, wh) + _dot(ch, wl) + _dot(cl, wh) + b_ref[...]


def _ada(c_pad, w_ada, b_ada):
    d, n = w_ada.shape
    tn = n // 4
    return pl.pallas_call(
        _ada_kernel,
        out_shape=jax.ShapeDtypeStruct((c_pad.shape[0], n), F32),
        grid=(n // tn,),
        in_specs=[pl.BlockSpec((c_pad.shape[0], d), lambda j: (0, 0)),
                  pl.BlockSpec((d, tn), lambda j: (0, j)),
                  pl.BlockSpec((1, tn), lambda j: (0, j))],
        out_specs=pl.BlockSpec((c_pad.shape[0], tn), lambda j: (0, j)),
        compiler_params=_cparams(("arbitrary",)),
        name="ada_mod",
    )(c_pad, w_ada, b_ada)


def _norm_rope_t(t, g_col, cos_t, sin_t):
    outs = []
    for i in range(t.shape[0] // HEAD_DIM):
        blk = t[i * HEAD_DIM:(i + 1) * HEAD_DIM, :]
        ms = jnp.mean(blk * blk, axis=0, keepdims=True)
        y = blk * lax.rsqrt(ms + NORM_EPS) * g_col
        y1 = y[:HALF]
        y2 = y[HALF:]
        outs.append(y1 * cos_t - y2 * sin_t)
        outs.append(y2 * cos_t + y1 * sin_t)
    return jnp.concatenate(outs, axis=0)


def _inproj_kernel(x_ref, mod_ref, g_ref, pos_ref, inv_ref, gq_ref, gk_ref, wt_ref, wsk_ref,
                   qd_ref, kd_ref, vd_ref, qs_ref, ks_ref, vs_ref):
    x = x_ref[0]
    ms = jnp.mean(x * x, axis=-1, keepdims=True)
    sh = mod_ref[0, 0:1, :]
    sc = mod_ref[0, 1:2, :]
    h = x * lax.rsqrt(ms + NORM_EPS) * g_ref[...] * (1.0 + sc) + sh
    hb = h.astype(BF16)
    w = qd_ref.shape[1]

    def proj_t(sec):
        return _dot_nt(wt_ref[sec * w:(sec + 1) * w, :], hb)

    ang = inv_ref[...] * pos_ref[0].astype(F32)
    cos_t = jnp.cos(ang)
    sin_t = jnp.sin(ang)
    q_t = _norm_rope_t(proj_t(0), gq_ref[...], cos_t, sin_t) * Q_SCALE
    qd_ref[0] = q_t.astype(BF16)
    k_t = _norm_rope_t(proj_t(1), gk_ref[...], cos_t, sin_t)
    kd_ref[0] = k_t.T.astype(BF16)
    v_t = proj_t(2).astype(BF16)
    s_t = proj_t(4).astype(BF16)
    nh = vd_ref.shape[1]
    hw = vd_ref.shape[3]
    for hh in range(nh):
        for jj in range(vd_ref.shape[2]):
            vd_ref[0, hh, jj] = v_t[hh * hw:(hh + 1) * hw, jj * TB:(jj + 1) * TB]
            vs_ref[0, hh, jj] = s_t[hh * hw:(hh + 1) * hw, jj * TB:(jj + 1) * TB]
    qs_ref[0] = (proj_t(3) * Q_SCALE).astype(BF16)
    ks_ref[0] = _dot(hb, wsk_ref[...]).astype(BF16)


def _inproj(x, mod, g_mix, pos3, inv, g_q, g_k, w_t, w_sk):
    b, s, d = x.shape
    w = w_sk.shape[1]
    nh = w // 128
    nsb = TS // TB
    const = lambda shape: pl.BlockSpec(shape, lambda bi, si: (0,) * len(shape))
    t_spec = pl.BlockSpec((1, w, TS), lambda bi, si: (bi, 0, si))
    n_spec = pl.BlockSpec((1, TS, w), lambda bi, si: (bi, si, 0))
    v_spec = pl.BlockSpec((1, nh, nsb, 128, TB), lambda bi, si: (bi, 0, si, 0, 0))
    t_shape = jax.ShapeDtypeStruct((b, w, s), BF16)
    n_shape = jax.ShapeDtypeStruct((b, s, w), BF16)
    v_shape = jax.ShapeDtypeStruct((b, nh, s // TB, 128, TB), BF16)
    return pl.pallas_call(
        _inproj_kernel,
        out_shape=(t_shape, n_shape, v_shape, t_shape, n_shape, v_shape),
        grid=(b, s // TS),
        in_specs=[pl.BlockSpec((1, TS, d), lambda bi, si: (bi, si, 0)),
                  pl.BlockSpec((1, N_MOD, d), lambda bi, si: (bi, 0, 0)),
                  const((1, d)),
                  pl.BlockSpec((1, 1, TS), lambda bi, si: (bi, 0, si)),
                  const((HALF, 1)), const((HEAD_DIM, 1)), const((HEAD_DIM, 1)),
                  const(w_t.shape), const(w_sk.shape)],
        out_specs=(t_spec, n_spec, v_spec, t_spec, n_spec, v_spec),
        compiler_params=_cparams(("parallel", "parallel")),
        name="norm_inproj",
    )(x, mod, g_mix, pos3, inv, g_q, g_k, w_t, w_sk)


def _q_pair(q_t):
    row = lax.broadcasted_iota(I32, q_t.shape, 0)
    zero = jnp.zeros_like(q_t)
    return jnp.concatenate([jnp.where(row < HEAD_DIM, q_t, zero),
                            jnp.where(row >= HEAD_DIM, q_t, zero)], axis=1)


def _key_query_iotas(tk, tq):
    r = lax.broadcasted_iota(I32, (tk, 2 * tq), 0)
    c = lax.broadcasted_iota(I32, (tk, 2 * tq), 1)
    return r, jnp.where(c >= tq, c - tq, c)


def _diff_kernel(qt_ref, k_ref, vt_ref, lam_ref, gsub_ref, o_ref):
    qi = pl.program_id(2)
    tq = qt_ref.shape[2]
    qcat = _q_pair(qt_ref[0])

    def step(j, carry, diagonal):
        m, l, acc = carry
        kj = k_ref[0, pl.ds(pl.multiple_of(j * TB, TB), TB), :]
        s = _dot(kj, qcat)
        if diagonal:
            r, c = _key_query_iotas(TB, tq)
            s = jnp.where(r <= c, s, -jnp.inf)
        m_new = jnp.maximum(m, jnp.max(s, axis=0, keepdims=True))
        alpha = jnp.exp(m - m_new)
        p = jnp.exp(s - m_new)
        l = l * alpha + jnp.sum(p, axis=0, keepdims=True)
        acc = acc * alpha + _dot(vt_ref[0, 0, j], p.astype(BF16))
        return m_new, l, acc

    init = (jnp.full((1, 2 * tq), -jnp.inf, F32), jnp.zeros((1, 2 * tq), F32),
            jnp.zeros((2 * HEAD_DIM, 2 * tq), F32))
    carry = lax.fori_loop(0, qi, lambda j, c: step(j, c, False), init)
    _, l, acc = step(qi, carry, True)

    lam = lam_ref[...]
    l1 = jnp.sum(lam[0:1] * lam[1:2], axis=-1, keepdims=True)
    l2 = jnp.sum(lam[2:3] * lam[3:4], axis=-1, keepdims=True)
    lmbda = jnp.exp(l1) - jnp.exp(l2) + LAMBDA_INIT
    o = acc[:, :tq] / l[:, :tq] - lmbda * (acc[:, tq:] / l[:, tq:])
    ms = jnp.mean(o * o, axis=0, keepdims=True)
    y = o * lax.rsqrt(ms + NORM_EPS) * gsub_ref[...] * (1.0 - LAMBDA_INIT)
    o_ref[0] = y.T.astype(BF16)


def _diff_attention(q_t, k, v_t, lam, g_sub):
    b, w, s = q_t.shape
    nh = w // 128
    return pl.pallas_call(
        _diff_kernel,
        out_shape=jax.ShapeDtypeStruct((b, s, w), BF16),
        grid=(b, nh, s // TB),
        in_specs=[pl.BlockSpec((1, 128, TB), lambda bi, hi, qi: (bi, hi, qi)),
                  pl.BlockSpec((1, s, 128), lambda bi, hi, qi: (bi, 0, hi)),
                  pl.BlockSpec((1, 1, s // TB, 128, TB), lambda bi, hi, qi: (bi, hi, 0, 0, 0)),
                  pl.BlockSpec(lam.shape, lambda bi, hi, qi: (0, 0)),
                  pl.BlockSpec(g_sub.shape, lambda bi, hi, qi: (0, 0))],
        out_specs=pl.BlockSpec((1, TB, 128), lambda bi, hi, qi: (bi, qi, hi)),
        compiler_params=_cparams(("parallel", "parallel", "parallel")),
        name="diff_attention",
    )(q_t, k, v_t, lam, g_sub)


def _sb_kernel(qt_ref, k_ref, vt_ref, o_ref):
    qi = pl.program_id(2)
    tq = qt_ref.shape[2]
    qcat = _q_pair(qt_ref[0])
    ur = lax.broadcasted_iota(I32, (TB, 2 * TB), 0)
    uc = lax.broadcasted_iota(I32, (TB, 2 * TB), 1)
    uc = jnp.where(uc >= TB, uc - TB, uc)
    u2 = jnp.where(uc > ur, 1.0, 0.0).astype(BF16)

    def step(j, carry, diagonal):
        c_run, acc = carry
        kj = k_ref[0, pl.ds(pl.multiple_of(j * TB, TB), TB), :]
        z = _dot(kj, qcat)
        sp = jnp.maximum(z, 0.0) + jnp.log(1.0 + jnp.exp(-jnp.abs(z)))
        lk = -sp
        if diagonal:
            r, c = _key_query_iotas(TB, tq)
            valid = r < c
            lk = jnp.where(valid, lk, 0.0)
        hi, lo = _split_bf16(lk)
        later = _dot(u2, jnp.concatenate([hi, lo], axis=0))
        w = jnp.exp(z - sp + later + c_run)
        if diagonal:
            w = jnp.where(valid, w, 0.0)
        wb = w.astype(BF16)
        vj = vt_ref[0, 0, j]
        acc = acc + jnp.concatenate([_dot(vj[:HEAD_DIM], wb[:, :tq]),
                                     _dot(vj[HEAD_DIM:], wb[:, tq:])], axis=0)
        c_run = c_run + jnp.sum(lk, axis=0, keepdims=True)
        return c_run, acc

    init = (jnp.zeros((1, 2 * tq), F32), jnp.zeros((2 * HEAD_DIM, tq), F32))
    carry = step(qi, init, True)
    _, acc = lax.fori_loop(0, qi, lambda i, c: step(qi - 1 - i, c, False), carry)
    o_ref[0] = acc.T.astype(BF16)


def _sb_attention(q_t, k, v_t):
    b, w, s = q_t.shape
    nh = w // 128
    return pl.pallas_call(
        _sb_kernel,
        out_shape=jax.ShapeDtypeStruct((b, s, w), BF16),
        grid=(b, nh, s // TB),
        in_specs=[pl.BlockSpec((1, 128, TB), lambda bi, hi, qi: (bi, hi, qi)),
                  pl.BlockSpec((1, s, 128), lambda bi, hi, qi: (bi, 0, hi)),
                  pl.BlockSpec((1, 1, s // TB, 128, TB), lambda bi, hi, qi: (bi, hi, 0, 0, 0))],
        out_specs=pl.BlockSpec((1, TB, 128), lambda bi, hi, qi: (bi, qi, hi)),
        compiler_params=_cparams(("parallel", "parallel", "parallel")),
        name="sb_attention",
    )(q_t, k, v_t)


def _first_index_of_max(v, idx, size):
    m = jnp.max(v, axis=0, keepdims=True)
    first = jnp.min(jnp.where(v == m, idx, size), axis=0, keepdims=True)
    return m, first


def _route_kernel(md_ref, ms_ref, x_ref, mod_ref, wo_ref, g_ref, wrh_ref, wrl_ref, eb_ref,
                  r1_ref, h2a_ref, h2b_ref, idx_ref, wn_ref, rank_ref, cnt_ref, base_ref):
    i = pl.program_id(0)

    @pl.when(i == 0)
    def _():
        base_ref[...] = jnp.zeros_like(base_ref)

    half = md_ref.shape[1]
    attn = _dot(md_ref[...], wo_ref[:half, :]) + _dot(ms_ref[...], wo_ref[half:, :])
    r1 = x_ref[...] + mod_ref[0, 2:3, :] * attn
    r1_ref[...] = r1
    ms = jnp.mean(r1 * r1, axis=-1, keepdims=True)
    h2 = r1 * lax.rsqrt(ms + NORM_EPS) * g_ref[...] * (1.0 + mod_ref[0, 4:5, :]) + mod_ref[0, 3:4, :]
    h2a_ref[...], h2b_ref[...] = _pack_rows(h2)

    hh, hl = _split_bf16(h2)
    wh = wrh_ref[...]
    logits = _dot_nt(wh, hh) + _dot_nt(wh, hl) + _dot_nt(wrl_ref[...], hh)
    scores = 1.0 / (1.0 + jnp.exp(-logits))
    biased = scores + eb_ref[...]
    tt = scores.shape[1]

    gi = lax.broadcasted_iota(I32, (GROUP_SIZE, tt), 0)
    gs = []
    for g in range(N_GROUPS):
        blk = biased[g * GROUP_SIZE:(g + 1) * GROUP_SIZE, :]
        m1, first = _first_index_of_max(blk, gi, GROUP_SIZE)
        m2 = jnp.max(jnp.where(gi == first, -jnp.inf, blk), axis=0, keepdims=True)
        gs.append(m1 + m2)
    gscore = jnp.concatenate(gs, axis=0)
    gidx = lax.broadcasted_iota(I32, (N_GROUPS, tt), 0)
    gsel = jnp.zeros((N_GROUPS, tt), F32)
    for _ in range(TOPK_GROUPS):
        _, first = _first_index_of_max(gscore, gidx, N_GROUPS)
        hit = gidx == first
        gsel = jnp.where(hit, 1.0, gsel)
        gscore = jnp.where(hit, -jnp.inf, gscore)
    cand = jnp.concatenate(
        [jnp.where(gsel[g:g + 1, :] > 0.0, biased[g * GROUP_SIZE:(g + 1) * GROUP_SIZE, :], -jnp.inf)
         for g in range(N_GROUPS)], axis=0)

    ei = lax.broadcasted_iota(I32, (N_EXPERTS, tt), 0)
    picked = jnp.zeros((N_EXPERTS, tt), F32)
    idxs, ws, hits = [], [], []
    for _ in range(TOP_K):
        _, first = _first_index_of_max(cand, ei, N_EXPERTS)
        hit = ei == first
        idxs.append(first)
        ws.append(jnp.sum(jnp.where(hit, scores, 0.0), axis=0, keepdims=True))
        hits.append(hit)
        picked = jnp.where(hit, 1.0, picked)
        cand = jnp.where(hit, -jnp.inf, cand)
    idx_ref[...] = jnp.concatenate(idxs, axis=0)
    w_t = jnp.concatenate(ws, axis=0)
    w_t = w_t / jnp.sum(w_t, axis=0, keepdims=True) * ROUTED_SCALE
    pad = jnp.zeros((128 - TOP_K, tt), F32)
    wn_ref[...] = jnp.concatenate([w_t, pad], axis=0).T[:, :TOP_K]

    tr = lax.broadcasted_iota(I32, (tt, tt), 0)
    tc = lax.broadcasted_iota(I32, (tt, tt), 1)
    before = jnp.where(tr < tc, 1.0, 0.0).astype(BF16)
    seen = _dot(picked.astype(BF16), before) + base_ref[...]
    rank_ref[...] = jnp.concatenate(
        [jnp.sum(jnp.where(h, seen, 0.0), axis=0, keepdims=True) for h in hits], axis=0).astype(I32)
    base_ref[...] += jnp.sum(picked, axis=1, keepdims=True)
    cnt_ref[...] = base_ref[...].astype(I32)


def _outproj_route(mixed_d, mixed_s, x2, mod, w_out, g_ffn, wr_hi, wr_lo, e_bias, s):
    t, d = x2.shape
    half = mixed_d.shape[1]
    per_b = s // TT
    const = lambda shape: pl.BlockSpec(shape, lambda i: (0,) * len(shape))
    tok = lambda wd: pl.BlockSpec((TT, wd), lambda i: (i, 0))
    lane = pl.BlockSpec((TOP_K, TT), lambda i: (0, i))
    return pl.pallas_call(
        _route_kernel,
        out_shape=(jax.ShapeDtypeStruct((t, d), F32),
                   jax.ShapeDtypeStruct((t, d // 4), U32),
                   jax.ShapeDtypeStruct((t, d // 4), U32),
                   jax.ShapeDtypeStruct((TOP_K, t), I32),
                   jax.ShapeDtypeStruct((t, TOP_K), F32),
                   jax.ShapeDtypeStruct((TOP_K, t), I32),
                   jax.ShapeDtypeStruct((N_EXPERTS, 1), I32)),
        grid=(t // TT,),
        in_specs=[tok(half), tok(half), tok(d),
                  pl.BlockSpec((1, N_MOD, d), lambda i: (i // per_b, 0, 0)),
                  const(w_out.shape), const((1, d)), const(wr_hi.shape), const(wr_lo.shape),
                  const((N_EXPERTS, 1))],
        out_specs=(tok(d), tok(d // 4), tok(d // 4), lane, tok(TOP_K), lane, const((N_EXPERTS, 1))),
        scratch_shapes=[pltpu.VMEM((N_EXPERTS, 1), F32)],
        compiler_params=_cparams(("arbitrary",)),
        name="outproj_route",
    )(mixed_d, mixed_s, x2, mod, w_out, g_ffn, wr_hi, wr_lo, e_bias)


def _pos_kernel(idx_ref, rank_ref, pstart_ref, pos_ref):
    tt = idx_ref.shape[1]
    ei = lax.broadcasted_iota(I32, (N_EXPERTS, tt), 0)
    ps = pstart_ref[...]
    rows = []
    for k in range(TOP_K):
        hit = ei == idx_ref[k:k + 1, :]
        rows.append(jnp.sum(jnp.where(hit, ps, 0), axis=0, keepdims=True))
    pos_ref[...] = jnp.concatenate(rows, axis=0) + rank_ref[...]


def _slot_positions(idx_t, rank_t, pstart):
    k, t = idx_t.shape
    tp = 1024
    lane = pl.BlockSpec((k, tp), lambda i: (0, i))
    return pl.pallas_call(
        _pos_kernel,
        out_shape=jax.ShapeDtypeStruct((k, t), I32),
        grid=(t // tp,),
        in_specs=[lane, lane, pl.BlockSpec((N_EXPERTS, 1), lambda i: (0, 0))],
        out_specs=lane,
        compiler_params=_cparams(("parallel",)),
        name="slot_positions",
    )(idx_t, rank_t, pstart)


def _expert_kernel(be_ref, nu_ref, xa_ref, xb_ref, wgu_ref, wd_ref, ya_ref, yb_ref, wgu_b, wd_b):
    b = pl.program_id(0)

    @pl.when(b < nu_ref[0])
    def _():
        prev = be_ref[jnp.maximum(b - 1, 0)]

        @pl.when((b == 0) | (be_ref[b] != prev))
        def _():
            wgu_b[...] = wgu_ref[0].astype(BF16)
            wd_b[...] = wd_ref[0].astype(BF16)

        x = _unpack_rows(xa_ref[...], xb_ref[...]).astype(BF16)
        gu = _dot(x, wgu_b[...])
        ff = gu.shape[1] // 2
        act = _silu(gu[:, :ff]) * gu[:, ff:]
        y = _dot(act.astype(BF16), wd_b[...])
        ya_ref[...], yb_ref[...] = _pack_rows(y)

    @pl.when(b >= nu_ref[0])
    def _():
        ya_ref[...] = jnp.zeros_like(ya_ref)
        yb_ref[...] = jnp.zeros_like(yb_ref)


def _experts(block_e, n_used, xs_a, xs_b, w_gu, w_down):
    p, dp = xs_a.shape
    n_blocks = p // EXPERT_ROWS
    _, d, f2 = w_gu.shape
    _, f, _ = w_down.shape
    grid_spec = pltpu.PrefetchScalarGridSpec(
        num_scalar_prefetch=2,
        grid=(n_blocks,),
        in_specs=[pl.BlockSpec((EXPERT_ROWS, dp), lambda b, be, nu: (jnp.minimum(b, nu[0] - 1), 0)),
                  pl.BlockSpec((EXPERT_ROWS, dp), lambda b, be, nu: (jnp.minimum(b, nu[0] - 1), 0)),
                  pl.BlockSpec((1, d, f2), lambda b, be, nu: (be[b], 0, 0)),
                  pl.BlockSpec((1, f, d), lambda b, be, nu: (be[b], 0, 0))],
        out_specs=(pl.BlockSpec((EXPERT_ROWS, dp), lambda b, be, nu: (b, 0)),
                   pl.BlockSpec((EXPERT_ROWS, dp), lambda b, be, nu: (b, 0))),
        scratch_shapes=[pltpu.VMEM((d, f2), BF16), pltpu.VMEM((f, d), BF16)],
    )
    return pl.pallas_call(
        _expert_kernel,
        out_shape=(jax.ShapeDtypeStruct((p, dp), U32), jax.ShapeDtypeStruct((p, dp), U32)),
        grid_spec=grid_spec,
        compiler_params=_cparams(("arbitrary",)),
        name="routed_experts",
    )(block_e, n_used, xs_a, xs_b, w_gu, w_down)


def _final_kernel(r1_ref, h2a_ref, h2b_ref, yga_ref, ygb_ref, wn_ref, mod_ref, wsg_ref, wsd_ref, o_ref):
    h2 = _unpack_rows(h2a_ref[...], h2b_ref[...]).astype(BF16)
    gu = _dot(h2, wsg_ref[...])
    ff = gu.shape[1] // 2
    y = _dot((_silu(gu[:, :ff]) * gu[:, ff:]).astype(BF16), wsd_ref[...])
    wn = wn_ref[...]
    for k in range(TOP_K):
        y = y + wn[:, k:k + 1] * _unpack_rows(yga_ref[k], ygb_ref[k])
    o_ref[...] = r1_ref[...] + mod_ref[0, 5:6, :] * y


def _final(r1, h2a, h2b, yg_a, yg_b, w_nat, mod, w_sh_gu, w_sh_down, s):
    t, d = r1.shape
    per_b = s // TT
    const = lambda shape: pl.BlockSpec(shape, lambda i: (0,) * len(shape))
    tok = lambda wd: pl.BlockSpec((TT, wd), lambda i: (i, 0))
    return pl.pallas_call(
        _final_kernel,
        out_shape=jax.ShapeDtypeStruct((t, d), F32),
        grid=(t // TT,),
        in_specs=[tok(d), tok(d // 4), tok(d // 4),
                  pl.BlockSpec((TOP_K, TT, d // 4), lambda i: (0, i, 0)),
                  pl.BlockSpec((TOP_K, TT, d // 4), lambda i: (0, i, 0)),
                  tok(TOP_K),
                  pl.BlockSpec((1, N_MOD, d), lambda i: (i // per_b, 0, 0)),
                  const(w_sh_gu.shape), const(w_sh_down.shape)],
        out_specs=tok(d),
        compiler_params=_cparams(("parallel",)),
        name="shared_combine",
    )(r1, h2a, h2b, yg_a, yg_b, w_nat, mod, w_sh_gu, w_sh_down)


def _sc_mesh():
    return plsc.VectorSubcoreMesh(core_axis_name="c", subcore_axis_name="s")


def _sc_dispatch(xa, xb, pos_t, p):
    t, dc = xa.shape
    k = pos_t.shape[0]
    out = jax.ShapeDtypeStruct((p, dc), xa.dtype)

    @functools.partial(pl.kernel, out_type=(out, out), mesh=_sc_mesh())
    def kern(xa_hbm, xb_hbm, pos_hbm, oa_hbm, ob_hbm):
        for src, dst in ((xa_hbm, oa_hbm), (xb_hbm, ob_hbm)):
            def body(x_vmem, i_vmem, dst=dst):
                pltpu.sync_copy(x_vmem, dst.at[i_vmem.at[0]])

            pltpu.emit_pipeline(
                body,
                grid=(t // SC_WINDOW, k),
                in_specs=[pl.BlockSpec((SC_WINDOW, dc), lambda i, kk: (i, 0)),
                          pl.BlockSpec((1, SC_WINDOW), lambda i, kk: (kk, i))],
                out_specs=[],
                core_axis_name=("c", "s"),
                dimension_semantics=(pltpu.PARALLEL, pltpu.ARBITRARY),
            )(src, pos_hbm)

    return kern(xa, xb, pos_t)


def _sc_gather(ya, yb, pos_row):
    n = pos_row.shape[1]
    dc = ya.shape[1]
    out = jax.ShapeDtypeStruct((n, dc), ya.dtype)

    @functools.partial(pl.kernel, out_type=(out, out), mesh=_sc_mesh())
    def kern(ya_hbm, yb_hbm, pos_hbm, oa_hbm, ob_hbm):
        for src, dst in ((ya_hbm, oa_hbm), (yb_hbm, ob_hbm)):
            def body(i_vmem, o_vmem, src=src):
                pltpu.sync_copy(src.at[i_vmem.at[0]], o_vmem)

            pltpu.emit_pipeline(
                body,
                grid=(n // SC_WINDOW,),
                in_specs=[pl.BlockSpec((1, SC_WINDOW), lambda i: (0, i))],
                out_specs=[pl.BlockSpec((SC_WINDOW, dc), lambda i: (i, 0))],
                core_axis_name=("c", "s"),
                dimension_semantics=(pltpu.PARALLEL,),
            )(pos_hbm, dst)

    return kern(ya, yb, pos_row)


def kernel(x, c, positions, w_ada, b_ada, g_mix, w_in, g_q, g_k, lam, g_sub, w_out, g_ffn, w_router,
           e_bias, w_sh_gu, w_sh_down, w_gu, w_down):
    b, s, d = x.shape
    assert w_ada.shape[0] == 1 and s % TS == 0 and d % 256 == 0
    t = b * s
    wdt = w_in.shape[2] // 6

    c_pad = jnp.zeros((8, d), F32).at[:b].set(c.astype(F32))
    mod = _ada(c_pad, w_ada[0], b_ada)[:b].reshape(b, N_MOD, d)

    sec = [w_in[0][:, i * wdt:(i + 1) * wdt] for i in range(6)]
    w_t = jnp.concatenate([sec[0], sec[1], sec[2], sec[3], sec[5]], axis=1).T.astype(BF16)
    w_sk = sec[4].astype(BF16)
    inv = (ROPE_THETA ** (-jnp.arange(0, HEAD_DIM, 2, dtype=F32) / HEAD_DIM)).reshape(HALF, 1)
    pos3 = positions.astype(I32).reshape(b, 1, s)

    qd, kd, vd, qs, ks, vs = _inproj(x, mod, g_mix, pos3, inv, g_q.reshape(HEAD_DIM, 1),
                                     g_k.reshape(HEAD_DIM, 1), w_t, w_sk)
    mixed_d = _diff_attention(qd, kd, vd, lam[0], g_sub.reshape(2 * HEAD_DIM, 1))
    mixed_s = _sb_attention(qs, ks, vs)

    wr_t = w_router[0].T
    wr_hi = wr_t.astype(BF16)
    wr_lo = (wr_t - wr_hi.astype(F32)).astype(BF16)
    r1, h2a, h2b, idx_t, w_nat, rank_t, counts = _outproj_route(
        mixed_d.reshape(t, wdt), mixed_s.reshape(t, wdt), x.reshape(t, d), mod,
        w_out[0].astype(BF16), g_ffn, wr_hi, wr_lo, e_bias.reshape(N_EXPERTS, 1), s)

    counts = counts.reshape(N_EXPERTS)
    padded = (counts + EXPERT_ROWS - 1) // EXPERT_ROWS * EXPERT_ROWS
    pend = jnp.cumsum(padded)
    pstart = (pend - padded).astype(I32)
    n_blocks = t * TOP_K // EXPERT_ROWS + N_EXPERTS
    block_row0 = jnp.arange(n_blocks, dtype=pend.dtype) * EXPERT_ROWS
    block_e = jnp.minimum(jnp.sum(pend[None, :] <= block_row0[:, None], axis=1), N_EXPERTS - 1).astype(I32)
    n_used = (pend[-1:] // EXPERT_ROWS).astype(I32)
    pos_t = _slot_positions(idx_t, rank_t, pstart.reshape(N_EXPERTS, 1))

    xs_a, xs_b = _sc_dispatch(h2a, h2b, pos_t, n_blocks * EXPERT_ROWS)
    y_a, y_b = _experts(block_e, n_used, xs_a, xs_b, w_gu[0], w_down[0])
    yg_a, yg_b = _sc_gather(y_a, y_b, pos_t.reshape(1, TOP_K * t))
    dq4 = d // 4
    out = _final(r1, h2a, h2b, yg_a.reshape(TOP_K, t, dq4), yg_b.reshape(TOP_K, t, dq4), w_nat, mod,
                 w_sh_gu[0].astype(BF16), w_sh_down[0].astype(BF16), s)
    return out.reshape(b, s, d).astype(x.dtype)
```

```python
import functools
import math

import jax
import jax.numpy as jnp
from jax import lax
from jax.experimental import pallas as pl
from jax.experimental.pallas import tpu as pltpu
from jax.experimental.pallas import tpu_sc as plsc

F32 = jnp.float32
BF16 = jnp.bfloat16
U32 = jnp.uint32
I32 = jnp.int32

HEAD_DIM = 64
HALF = HEAD_DIM // 2
ROPE_THETA = 10000.0
N_EXPERTS = 256
TOP_K = 8
N_GROUPS = 8
GROUP_SIZE = N_EXPERTS // N_GROUPS
TOPK_GROUPS = 4
EXPERT_FF = 256
ROUTED_SCALE = 2.5
EXPERT_ROWS = 128
EXPERT_SUBS = 4
STEP_ROWS = EXPERT_SUBS * EXPERT_ROWS
NORM_EPS = 1e-6
SB_LOG_FLOOR = -110.0
NSTREAM = 2
N_MOD = 6
LAMBDA_INIT = 0.8 - 0.6 * math.exp(-0.3 * 0)
Q_SCALE = HEAD_DIM ** -0.5

TS = 512
TB = 256
TT = 256
HI_MASK = 0xFFFF0000
SC_WINDOW = 128

VMEM_LIMIT = 48 * 1024 * 1024


def _cparams(sem):
    return pltpu.CompilerParams(dimension_semantics=sem, vmem_limit_bytes=VMEM_LIMIT)


def _split_bf16(a):
    hi = a.astype(BF16)
    lo = (a - hi.astype(F32)).astype(BF16)
    return hi, lo


def _dot(a, b):
    return jnp.dot(a, b, preferred_element_type=F32)


def _dot_nt(a, b):
    return lax.dot_general(a, b, (((1,), (1,)), ((), ())), preferred_element_type=F32)


def _silu(x):
    return x / (1.0 + jnp.exp(-x))


def _pack_bf16_pair(a, b):
    ab = pltpu.bitcast(a.astype(BF16).astype(F32), U32)
    bb = pltpu.bitcast(b.astype(BF16).astype(F32), U32)
    return (ab & jnp.uint32(HI_MASK)) | (bb >> 16)


def _unpack_bf16_pair(p):
    a = pltpu.bitcast(p & jnp.uint32(HI_MASK), F32)
    b = pltpu.bitcast(p << 16, F32)
    return a, b


def _pack_rows(v):
    q = v.shape[1] // 4
    return (_pack_bf16_pair(v[:, :q], v[:, q:2 * q]),
            _pack_bf16_pair(v[:, 2 * q:3 * q], v[:, 3 * q:]))


def _unpack_rows(pa, pb):
    return jnp.concatenate(_unpack_bf16_pair(pa) + _unpack_bf16_pair(pb), axis=1)


def _ada_kernel(c_ref, w_ref, b_ref, o_ref):
    cond = _silu(c_ref[...])
    ch, cl = _split_bf16(cond)
    wh, wl = _split_bf16(w_ref[...])
    o_ref[...] = _dot(ch, wh) + _dot(ch, wl) + _dot(cl, wh) + b_ref[...]


def _ada(c_pad, w_ada, b_ada):
    d, n = w_ada.shape
    tn = n // 4
    return pl.pallas_call(
        _ada_kernel,
        out_shape=jax.ShapeDtypeStruct((c_pad.shape[0], n), F32),
        grid=(n // tn,),
        in_specs=[pl.BlockSpec((c_pad.shape[0], d), lambda j: (0, 0)),
                  pl.BlockSpec((d, tn), lambda j: (0, j)),
                  pl.BlockSpec((1, tn), lambda j: (0, j))],
        out_specs=pl.BlockSpec((c_pad.shape[0], tn), lambda j: (0, j)),
        compiler_params=_cparams(("arbitrary",)),
        name="ada_mod",
    )(c_pad, w_ada, b_ada)


def _norm_rope_t(t, g_col, cos_t, sin_t):
    outs = []
    for i in range(t.shape[0] // HEAD_DIM):
        blk = t[i * HEAD_DIM:(i + 1) * HEAD_DIM, :]
        ms = jnp.mean(blk * blk, axis=0, keepdims=True)
        y = blk * lax.rsqrt(ms + NORM_EPS) * g_col
        y1 = y[:HALF]
        y2 = y[HALF:]
        outs.append(y1 * cos_t - y2 * sin_t)
        outs.append(y2 * cos_t + y1 * sin_t)
    return jnp.concatenate(outs, axis=0)


def _inproj_kernel(x_ref, mod_ref, g_ref, pos_ref, inv_ref, gq_ref, gk_ref, wt_ref, wsk_ref,
                   qd_ref, kd_ref, vd_ref, qs_ref, ks_ref, vs_ref):
    x = x_ref[0]
    ms = jnp.mean(x * x, axis=-1, keepdims=True)
    sh = mod_ref[0, 0:1, :]
    sc = mod_ref[0, 1:2, :]
    h = x * lax.rsqrt(ms + NORM_EPS) * g_ref[...] * (1.0 + sc) + sh
    hb = h.astype(BF16)
    w = qd_ref.shape[1]

    def proj_t(sec):
        return _dot_nt(wt_ref[sec * w:(sec + 1) * w, :], hb)

    ang = inv_ref[...] * pos_ref[0].astype(F32)
    cos_t = jnp.cos(ang)
    sin_t = jnp.sin(ang)
    q_t = _norm_rope_t(proj_t(0), gq_ref[...], cos_t, sin_t) * Q_SCALE
    qd_ref[0] = q_t.astype(BF16)
    k_t = _norm_rope_t(proj_t(1), gk_ref[...], cos_t, sin_t)
    kd_ref[0] = k_t.T.astype(BF16)
    v_t = proj_t(2).astype(BF16)
    s_t = proj_t(4).astype(BF16)
    nh = vd_ref.shape[1]
    hw = vd_ref.shape[3]
    for hh in range(nh):
        for jj in range(vd_ref.shape[2]):
            vd_ref[0, hh, jj] = v_t[hh * hw:(hh + 1) * hw, jj * TB:(jj + 1) * TB]
            vs_ref[0, hh, jj] = s_t[hh * hw:(hh + 1) * hw, jj * TB:(jj + 1) * TB]
    qs_ref[0] = (proj_t(3) * Q_SCALE).astype(BF16)
    ks_ref[0] = _dot(hb, wsk_ref[...]).astype(BF16)


def _inproj(x, mod, g_mix, pos3, inv, g_q, g_k, w_t, w_sk):
    b, s, d = x.shape
    w = w_sk.shape[1]
    nh = w // 128
    nsb = TS // TB
    const = lambda shape: pl.BlockSpec(shape, lambda bi, si: (0,) * len(shape))
    t_spec = pl.BlockSpec((1, w, TS), lambda bi, si: (bi, 0, si))
    n_spec = pl.BlockSpec((1, TS, w), lambda bi, si: (bi, si, 0))
    v_spec = pl.BlockSpec((1, nh, nsb, 128, TB), lambda bi, si: (bi, 0, si, 0, 0))
    t_shape = jax.ShapeDtypeStruct((b, w, s), BF16)
    n_shape = jax.ShapeDtypeStruct((b, s, w), BF16)
    v_shape = jax.ShapeDtypeStruct((b, nh, s // TB, 128, TB), BF16)
    return pl.pallas_call(
        _inproj_kernel,
        out_shape=(t_shape, n_shape, v_shape, t_shape, n_shape, v_shape),
        grid=(b, s // TS),
        in_specs=[pl.BlockSpec((1, TS, d), lambda bi, si: (bi, si, 0)),
                  pl.BlockSpec((1, N_MOD, d), lambda bi, si: (bi, 0, 0)),
                  const((1, d)),
                  pl.BlockSpec((1, 1, TS), lambda bi, si: (bi, 0, si)),
                  const((HALF, 1)), const((HEAD_DIM, 1)), const((HEAD_DIM, 1)),
                  const(w_t.shape), const(w_sk.shape)],
        out_specs=(t_spec, n_spec, v_spec, t_spec, n_spec, v_spec),
        compiler_params=_cparams(("parallel", "parallel")),
        name="norm_inproj",
    )(x, mod, g_mix, pos3, inv, g_q, g_k, w_t, w_sk)


def _q_pair(q_t):
    row = lax.broadcasted_iota(I32, q_t.shape, 0)
    zero = jnp.zeros_like(q_t)
    return jnp.concatenate([jnp.where(row < HEAD_DIM, q_t, zero),
                            jnp.where(row >= HEAD_DIM, q_t, zero)], axis=1)


def _key_query_iotas(tk, tq):
    r = lax.broadcasted_iota(I32, (tk, 2 * tq), 0)
    c = lax.broadcasted_iota(I32, (tk, 2 * tq), 1)
    return r, jnp.where(c >= tq, c - tq, c)


def _diff_kernel(qt_ref, k_ref, vt_ref, lam_ref, gsub_ref, o_ref):
    qi = pl.program_id(2)
    tq = qt_ref.shape[2]
    qcats = [_q_pair(qt_ref[0, st * 128:(st + 1) * 128, :]) for st in range(NSTREAM)]

    def step(j, carry, diagonal):
        row0 = pl.multiple_of(j * TB, TB)
        out = []
        for st in range(NSTREAM):
            m, l, acc = carry[st]
            kj = k_ref[0, pl.ds(row0, TB), st * 128:(st + 1) * 128]
            s = _dot(kj, qcats[st])
            if diagonal:
                r, c = _key_query_iotas(TB, tq)
                s = jnp.where(r <= c, s, -jnp.inf)
            m_new = jnp.maximum(m, jnp.max(s, axis=0, keepdims=True))
            alpha = jnp.exp(m - m_new)
            p = jnp.exp(s - m_new)
            l = l * alpha + jnp.sum(p, axis=0, keepdims=True)
            acc = acc * alpha + _dot(vt_ref[0, st, j], p.astype(BF16))
            out.append((m_new, l, acc))
        return tuple(out)

    init = tuple((jnp.full((1, 2 * tq), -jnp.inf, F32), jnp.zeros((1, 2 * tq), F32),
                  jnp.zeros((2 * HEAD_DIM, 2 * tq), F32)) for _ in range(NSTREAM))
    carry = lax.fori_loop(0, qi, lambda j, c: step(j, c, False), init)
    carry = step(qi, carry, True)

    lam = lam_ref[...]
    l1 = jnp.sum(lam[0:1] * lam[1:2], axis=-1, keepdims=True)
    l2 = jnp.sum(lam[2:3] * lam[3:4], axis=-1, keepdims=True)
    lmbda = jnp.exp(l1) - jnp.exp(l2) + LAMBDA_INIT
    for st in range(NSTREAM):
        _, l, acc = carry[st]
        o = acc[:, :tq] / l[:, :tq] - lmbda * (acc[:, tq:] / l[:, tq:])
        ms = jnp.mean(o * o, axis=0, keepdims=True)
        y = o * lax.rsqrt(ms + NORM_EPS) * gsub_ref[...] * (1.0 - LAMBDA_INIT)
        o_ref[0, :, st * 128:(st + 1) * 128] = y.T.astype(BF16)


def _attention_specs(b, w, s):
    gw = NSTREAM * 128
    in_specs = [pl.BlockSpec((1, gw, TB), lambda bi, gi, qi: (bi, gi, qi)),
                pl.BlockSpec((1, s, gw), lambda bi, gi, qi: (bi, 0, gi)),
                pl.BlockSpec((1, NSTREAM, s // TB, 128, TB), lambda bi, gi, qi: (bi, gi, 0, 0, 0))]
    out_spec = pl.BlockSpec((1, TB, gw), lambda bi, gi, qi: (bi, qi, gi))
    return (b, w // gw, s // TB), in_specs, out_spec


def _diff_attention(q_t, k, v_t, lam, g_sub):
    b, w, s = q_t.shape
    grid, in_specs, out_spec = _attention_specs(b, w, s)
    return pl.pallas_call(
        _diff_kernel,
        out_shape=jax.ShapeDtypeStruct((b, s, w), BF16),
        grid=grid,
        in_specs=in_specs + [pl.BlockSpec(lam.shape, lambda bi, gi, qi: (0, 0)),
                             pl.BlockSpec(g_sub.shape, lambda bi, gi, qi: (0, 0))],
        out_specs=out_spec,
        compiler_params=_cparams(("parallel", "parallel", "parallel")),
        name="diff_attention",
    )(q_t, k, v_t, lam, g_sub)


def _sb_kernel(qt_ref, k_ref, vt_ref, o_ref):
    qi = pl.program_id(2)
    tq = qt_ref.shape[2]
    qcats = [_q_pair(qt_ref[0, st * 128:(st + 1) * 128, :]) for st in range(NSTREAM)]
    ur = lax.broadcasted_iota(I32, (TB, 2 * TB), 0)
    uc = lax.broadcasted_iota(I32, (TB, 2 * TB), 1)
    uc = jnp.where(uc >= TB, uc - TB, uc)
    u2 = jnp.where(uc > ur, 1.0, 0.0).astype(BF16)

    def step(j, carry, diagonal):
        row0 = pl.multiple_of(j * TB, TB)
        out = []
        for st in range(NSTREAM):
            c_run, acc = carry[st]
            kj = k_ref[0, pl.ds(row0, TB), st * 128:(st + 1) * 128]
            z = _dot(kj, qcats[st])
            sp = jnp.maximum(z, 0.0) + jnp.log(1.0 + jnp.exp(-jnp.abs(z)))
            lk = -sp
            if diagonal:
                r, c = _key_query_iotas(TB, tq)
                valid = r < c
                lk = jnp.where(valid, lk, 0.0)
            hi, lo = _split_bf16(lk)
            later = _dot(u2, jnp.concatenate([hi, lo], axis=0))
            w = jnp.exp(z - sp + later + c_run)
            if diagonal:
                w = jnp.where(valid, w, 0.0)
            wb = w.astype(BF16)
            vj = vt_ref[0, st, j]
            acc = acc + jnp.concatenate([_dot(vj[:HEAD_DIM], wb[:, :tq]),
                                         _dot(vj[HEAD_DIM:], wb[:, tq:])], axis=0)
            c_run = c_run + jnp.sum(lk, axis=0, keepdims=True)
            out.append((c_run, acc))
        return tuple(out)

    def live(carry):
        c_max = functools.reduce(jnp.maximum, [jnp.max(cr) for cr, _ in carry])
        return c_max > SB_LOG_FLOOR

    init = tuple((jnp.zeros((1, 2 * tq), F32), jnp.zeros((2 * HEAD_DIM, tq), F32)) for _ in range(NSTREAM))
    carry = step(qi, init, True)
    _, carry = lax.while_loop(lambda sc: (sc[0] < qi) & live(sc[1]),
                              lambda sc: (sc[0] + 1, step(qi - 1 - sc[0], sc[1], False)),
                              (jnp.int32(0), carry))
    for st in range(NSTREAM):
        o_ref[0, :, st * 128:(st + 1) * 128] = carry[st][1].T.astype(BF16)


def _sb_attention(q_t, k, v_t):
    b, w, s = q_t.shape
    grid, in_specs, out_spec = _attention_specs(b, w, s)
    return pl.pallas_call(
        _sb_kernel,
        out_shape=jax.ShapeDtypeStruct((b, s, w), BF16),
        grid=grid,
        in_specs=in_specs,
        out_specs=out_spec,
        compiler_params=_cparams(("parallel", "parallel", "parallel")),
        name="sb_attention",
    )(q_t, k, v_t)


def _first_index_of_max(v, idx, size):
    m = jnp.max(v, axis=0, keepdims=True)
    first = jnp.min(jnp.where(v == m, idx, size), axis=0, keepdims=True)
    return m, first


def _route_kernel(md_ref, ms_ref, x_ref, mod_ref, wo_ref, g_ref, wrh_ref, wrl_ref, eb_ref,
                  r1_ref, h2a_ref, h2b_ref, idx_ref, wn_ref, rank_ref, cnt_ref, base_ref):
    i = pl.program_id(0)

    @pl.when(i == 0)
    def _():
        base_ref[...] = jnp.zeros_like(base_ref)

    half = md_ref.shape[1]
    attn = _dot(md_ref[...], wo_ref[:half, :]) + _dot(ms_ref[...], wo_ref[half:, :])
    r1 = x_ref[...] + mod_ref[0, 2:3, :] * attn
    r1_ref[...] = r1
    ms = jnp.mean(r1 * r1, axis=-1, keepdims=True)
    h2 = r1 * lax.rsqrt(ms + NORM_EPS) * g_ref[...] * (1.0 + mod_ref[0, 4:5, :]) + mod_ref[0, 3:4, :]
    h2a_ref[...], h2b_ref[...] = _pack_rows(h2)

    hh, hl = _split_bf16(h2)
    wh = wrh_ref[...]
    logits = _dot_nt(wh, hh) + _dot_nt(wh, hl) + _dot_nt(wrl_ref[...], hh)
    scores = 1.0 / (1.0 + jnp.exp(-logits))
    biased = scores + eb_ref[...]
    tt = scores.shape[1]

    gi = lax.broadcasted_iota(I32, (GROUP_SIZE, tt), 0)
    gs = []
    for g in range(N_GROUPS):
        blk = biased[g * GROUP_SIZE:(g + 1) * GROUP_SIZE, :]
        m1, first = _first_index_of_max(blk, gi, GROUP_SIZE)
        m2 = jnp.max(jnp.where(gi == first, -jnp.inf, blk), axis=0, keepdims=True)
        gs.append(m1 + m2)
    gscore = jnp.concatenate(gs, axis=0)
    gidx = lax.broadcasted_iota(I32, (N_GROUPS, tt), 0)
    gsel = jnp.zeros((N_GROUPS, tt), F32)
    for _ in range(TOPK_GROUPS):
        _, first = _first_index_of_max(gscore, gidx, N_GROUPS)
        hit = gidx == first
        gsel = jnp.where(hit, 1.0, gsel)
        gscore = jnp.where(hit, -jnp.inf, gscore)
    cand = jnp.concatenate(
        [jnp.where(gsel[g:g + 1, :] > 0.0, biased[g * GROUP_SIZE:(g + 1) * GROUP_SIZE, :], -jnp.inf)
         for g in range(N_GROUPS)], axis=0)

    ei = lax.broadcasted_iota(I32, (N_EXPERTS, tt), 0)
    picked = jnp.zeros((N_EXPERTS, tt), F32)
    idxs, ws, hits = [], [], []
    for _ in range(TOP_K):
        _, first = _first_index_of_max(cand, ei, N_EXPERTS)
        hit = ei == first
        idxs.append(first)
        ws.append(jnp.sum(jnp.where(hit, scores, 0.0), axis=0, keepdims=True))
        hits.append(hit)
        picked = jnp.where(hit, 1.0, picked)
        cand = jnp.where(hit, -jnp.inf, cand)
    idx_ref[...] = jnp.concatenate(idxs, axis=0)
    w_t = jnp.concatenate(ws, axis=0)
    w_t = w_t / jnp.sum(w_t, axis=0, keepdims=True) * ROUTED_SCALE
    pad = jnp.zeros((128 - TOP_K, tt), F32)
    wn_ref[...] = jnp.concatenate([w_t, pad], axis=0).T[:, :TOP_K]

    tr = lax.broadcasted_iota(I32, (tt, tt), 0)
    tc = lax.broadcasted_iota(I32, (tt, tt), 1)
    before = jnp.where(tr < tc, 1.0, 0.0).astype(BF16)
    seen = _dot(picked.astype(BF16), before) + base_ref[...]
    rank_ref[...] = jnp.concatenate(
        [jnp.sum(jnp.where(h, seen, 0.0), axis=0, keepdims=True) for h in hits], axis=0).astype(I32)
    base_ref[...] += jnp.sum(picked, axis=1, keepdims=True)
    cnt_ref[...] = base_ref[...].astype(I32)


def _outproj_route(mixed_d, mixed_s, x2, mod, w_out, g_ffn, wr_hi, wr_lo, e_bias, s):
    t, d = x2.shape
    half = mixed_d.shape[1]
    per_b = s // TT
    const = lambda shape: pl.BlockSpec(shape, lambda i: (0,) * len(shape))
    tok = lambda wd: pl.BlockSpec((TT, wd), lambda i: (i, 0))
    lane = pl.BlockSpec((TOP_K, TT), lambda i: (0, i))
    return pl.pallas_call(
        _route_kernel,
        out_shape=(jax.ShapeDtypeStruct((t, d), F32),
                   jax.ShapeDtypeStruct((t, d // 4), U32),
                   jax.ShapeDtypeStruct((t, d // 4), U32),
                   jax.ShapeDtypeStruct((TOP_K, t), I32),
                   jax.ShapeDtypeStruct((t, TOP_K), F32),
                   jax.ShapeDtypeStruct((TOP_K, t), I32),
                   jax.ShapeDtypeStruct((N_EXPERTS, 1), I32)),
        grid=(t // TT,),
        in_specs=[tok(half), tok(half), tok(d),
                  pl.BlockSpec((1, N_MOD, d), lambda i: (i // per_b, 0, 0)),
                  const(w_out.shape), const((1, d)), const(wr_hi.shape), const(wr_lo.shape),
                  const((N_EXPERTS, 1))],
        out_specs=(tok(d), tok(d // 4), tok(d // 4), lane, tok(TOP_K), lane, const((N_EXPERTS, 1))),
        scratch_shapes=[pltpu.VMEM((N_EXPERTS, 1), F32)],
        compiler_params=_cparams(("arbitrary",)),
        name="outproj_route",
    )(mixed_d, mixed_s, x2, mod, w_out, g_ffn, wr_hi, wr_lo, e_bias)


def _pos_kernel(idx_ref, rank_ref, pstart_ref, pos_ref):
    tt = idx_ref.shape[1]
    ei = lax.broadcasted_iota(I32, (N_EXPERTS, tt), 0)
    ps = pstart_ref[...]
    rows = []
    for k in range(TOP_K):
        hit = ei == idx_ref[k:k + 1, :]
        rows.append(jnp.sum(jnp.where(hit, ps, 0), axis=0, keepdims=True))
    pos_ref[...] = jnp.concatenate(rows, axis=0) + rank_ref[...]


def _slot_positions(idx_t, rank_t, pstart):
    k, t = idx_t.shape
    tp = 1024
    lane = pl.BlockSpec((k, tp), lambda i: (0, i))
    return pl.pallas_call(
        _pos_kernel,
        out_shape=jax.ShapeDtypeStruct((k, t), I32),
        grid=(t // tp,),
        in_specs=[lane, lane, pl.BlockSpec((N_EXPERTS, 1), lambda i: (0, 0))],
        out_specs=lane,
        compiler_params=_cparams(("parallel",)),
        name="slot_positions",
    )(idx_t, rank_t, pstart)


def _expert_kernel(be_ref, nv_ref, nu_ref, xa_ref, xb_ref, wgu_ref, wd_ref, ya_ref, yb_ref, wgu_b, wd_b):
    b = pl.program_id(0)
    nv = nv_ref[b]

    @pl.when(nv > 0)
    def _():
        prev = be_ref[jnp.maximum(b - 1, 0)]

        @pl.when((b == 0) | (be_ref[b] != prev))
        def _():
            wgu_b[...] = wgu_ref[0].astype(BF16)
            wd_b[...] = wd_ref[0].astype(BF16)

    for v in range(EXPERT_SUBS + 1):
        @pl.when(nv == v)
        def _(v=v):
            rows = v * EXPERT_ROWS
            if rows:
                x = _unpack_rows(xa_ref[:rows, :], xb_ref[:rows, :]).astype(BF16)
                gu = _dot(x, wgu_b[...])
                ff = gu.shape[1] // 2
                act = _silu(gu[:, :ff]) * gu[:, ff:]
                y = _dot(act.astype(BF16), wd_b[...])
                ya_ref[:rows, :], yb_ref[:rows, :] = _pack_rows(y)
            if rows < STEP_ROWS:
                ya_ref[rows:, :] = jnp.zeros((STEP_ROWS - rows, ya_ref.shape[1]), U32)
                yb_ref[rows:, :] = jnp.zeros((STEP_ROWS - rows, yb_ref.shape[1]), U32)


def _experts(block_e, n_valid, n_used, xs_a, xs_b, w_gu, w_down):
    p, dp = xs_a.shape
    n_steps = p // STEP_ROWS
    _, d, f2 = w_gu.shape
    _, f, _ = w_down.shape
    x_spec = pl.BlockSpec((STEP_ROWS, dp), lambda b, be, nv, nu: (jnp.minimum(b, nu[0] - 1), 0))
    y_spec = pl.BlockSpec((STEP_ROWS, dp), lambda b, be, nv, nu: (jnp.where(b < nu[0], b, n_steps), 0))
    grid_spec = pltpu.PrefetchScalarGridSpec(
        num_scalar_prefetch=3,
        grid=(n_steps,),
        in_specs=[x_spec, x_spec,
                  pl.BlockSpec((1, d, f2), lambda b, be, nv, nu: (be[b], 0, 0)),
                  pl.BlockSpec((1, f, d), lambda b, be, nv, nu: (be[b], 0, 0))],
        out_specs=(y_spec, y_spec),
        scratch_shapes=[pltpu.VMEM((d, f2), BF16), pltpu.VMEM((f, d), BF16)],
    )
    y_shape = jax.ShapeDtypeStruct((p + STEP_ROWS, dp), U32)
    return pl.pallas_call(
        _expert_kernel,
        out_shape=(y_shape, y_shape),
        grid_spec=grid_spec,
        compiler_params=_cparams(("arbitrary",)),
        name="routed_experts",
    )(block_e, n_valid, n_used, xs_a, xs_b, w_gu, w_down)


def _final_kernel(r1_ref, h2a_ref, h2b_ref, yga_ref, ygb_ref, wn_ref, mod_ref, wsg_ref, wsd_ref, o_ref):
    h2 = _unpack_rows(h2a_ref[...], h2b_ref[...]).astype(BF16)
    gu = _dot(h2, wsg_ref[...])
    ff = gu.shape[1] // 2
    y = _dot((_silu(gu[:, :ff]) * gu[:, ff:]).astype(BF16), wsd_ref[...])
    wn = wn_ref[...]
    for k in range(TOP_K):
        y = y + wn[:, k:k + 1] * _unpack_rows(yga_ref[k], ygb_ref[k])
    o_ref[...] = r1_ref[...] + mod_ref[0, 5:6, :] * y


def _final(r1, h2a, h2b, yg_a, yg_b, w_nat, mod, w_sh_gu, w_sh_down, s):
    t, d = r1.shape
    per_b = s // TT
    const = lambda shape: pl.BlockSpec(shape, lambda i: (0,) * len(shape))
    tok = lambda wd: pl.BlockSpec((TT, wd), lambda i: (i, 0))
    return pl.pallas_call(
        _final_kernel,
        out_shape=jax.ShapeDtypeStruct((t, d), F32),
        grid=(t // TT,),
        in_specs=[tok(d), tok(d // 4), tok(d // 4),
                  pl.BlockSpec((TOP_K, TT, d // 4), lambda i: (0, i, 0)),
                  pl.BlockSpec((TOP_K, TT, d // 4), lambda i: (0, i, 0)),
                  tok(TOP_K),
                  pl.BlockSpec((1, N_MOD, d), lambda i: (i // per_b, 0, 0)),
                  const(w_sh_gu.shape), const(w_sh_down.shape)],
        out_specs=tok(d),
        compiler_params=_cparams(("parallel",)),
        name="shared_combine",
    )(r1, h2a, h2b, yg_a, yg_b, w_nat, mod, w_sh_gu, w_sh_down)


def _sc_mesh():
    return plsc.VectorSubcoreMesh(core_axis_name="c", subcore_axis_name="s")


def _sc_dispatch(xa, xb, pos_t, p):
    t, dc = xa.shape
    k = pos_t.shape[0]
    out = jax.ShapeDtypeStruct((p, dc), xa.dtype)

    @functools.partial(pl.kernel, out_type=(out, out), mesh=_sc_mesh())
    def kern(xa_hbm, xb_hbm, pos_hbm, oa_hbm, ob_hbm):
        for src, dst in ((xa_hbm, oa_hbm), (xb_hbm, ob_hbm)):
            def body(x_vmem, i_vmem, dst=dst):
                pltpu.sync_copy(x_vmem, dst.at[i_vmem.at[0]])

            pltpu.emit_pipeline(
                body,
                grid=(t // SC_WINDOW, k),
                in_specs=[pl.BlockSpec((SC_WINDOW, dc), lambda i, kk: (i, 0)),
                          pl.BlockSpec((1, SC_WINDOW), lambda i, kk: (kk, i))],
                out_specs=[],
                core_axis_name=("c", "s"),
                dimension_semantics=(pltpu.PARALLEL, pltpu.ARBITRARY),
            )(src, pos_hbm)

    return kern(xa, xb, pos_t)


def _sc_gather(ya, yb, pos_row):
    n = pos_row.shape[1]
    dc = ya.shape[1]
    out = jax.ShapeDtypeStruct((n, dc), ya.dtype)

    @functools.partial(pl.kernel, out_type=(out, out), mesh=_sc_mesh())
    def kern(ya_hbm, yb_hbm, pos_hbm, oa_hbm, ob_hbm):
        for src, dst in ((ya_hbm, oa_hbm), (yb_hbm, ob_hbm)):
            def body(i_vmem, o_vmem, src=src):
                pltpu.sync_copy(src.at[i_vmem.at[0]], o_vmem)

            pltpu.emit_pipeline(
                body,
                grid=(n // SC_WINDOW,),
                in_specs=[pl.BlockSpec((1, SC_WINDOW), lambda i: (0, i))],
                out_specs=[pl.BlockSpec((SC_WINDOW, dc), lambda i: (i, 0))],
                core_axis_name=("c", "s"),
                dimension_semantics=(pltpu.PARALLEL,),
            )(pos_hbm, dst)

    return kern(ya, yb, pos_row)


def kernel(x, c, positions, w_ada, b_ada, g_mix, w_in, g_q, g_k, lam, g_sub, w_out, g_ffn, w_router,
           e_bias, w_sh_gu, w_sh_down, w_gu, w_down):
    b, s, d = x.shape
    assert w_ada.shape[0] == 1 and s % TS == 0 and d % 256 == 0
    t = b * s
    wdt = w_in.shape[2] // 6

    c_pad = jnp.zeros((8, d), F32).at[:b].set(c.astype(F32))
    mod = _ada(c_pad, w_ada[0], b_ada)[:b].reshape(b, N_MOD, d)

    sec = [w_in[0][:, i * wdt:(i + 1) * wdt] for i in range(6)]
    w_t = jnp.concatenate([sec[0], sec[1], sec[2], sec[3], sec[5]], axis=1).T.astype(BF16)
    w_sk = sec[4].astype(BF16)
    inv = (ROPE_THETA ** (-jnp.arange(0, HEAD_DIM, 2, dtype=F32) / HEAD_DIM)).reshape(HALF, 1)
    pos3 = positions.astype(I32).reshape(b, 1, s)

    qd, kd, vd, qs, ks, vs = _inproj(x, mod, g_mix, pos3, inv, g_q.reshape(HEAD_DIM, 1),
                                     g_k.reshape(HEAD_DIM, 1), w_t, w_sk)
    mixed_d = _diff_attention(qd, kd, vd, lam[0], g_sub.reshape(2 * HEAD_DIM, 1))
    mixed_s = _sb_attention(qs, ks, vs)

    wr_t = w_router[0].T
    wr_hi = wr_t.astype(BF16)
    wr_lo = (wr_t - wr_hi.astype(F32)).astype(BF16)
    r1, h2a, h2b, idx_t, w_nat, rank_t, counts = _outproj_route(
        mixed_d.reshape(t, wdt), mixed_s.reshape(t, wdt), x.reshape(t, d), mod,
        w_out[0].astype(BF16), g_ffn, wr_hi, wr_lo, e_bias.reshape(N_EXPERTS, 1), s)

    counts = counts.reshape(N_EXPERTS)
    padded = (counts + STEP_ROWS - 1) // STEP_ROWS * STEP_ROWS
    pend = jnp.cumsum(padded)
    pstart = (pend - padded).astype(I32)
    n_steps = t * TOP_K // STEP_ROWS + N_EXPERTS
    step_row0 = jnp.arange(n_steps, dtype=pend.dtype) * STEP_ROWS
    block_e = jnp.minimum(jnp.sum(pend[None, :] <= step_row0[:, None], axis=1), N_EXPERTS - 1).astype(I32)
    rows_left = counts[block_e] - (step_row0 - pstart[block_e])
    n_valid = jnp.where(step_row0 < pend[-1],
                        jnp.clip((rows_left + EXPERT_ROWS - 1) // EXPERT_ROWS, 0, EXPERT_SUBS), 0).astype(I32)
    n_used = (pend[-1:] // STEP_ROWS).astype(I32)
    pos_t = _slot_positions(idx_t, rank_t, pstart.reshape(N_EXPERTS, 1))

    xs_a, xs_b = _sc_dispatch(h2a, h2b, pos_t, n_steps * STEP_ROWS)
    y_a, y_b = _experts(block_e, n_valid, n_used, xs_a, xs_b, w_gu[0], w_down[0])
    yg_a, yg_b = _sc_gather(y_a, y_b, pos_t.reshape(1, TOP_K * t))
    dq4 = d // 4
    out = _final(r1, h2a, h2b, yg_a.reshape(TOP_K, t, dq4), yg_b.reshape(TOP_K, t, dq4), w_nat, mod,
                 w_sh_gu[0].astype(BF16), w_sh_down[0].astype(BF16), s)
    return out.reshape(b, s, d).astype(x.dtype)
```

```python
import functools
import math

import jax
import jax.numpy as jnp
from jax import lax
from jax.experimental import pallas as pl
from jax.experimental.pallas import tpu as pltpu
from jax.experimental.pallas import tpu_sc as plsc

F32 = jnp.float32
BF16 = jnp.bfloat16
U32 = jnp.uint32
I32 = jnp.int32

HEAD_DIM = 64
HALF = HEAD_DIM // 2
ROPE_THETA = 10000.0
N_EXPERTS = 256
TOP_K = 8
N_GROUPS = 8
GROUP_SIZE = N_EXPERTS // N_GROUPS
TOPK_GROUPS = 4
EXPERT_FF = 256
ROUTED_SCALE = 2.5
EXPERT_ROWS = 128
EXPERT_SUBS = 4
STEP_ROWS = EXPERT_SUBS * EXPERT_ROWS
NORM_EPS = 1e-6
SB_LOG_FLOOR = -110.0
NSTREAM = 4
N_MOD = 6
LAMBDA_INIT = 0.8 - 0.6 * math.exp(-0.3 * 0)
Q_SCALE = HEAD_DIM ** -0.5

TS = 512
TB = 256
TT = 256
HI_MASK = 0xFFFF0000
SC_WINDOW = 128

VMEM_LIMIT = 48 * 1024 * 1024


def _cparams(sem):
    return pltpu.CompilerParams(dimension_semantics=sem, vmem_limit_bytes=VMEM_LIMIT)


def _split_bf16(a):
    hi = a.astype(BF16)
    lo = (a - hi.astype(F32)).astype(BF16)
    return hi, lo


def _dot(a, b):
    return jnp.dot(a, b, preferred_element_type=F32)


def _dot_nt(a, b):
    return lax.dot_general(a, b, (((1,), (1,)), ((), ())), preferred_element_type=F32)


def _silu(x):
    return x / (1.0 + jnp.exp(-x))


def _pack_bf16_pair(a, b):
    ab = pltpu.bitcast(a.astype(BF16).astype(F32), U32)
    bb = pltpu.bitcast(b.astype(BF16).astype(F32), U32)
    return (ab & jnp.uint32(HI_MASK)) | (bb >> 16)


def _unpack_bf16_pair(p):
    a = pltpu.bitcast(p & jnp.uint32(HI_MASK), F32)
    b = pltpu.bitcast(p << 16, F32)
    return a, b


def _pack_rows(v):
    q = v.shape[1] // 4
    return (_pack_bf16_pair(v[:, :q], v[:, q:2 * q]),
            _pack_bf16_pair(v[:, 2 * q:3 * q], v[:, 3 * q:]))


def _unpack_rows(pa, pb):
    return jnp.concatenate(_unpack_bf16_pair(pa) + _unpack_bf16_pair(pb), axis=1)


def _ada_kernel(c_ref, w_ref, b_ref, o_ref):
    cond = _silu(c_ref[...])
    ch, cl = _split_bf16(cond)
    wh, wl = _split_bf16(w_ref[...])
    o_ref[...] = _dot(ch, wh) + _dot(ch, wl) + _dot(cl, wh) + b_ref[...]


def _ada(c_pad, w_ada, b_ada):
    d, n = w_ada.shape
    tn = n // 4
    return pl.pallas_call(
        _ada_kernel,
        out_shape=jax.ShapeDtypeStruct((c_pad.shape[0], n), F32),
        grid=(n // tn,),
        in_specs=[pl.BlockSpec((c_pad.shape[0], d), lambda j: (0, 0)),
                  pl.BlockSpec((d, tn), lambda j: (0, j)),
                  pl.BlockSpec((1, tn), lambda j: (0, j))],
        out_specs=pl.BlockSpec((c_pad.shape[0], tn), lambda j: (0, j)),
        compiler_params=_cparams(("arbitrary",)),
        name="ada_mod",
    )(c_pad, w_ada, b_ada)


def _norm_rope_t(t, g_col, cos_t, sin_t):
    outs = []
    for i in range(t.shape[0] // HEAD_DIM):
        blk = t[i * HEAD_DIM:(i + 1) * HEAD_DIM, :]
        ms = jnp.mean(blk * blk, axis=0, keepdims=True)
        y = blk * lax.rsqrt(ms + NORM_EPS) * g_col
        y1 = y[:HALF]
        y2 = y[HALF:]
        outs.append(y1 * cos_t - y2 * sin_t)
        outs.append(y2 * cos_t + y1 * sin_t)
    return jnp.concatenate(outs, axis=0)


def _inproj_kernel(x_ref, mod_ref, g_ref, pos_ref, inv_ref, gq_ref, gk_ref, wt_ref, wsk_ref,
                   qd_ref, kd_ref, vd_ref, qs_ref, ks_ref, vs_ref):
    x = x_ref[0]
    ms = jnp.mean(x * x, axis=-1, keepdims=True)
    sh = mod_ref[0, 0:1, :]
    sc = mod_ref[0, 1:2, :]
    h = x * lax.rsqrt(ms + NORM_EPS) * g_ref[...] * (1.0 + sc) + sh
    hb = h.astype(BF16)
    w = qd_ref.shape[1]

    def proj_t(sec):
        return _dot_nt(wt_ref[sec * w:(sec + 1) * w, :], hb)

    ang = inv_ref[...] * pos_ref[0].astype(F32)
    cos_t = jnp.cos(ang)
    sin_t = jnp.sin(ang)
    q_t = _norm_rope_t(proj_t(0), gq_ref[...], cos_t, sin_t) * Q_SCALE
    qd_ref[0] = q_t.astype(BF16)
    k_t = _norm_rope_t(proj_t(1), gk_ref[...], cos_t, sin_t)
    kd_ref[0] = k_t.T.astype(BF16)
    v_t = proj_t(2).astype(BF16)
    s_t = proj_t(4).astype(BF16)
    nh = vd_ref.shape[1]
    hw = vd_ref.shape[3]
    for hh in range(nh):
        for jj in range(vd_ref.shape[2]):
            vd_ref[0, hh, jj] = v_t[hh * hw:(hh + 1) * hw, jj * TB:(jj + 1) * TB]
            vs_ref[0, hh, jj] = s_t[hh * hw:(hh + 1) * hw, jj * TB:(jj + 1) * TB]
    qs_ref[0] = (proj_t(3) * Q_SCALE).astype(BF16)
    ks_ref[0] = _dot(hb, wsk_ref[...]).astype(BF16)


def _inproj(x, mod, g_mix, pos3, inv, g_q, g_k, w_t, w_sk):
    b, s, d = x.shape
    w = w_sk.shape[1]
    nh = w // 128
    nsb = TS // TB
    const = lambda shape: pl.BlockSpec(shape, lambda bi, si: (0,) * len(shape))
    t_spec = pl.BlockSpec((1, w, TS), lambda bi, si: (bi, 0, si))
    n_spec = pl.BlockSpec((1, TS, w), lambda bi, si: (bi, si, 0))
    v_spec = pl.BlockSpec((1, nh, nsb, 128, TB), lambda bi, si: (bi, 0, si, 0, 0))
    t_shape = jax.ShapeDtypeStruct((b, w, s), BF16)
    n_shape = jax.ShapeDtypeStruct((b, s, w), BF16)
    v_shape = jax.ShapeDtypeStruct((b, nh, s // TB, 128, TB), BF16)
    return pl.pallas_call(
        _inproj_kernel,
        out_shape=(t_shape, n_shape, v_shape, t_shape, n_shape, v_shape),
        grid=(b, s // TS),
        in_specs=[pl.BlockSpec((1, TS, d), lambda bi, si: (bi, si, 0)),
                  pl.BlockSpec((1, N_MOD, d), lambda bi, si: (bi, 0, 0)),
                  const((1, d)),
                  pl.BlockSpec((1, 1, TS), lambda bi, si: (bi, 0, si)),
                  const((HALF, 1)), const((HEAD_DIM, 1)), const((HEAD_DIM, 1)),
                  const(w_t.shape), const(w_sk.shape)],
        out_specs=(t_spec, n_spec, v_spec, t_spec, n_spec, v_spec),
        compiler_params=_cparams(("parallel", "parallel")),
        name="norm_inproj",
    )(x, mod, g_mix, pos3, inv, g_q, g_k, w_t, w_sk)


def _q_pair(q_t):
    row = lax.broadcasted_iota(I32, q_t.shape, 0)
    zero = jnp.zeros_like(q_t)
    return jnp.concatenate([jnp.where(row < HEAD_DIM, q_t, zero),
                            jnp.where(row >= HEAD_DIM, q_t, zero)], axis=1)


def _key_query_iotas(tk, tq):
    r = lax.broadcasted_iota(I32, (tk, 2 * tq), 0)
    c = lax.broadcasted_iota(I32, (tk, 2 * tq), 1)
    return r, jnp.where(c >= tq, c - tq, c)


def _diff_kernel(qt_ref, k_ref, vt_ref, lam_ref, gsub_ref, o_ref):
    qi = pl.program_id(2)
    tq = qt_ref.shape[2]
    qcats = [_q_pair(qt_ref[0, st * 128:(st + 1) * 128, :]) for st in range(NSTREAM)]

    def step(j, carry, diagonal):
        row0 = pl.multiple_of(j * TB, TB)
        scores = [_dot(k_ref[0, pl.ds(row0, TB), st * 128:(st + 1) * 128], qcats[st])
                  for st in range(NSTREAM)]
        soft = []
        for st in range(NSTREAM):
            m, l, _ = carry[st]
            s = scores[st]
            if diagonal:
                r, c = _key_query_iotas(TB, tq)
                s = jnp.where(r <= c, s, -jnp.inf)
            m_new = jnp.maximum(m, jnp.max(s, axis=0, keepdims=True))
            alpha = jnp.exp(m - m_new)
            p = jnp.exp(s - m_new)
            l = l * alpha + jnp.sum(p, axis=0, keepdims=True)
            soft.append((m_new, l, alpha, p.astype(BF16)))
        out = []
        for st in range(NSTREAM):
            m_new, l, alpha, p = soft[st]
            acc = carry[st][2] * alpha + _dot(vt_ref[0, st, j], p)
            out.append((m_new, l, acc))
        return tuple(out)

    init = tuple((jnp.full((1, 2 * tq), -jnp.inf, F32), jnp.zeros((1, 2 * tq), F32),
                  jnp.zeros((2 * HEAD_DIM, 2 * tq), F32)) for _ in range(NSTREAM))
    carry = lax.fori_loop(0, qi, lambda j, c: step(j, c, False), init)
    carry = step(qi, carry, True)

    lam = lam_ref[...]
    l1 = jnp.sum(lam[0:1] * lam[1:2], axis=-1, keepdims=True)
    l2 = jnp.sum(lam[2:3] * lam[3:4], axis=-1, keepdims=True)
    lmbda = jnp.exp(l1) - jnp.exp(l2) + LAMBDA_INIT
    for st in range(NSTREAM):
        _, l, acc = carry[st]
        o = acc[:, :tq] / l[:, :tq] - lmbda * (acc[:, tq:] / l[:, tq:])
        ms = jnp.mean(o * o, axis=0, keepdims=True)
        y = o * lax.rsqrt(ms + NORM_EPS) * gsub_ref[...] * (1.0 - LAMBDA_INIT)
        o_ref[0, :, st * 128:(st + 1) * 128] = y.T.astype(BF16)


def _attention_specs(b, w, s):
    gw = NSTREAM * 128
    in_specs = [pl.BlockSpec((1, gw, TB), lambda bi, gi, qi: (bi, gi, qi)),
                pl.BlockSpec((1, s, gw), lambda bi, gi, qi: (bi, 0, gi)),
                pl.BlockSpec((1, NSTREAM, s // TB, 128, TB), lambda bi, gi, qi: (bi, gi, 0, 0, 0))]
    out_spec = pl.BlockSpec((1, TB, gw), lambda bi, gi, qi: (bi, qi, gi))
    return (b, w // gw, s // TB), in_specs, out_spec


def _diff_attention(q_t, k, v_t, lam, g_sub):
    b, w, s = q_t.shape
    grid, in_specs, out_spec = _attention_specs(b, w, s)
    return pl.pallas_call(
        _diff_kernel,
        out_shape=jax.ShapeDtypeStruct((b, s, w), BF16),
        grid=grid,
        in_specs=in_specs + [pl.BlockSpec(lam.shape, lambda bi, gi, qi: (0, 0)),
                             pl.BlockSpec(g_sub.shape, lambda bi, gi, qi: (0, 0))],
        out_specs=out_spec,
        compiler_params=_cparams(("parallel", "parallel", "parallel")),
        name="diff_attention",
    )(q_t, k, v_t, lam, g_sub)


def _sb_kernel(qt_ref, k_ref, vt_ref, o_ref):
    qi = pl.program_id(2)
    tq = qt_ref.shape[2]
    qcats = [_q_pair(qt_ref[0, st * 128:(st + 1) * 128, :]) for st in range(NSTREAM)]
    ur = lax.broadcasted_iota(I32, (TB, 2 * TB), 0)
    uc = lax.broadcasted_iota(I32, (TB, 2 * TB), 1)
    uc = jnp.where(uc >= TB, uc - TB, uc)
    u2 = jnp.where(uc > ur, 1.0, 0.0).astype(BF16)

    def step(j, carry, diagonal):
        row0 = pl.multiple_of(j * TB, TB)
        if diagonal:
            r, c = _key_query_iotas(TB, tq)
            valid = r < c
        zs = [_dot(k_ref[0, pl.ds(row0, TB), st * 128:(st + 1) * 128], qcats[st])
              for st in range(NSTREAM)]
        pre = []
        for st in range(NSTREAM):
            z = zs[st]
            sp = jnp.maximum(z, 0.0) + jnp.log(1.0 + jnp.exp(-jnp.abs(z)))
            lk = -sp
            if diagonal:
                lk = jnp.where(valid, lk, 0.0)
            hi, lo = _split_bf16(lk)
            pre.append((z - sp, jnp.concatenate([hi, lo], axis=0), jnp.sum(lk, axis=0, keepdims=True)))
        laters = [_dot(u2, pre[st][1]) for st in range(NSTREAM)]
        out = []
        for st in range(NSTREAM):
            c_run, acc = carry[st]
            w = jnp.exp(pre[st][0] + laters[st] + c_run)
            if diagonal:
                w = jnp.where(valid, w, 0.0)
            wb = w.astype(BF16)
            vj = vt_ref[0, st, j]
            acc = acc + jnp.concatenate([_dot(vj[:HEAD_DIM], wb[:, :tq]),
                                         _dot(vj[HEAD_DIM:], wb[:, tq:])], axis=0)
            out.append((c_run + pre[st][2], acc))
        return tuple(out)

    def live(carry):
        c_max = functools.reduce(jnp.maximum, [jnp.max(cr) for cr, _ in carry])
        return c_max > SB_LOG_FLOOR

    init = tuple((jnp.zeros((1, 2 * tq), F32), jnp.zeros((2 * HEAD_DIM, tq), F32)) for _ in range(NSTREAM))
    carry = step(qi, init, True)
    _, carry = lax.while_loop(lambda sc: (sc[0] < qi) & live(sc[1]),
                              lambda sc: (sc[0] + 1, step(qi - 1 - sc[0], sc[1], False)),
                              (jnp.int32(0), carry))
    for st in range(NSTREAM):
        o_ref[0, :, st * 128:(st + 1) * 128] = carry[st][1].T.astype(BF16)


def _sb_attention(q_t, k, v_t):
    b, w, s = q_t.shape
    grid, in_specs, out_spec = _attention_specs(b, w, s)
    return pl.pallas_call(
        _sb_kernel,
        out_shape=jax.ShapeDtypeStruct((b, s, w), BF16),
        grid=grid,
        in_specs=in_specs,
        out_specs=out_spec,
        compiler_params=_cparams(("parallel", "parallel", "parallel")),
        name="sb_attention",
    )(q_t, k, v_t)


def _first_index_of_max(v, idx, size):
    m = jnp.max(v, axis=0, keepdims=True)
    first = jnp.min(jnp.where(v == m, idx, size), axis=0, keepdims=True)
    return m, first


def _route_kernel(md_ref, ms_ref, x_ref, mod_ref, wo_ref, g_ref, wrh_ref, wrl_ref, eb_ref,
                  r1_ref, h2a_ref, h2b_ref, idx_ref, wn_ref, rank_ref, cnt_ref, base_ref):
    i = pl.program_id(0)

    @pl.when(i == 0)
    def _():
        base_ref[...] = jnp.zeros_like(base_ref)

    half = md_ref.shape[1]
    attn = _dot(md_ref[...], wo_ref[:half, :]) + _dot(ms_ref[...], wo_ref[half:, :])
    r1 = x_ref[...] + mod_ref[0, 2:3, :] * attn
    r1_ref[...] = r1
    ms = jnp.mean(r1 * r1, axis=-1, keepdims=True)
    h2 = r1 * lax.rsqrt(ms + NORM_EPS) * g_ref[...] * (1.0 + mod_ref[0, 4:5, :]) + mod_ref[0, 3:4, :]
    h2a_ref[...], h2b_ref[...] = _pack_rows(h2)

    hh, hl = _split_bf16(h2)
    wh = wrh_ref[...]
    logits = _dot_nt(wh, hh) + _dot_nt(wh, hl) + _dot_nt(wrl_ref[...], hh)
    scores = 1.0 / (1.0 + jnp.exp(-logits))
    biased = scores + eb_ref[...]
    tt = scores.shape[1]

    gi = lax.broadcasted_iota(I32, (GROUP_SIZE, tt), 0)
    gs = []
    for g in range(N_GROUPS):
        blk = biased[g * GROUP_SIZE:(g + 1) * GROUP_SIZE, :]
        m1, first = _first_index_of_max(blk, gi, GROUP_SIZE)
        m2 = jnp.max(jnp.where(gi == first, -jnp.inf, blk), axis=0, keepdims=True)
        gs.append(m1 + m2)
    gscore = jnp.concatenate(gs, axis=0)
    gidx = lax.broadcasted_iota(I32, (N_GROUPS, tt), 0)
    gsel = jnp.zeros((N_GROUPS, tt), F32)
    for _ in range(TOPK_GROUPS):
        _, first = _first_index_of_max(gscore, gidx, N_GROUPS)
        hit = gidx == first
        gsel = jnp.where(hit, 1.0, gsel)
        gscore = jnp.where(hit, -jnp.inf, gscore)
    cand = jnp.concatenate(
        [jnp.where(gsel[g:g + 1, :] > 0.0, biased[g * GROUP_SIZE:(g + 1) * GROUP_SIZE, :], -jnp.inf)
         for g in range(N_GROUPS)], axis=0)

    ei = lax.broadcasted_iota(I32, (N_EXPERTS, tt), 0)
    picked = jnp.zeros((N_EXPERTS, tt), F32)
    idxs, ws, hits = [], [], []
    for _ in range(TOP_K):
        _, first = _first_index_of_max(cand, ei, N_EXPERTS)
        hit = ei == first
        idxs.append(first)
        ws.append(jnp.sum(jnp.where(hit, scores, 0.0), axis=0, keepdims=True))
        hits.append(hit)
        picked = jnp.where(hit, 1.0, picked)
        cand = jnp.where(hit, -jnp.inf, cand)
    idx_ref[...] = jnp.concatenate(idxs, axis=0)
    w_t = jnp.concatenate(ws, axis=0)
    w_t = w_t / jnp.sum(w_t, axis=0, keepdims=True) * ROUTED_SCALE
    pad = jnp.zeros((128 - TOP_K, tt), F32)
    wn_ref[...] = jnp.concatenate([w_t, pad], axis=0).T[:, :TOP_K]

    tr = lax.broadcasted_iota(I32, (tt, tt), 0)
    tc = lax.broadcasted_iota(I32, (tt, tt), 1)
    before = jnp.where(tr < tc, 1.0, 0.0).astype(BF16)
    seen = _dot(picked.astype(BF16), before) + base_ref[...]
    rank_ref[...] = jnp.concatenate(
        [jnp.sum(jnp.where(h, seen, 0.0), axis=0, keepdims=True) for h in hits], axis=0).astype(I32)
    base_ref[...] += jnp.sum(picked, axis=1, keepdims=True)
    cnt_ref[...] = base_ref[...].astype(I32)


def _outproj_route(mixed_d, mixed_s, x2, mod, w_out, g_ffn, wr_hi, wr_lo, e_bias, s):
    t, d = x2.shape
    half = mixed_d.shape[1]
    per_b = s // TT
    const = lambda shape: pl.BlockSpec(shape, lambda i: (0,) * len(shape))
    tok = lambda wd: pl.BlockSpec((TT, wd), lambda i: (i, 0))
    lane = pl.BlockSpec((TOP_K, TT), lambda i: (0, i))
    return pl.pallas_call(
        _route_kernel,
        out_shape=(jax.ShapeDtypeStruct((t, d), F32),
                   jax.ShapeDtypeStruct((t, d // 4), U32),
                   jax.ShapeDtypeStruct((t, d // 4), U32),
                   jax.ShapeDtypeStruct((TOP_K, t), I32),
                   jax.ShapeDtypeStruct((t, TOP_K), F32),
                   jax.ShapeDtypeStruct((TOP_K, t), I32),
                   jax.ShapeDtypeStruct((N_EXPERTS, 1), I32)),
        grid=(t // TT,),
        in_specs=[tok(half), tok(half), tok(d),
                  pl.BlockSpec((1, N_MOD, d), lambda i: (i // per_b, 0, 0)),
                  const(w_out.shape), const((1, d)), const(wr_hi.shape), const(wr_lo.shape),
                  const((N_EXPERTS, 1))],
        out_specs=(tok(d), tok(d // 4), tok(d // 4), lane, tok(TOP_K), lane, const((N_EXPERTS, 1))),
        scratch_shapes=[pltpu.VMEM((N_EXPERTS, 1), F32)],
        compiler_params=_cparams(("arbitrary",)),
        name="outproj_route",
    )(mixed_d, mixed_s, x2, mod, w_out, g_ffn, wr_hi, wr_lo, e_bias)


def _pos_kernel(idx_ref, rank_ref, pstart_ref, pos_ref):
    tt = idx_ref.shape[1]
    ei = lax.broadcasted_iota(I32, (N_EXPERTS, tt), 0)
    ps = pstart_ref[...]
    rows = []
    for k in range(TOP_K):
        hit = ei == idx_ref[k:k + 1, :]
        rows.append(jnp.sum(jnp.where(hit, ps, 0), axis=0, keepdims=True))
    pos_ref[...] = jnp.concatenate(rows, axis=0) + rank_ref[...]


def _slot_positions(idx_t, rank_t, pstart):
    k, t = idx_t.shape
    tp = 1024
    lane = pl.BlockSpec((k, tp), lambda i: (0, i))
    return pl.pallas_call(
        _pos_kernel,
        out_shape=jax.ShapeDtypeStruct((k, t), I32),
        grid=(t // tp,),
        in_specs=[lane, lane, pl.BlockSpec((N_EXPERTS, 1), lambda i: (0, 0))],
        out_specs=lane,
        compiler_params=_cparams(("parallel",)),
        name="slot_positions",
    )(idx_t, rank_t, pstart)


def _expert_kernel(be_ref, nv_ref, nu_ref, xa_ref, xb_ref, wgu_ref, wd_ref, ya_ref, yb_ref, wgu_b, wd_b):
    b = pl.program_id(0)
    nv = nv_ref[b]

    @pl.when(nv > 0)
    def _():
        prev = be_ref[jnp.maximum(b - 1, 0)]

        @pl.when((b == 0) | (be_ref[b] != prev))
        def _():
            wgu_b[...] = wgu_ref[0].astype(BF16)
            wd_b[...] = wd_ref[0].astype(BF16)

    for v in range(EXPERT_SUBS + 1):
        @pl.when(nv == v)
        def _(v=v):
            rows = v * EXPERT_ROWS
            if rows:
                x = _unpack_rows(xa_ref[:rows, :], xb_ref[:rows, :]).astype(BF16)
                gu = _dot(x, wgu_b[...])
                ff = gu.shape[1] // 2
                act = _silu(gu[:, :ff]) * gu[:, ff:]
                y = _dot(act.astype(BF16), wd_b[...])
                ya_ref[:rows, :], yb_ref[:rows, :] = _pack_rows(y)
            if rows < STEP_ROWS:
                ya_ref[rows:, :] = jnp.zeros((STEP_ROWS - rows, ya_ref.shape[1]), U32)
                yb_ref[rows:, :] = jnp.zeros((STEP_ROWS - rows, yb_ref.shape[1]), U32)


def _experts(block_e, n_valid, n_used, xs_a, xs_b, w_gu, w_down):
    p, dp = xs_a.shape
    n_steps = p // STEP_ROWS
    _, d, f2 = w_gu.shape
    _, f, _ = w_down.shape
    x_spec = pl.BlockSpec((STEP_ROWS, dp), lambda b, be, nv, nu: (jnp.minimum(b, nu[0] - 1), 0))
    y_spec = pl.BlockSpec((STEP_ROWS, dp), lambda b, be, nv, nu: (jnp.where(b < nu[0], b, n_steps), 0))
    grid_spec = pltpu.PrefetchScalarGridSpec(
        num_scalar_prefetch=3,
        grid=(n_steps,),
        in_specs=[x_spec, x_spec,
                  pl.BlockSpec((1, d, f2), lambda b, be, nv, nu: (be[b], 0, 0)),
                  pl.BlockSpec((1, f, d), lambda b, be, nv, nu: (be[b], 0, 0))],
        out_specs=(y_spec, y_spec),
        scratch_shapes=[pltpu.VMEM((d, f2), BF16), pltpu.VMEM((f, d), BF16)],
    )
    y_shape = jax.ShapeDtypeStruct((p + STEP_ROWS, dp), U32)
    return pl.pallas_call(
        _expert_kernel,
        out_shape=(y_shape, y_shape),
        grid_spec=grid_spec,
        compiler_params=_cparams(("arbitrary",)),
        name="routed_experts",
    )(block_e, n_valid, n_used, xs_a, xs_b, w_gu, w_down)


def _final_kernel(r1_ref, h2a_ref, h2b_ref, yga_ref, ygb_ref, wn_ref, mod_ref, wsg_ref, wsd_ref, o_ref):
    h2 = _unpack_rows(h2a_ref[...], h2b_ref[...]).astype(BF16)
    gu = _dot(h2, wsg_ref[...])
    ff = gu.shape[1] // 2
    y = _dot((_silu(gu[:, :ff]) * gu[:, ff:]).astype(BF16), wsd_ref[...])
    wn = wn_ref[...]
    for k in range(TOP_K):
        y = y + wn[:, k:k + 1] * _unpack_rows(yga_ref[k], ygb_ref[k])
    o_ref[...] = r1_ref[...] + mod_ref[0, 5:6, :] * y


def _final(r1, h2a, h2b, yg_a, yg_b, w_nat, mod, w_sh_gu, w_sh_down, s):
    t, d = r1.shape
    per_b = s // TT
    const = lambda shape: pl.BlockSpec(shape, lambda i: (0,) * len(shape))
    tok = lambda wd: pl.BlockSpec((TT, wd), lambda i: (i, 0))
    return pl.pallas_call(
        _final_kernel,
        out_shape=jax.ShapeDtypeStruct((t, d), F32),
        grid=(t // TT,),
        in_specs=[tok(d), tok(d // 4), tok(d // 4),
                  pl.BlockSpec((TOP_K, TT, d // 4), lambda i: (0, i, 0)),
                  pl.BlockSpec((TOP_K, TT, d // 4), lambda i: (0, i, 0)),
                  tok(TOP_K),
                  pl.BlockSpec((1, N_MOD, d), lambda i: (i // per_b, 0, 0)),
                  const(w_sh_gu.shape), const(w_sh_down.shape)],
        out_specs=tok(d),
        compiler_params=_cparams(("parallel",)),
        name="shared_combine",
    )(r1, h2a, h2b, yg_a, yg_b, w_nat, mod, w_sh_gu, w_sh_down)


def _sc_mesh():
    return plsc.VectorSubcoreMesh(core_axis_name="c", subcore_axis_name="s")


def _sc_dispatch(xa, xb, pos_t, p):
    t, dc = xa.shape
    k = pos_t.shape[0]
    out = jax.ShapeDtypeStruct((p, dc), xa.dtype)

    @functools.partial(pl.kernel, out_type=(out, out), mesh=_sc_mesh())
    def kern(xa_hbm, xb_hbm, pos_hbm, oa_hbm, ob_hbm):
        for src, dst in ((xa_hbm, oa_hbm), (xb_hbm, ob_hbm)):
            def body(x_vmem, i_vmem, dst=dst):
                pltpu.sync_copy(x_vmem, dst.at[i_vmem.at[0]])

            pltpu.emit_pipeline(
                body,
                grid=(t // SC_WINDOW, k),
                in_specs=[pl.BlockSpec((SC_WINDOW, dc), lambda i, kk: (i, 0)),
                          pl.BlockSpec((1, SC_WINDOW), lambda i, kk: (kk, i))],
                out_specs=[],
                core_axis_name=("c", "s"),
                dimension_semantics=(pltpu.PARALLEL, pltpu.ARBITRARY),
            )(src, pos_hbm)

    return kern(xa, xb, pos_t)


def _sc_gather(ya, yb, pos_row):
    n = pos_row.shape[1]
    dc = ya.shape[1]
    out = jax.ShapeDtypeStruct((n, dc), ya.dtype)

    @functools.partial(pl.kernel, out_type=(out, out), mesh=_sc_mesh())
    def kern(ya_hbm, yb_hbm, pos_hbm, oa_hbm, ob_hbm):
        for src, dst in ((ya_hbm, oa_hbm), (yb_hbm, ob_hbm)):
            def body(i_vmem, o_vmem, src=src):
                pltpu.sync_copy(src.at[i_vmem.at[0]], o_vmem)

            pltpu.emit_pipeline(
                body,
                grid=(n // SC_WINDOW,),
                in_specs=[pl.BlockSpec((1, SC_WINDOW), lambda i: (0, i))],
                out_specs=[pl.BlockSpec((SC_WINDOW, dc), lambda i: (i, 0))],
                core_axis_name=("c", "s"),
                dimension_semantics=(pltpu.PARALLEL,),
            )(pos_hbm, dst)

    return kern(ya, yb, pos_row)


def kernel(x, c, positions, w_ada, b_ada, g_mix, w_in, g_q, g_k, lam, g_sub, w_out, g_ffn, w_router,
           e_bias, w_sh_gu, w_sh_down, w_gu, w_down):
    b, s, d = x.shape
    assert w_ada.shape[0] == 1 and s % TS == 0 and d % 256 == 0
    t = b * s
    wdt = w_in.shape[2] // 6

    c_pad = jnp.zeros((8, d), F32).at[:b].set(c.astype(F32))
    mod = _ada(c_pad, w_ada[0], b_ada)[:b].reshape(b, N_MOD, d)

    sec = [w_in[0][:, i * wdt:(i + 1) * wdt] for i in range(6)]
    w_t = jnp.concatenate([sec[0], sec[1], sec[2], sec[3], sec[5]], axis=1).T.astype(BF16)
    w_sk = sec[4].astype(BF16)
    inv = (ROPE_THETA ** (-jnp.arange(0, HEAD_DIM, 2, dtype=F32) / HEAD_DIM)).reshape(HALF, 1)
    pos3 = positions.astype(I32).reshape(b, 1, s)

    qd, kd, vd, qs, ks, vs = _inproj(x, mod, g_mix, pos3, inv, g_q.reshape(HEAD_DIM, 1),
                                     g_k.reshape(HEAD_DIM, 1), w_t, w_sk)
    mixed_d = _diff_attention(qd, kd, vd, lam[0], g_sub.reshape(2 * HEAD_DIM, 1))
    mixed_s = _sb_attention(qs, ks, vs)

    wr_t = w_router[0].T
    wr_hi = wr_t.astype(BF16)
    wr_lo = (wr_t - wr_hi.astype(F32)).astype(BF16)
    r1, h2a, h2b, idx_t, w_nat, rank_t, counts = _outproj_route(
        mixed_d.reshape(t, wdt), mixed_s.reshape(t, wdt), x.reshape(t, d), mod,
        w_out[0].astype(BF16), g_ffn, wr_hi, wr_lo, e_bias.reshape(N_EXPERTS, 1), s)

    counts = counts.reshape(N_EXPERTS)
    padded = (counts + STEP_ROWS - 1) // STEP_ROWS * STEP_ROWS
    pend = jnp.cumsum(padded)
    pstart = (pend - padded).astype(I32)
    n_steps = t * TOP_K // STEP_ROWS + N_EXPERTS
    step_row0 = jnp.arange(n_steps, dtype=pend.dtype) * STEP_ROWS
    block_e = jnp.minimum(jnp.sum(pend[None, :] <= step_row0[:, None], axis=1), N_EXPERTS - 1).astype(I32)
    rows_left = counts[block_e] - (step_row0 - pstart[block_e])
    n_valid = jnp.where(step_row0 < pend[-1],
                        jnp.clip((rows_left + EXPERT_ROWS - 1) // EXPERT_ROWS, 0, EXPERT_SUBS), 0).astype(I32)
    n_used = (pend[-1:] // STEP_ROWS).astype(I32)
    pos_t = _slot_positions(idx_t, rank_t, pstart.reshape(N_EXPERTS, 1))

    xs_a, xs_b = _sc_dispatch(h2a, h2b, pos_t, n_steps * STEP_ROWS)
    y_a, y_b = _experts(block_e, n_valid, n_used, xs_a, xs_b, w_gu[0], w_down[0])
    yg_a, yg_b = _sc_gather(y_a, y_b, pos_t.reshape(1, TOP_K * t))
    dq4 = d // 4
    out = _final(r1, h2a, h2b, yg_a.reshape(TOP_K, t, dq4), yg_b.reshape(TOP_K, t, dq4), w_nat, mod,
                 w_sh_gu[0].astype(BF16), w_sh_down[0].astype(BF16), s)
    return out.reshape(b, s, d).astype(x.dtype)
```

```python
import functools
import math

import jax
import jax.numpy as jnp
from jax import lax
from jax.experimental import pallas as pl
from jax.experimental.pallas import tpu as pltpu
from jax.experimental.pallas import tpu_sc as plsc

F32 = jnp.float32
BF16 = jnp.bfloat16
U32 = jnp.uint32
I32 = jnp.int32

HEAD_DIM = 64
HALF = HEAD_DIM // 2
ROPE_THETA = 10000.0
N_EXPERTS = 256
TOP_K = 8
N_GROUPS = 8
GROUP_SIZE = N_EXPERTS // N_GROUPS
TOPK_GROUPS = 4
EXPERT_FF = 256
ROUTED_SCALE = 2.5
EXPERT_ROWS = 128
EXPERT_SUBS = 4
STEP_ROWS = EXPERT_SUBS * EXPERT_ROWS
NORM_EPS = 1e-6
SB_LOG_FLOOR = -110.0
NSTREAM = 4
N_MOD = 6
LAMBDA_INIT = 0.8 - 0.6 * math.exp(-0.3 * 0)
Q_SCALE = HEAD_DIM ** -0.5
LOG2E = math.log2(math.e)
SUM_ROWS = 16
DV_ROWS = 2 * HEAD_DIM + SUM_ROWS

TS = 512
TB = 256
TT = 256
HI_MASK = 0xFFFF0000
SC_WINDOW = 128

VMEM_LIMIT = 48 * 1024 * 1024


def _cparams(sem):
    return pltpu.CompilerParams(dimension_semantics=sem, vmem_limit_bytes=VMEM_LIMIT)


def _split_bf16(a):
    hi = a.astype(BF16)
    lo = (a - hi.astype(F32)).astype(BF16)
    return hi, lo


def _dot(a, b):
    return jnp.dot(a, b, preferred_element_type=F32)


def _dot_nt(a, b):
    return lax.dot_general(a, b, (((1,), (1,)), ((), ())), preferred_element_type=F32)


def _silu(x):
    return x / (1.0 + jnp.exp(-x))


def _pack_bf16_pair(a, b):
    ab = pltpu.bitcast(a.astype(BF16).astype(F32), U32)
    bb = pltpu.bitcast(b.astype(BF16).astype(F32), U32)
    return (ab & jnp.uint32(HI_MASK)) | (bb >> 16)


def _unpack_bf16_pair(p):
    a = pltpu.bitcast(p & jnp.uint32(HI_MASK), F32)
    b = pltpu.bitcast(p << 16, F32)
    return a, b


def _pack_rows(v):
    q = v.shape[1] // 4
    return (_pack_bf16_pair(v[:, :q], v[:, q:2 * q]),
            _pack_bf16_pair(v[:, 2 * q:3 * q], v[:, 3 * q:]))


def _unpack_rows(pa, pb):
    return jnp.concatenate(_unpack_bf16_pair(pa) + _unpack_bf16_pair(pb), axis=1)


def _ada_kernel(c_ref, w_ref, b_ref, o_ref):
    cond = _silu(c_ref[...])
    ch, cl = _split_bf16(cond)
    wh, wl = _split_bf16(w_ref[...])
    o_ref[...] = _dot(ch, wh) + _dot(ch, wl) + _dot(cl, wh) + b_ref[...]


def _ada(c_pad, w_ada, b_ada):
    d, n = w_ada.shape
    tn = n // 4
    return pl.pallas_call(
        _ada_kernel,
        out_shape=jax.ShapeDtypeStruct((c_pad.shape[0], n), F32),
        grid=(n // tn,),
        in_specs=[pl.BlockSpec((c_pad.shape[0], d), lambda j: (0, 0)),
                  pl.BlockSpec((d, tn), lambda j: (0, j)),
                  pl.BlockSpec((1, tn), lambda j: (0, j))],
        out_specs=pl.BlockSpec((c_pad.shape[0], tn), lambda j: (0, j)),
        compiler_params=_cparams(("arbitrary",)),
        name="ada_mod",
    )(c_pad, w_ada, b_ada)


def _norm_rope_t(t, g_col, cos_t, sin_t):
    outs = []
    for i in range(t.shape[0] // HEAD_DIM):
        blk = t[i * HEAD_DIM:(i + 1) * HEAD_DIM, :]
        ms = jnp.mean(blk * blk, axis=0, keepdims=True)
        y = blk * lax.rsqrt(ms + NORM_EPS) * g_col
        y1 = y[:HALF]
        y2 = y[HALF:]
        outs.append(y1 * cos_t - y2 * sin_t)
        outs.append(y2 * cos_t + y1 * sin_t)
    return jnp.concatenate(outs, axis=0)


def _inproj_kernel(x_ref, mod_ref, g_ref, pos_ref, inv_ref, gq_ref, gk_ref, wt_ref, wsk_ref,
                   qd_ref, kd_ref, vd_ref, qs_ref, ks_ref, vs_ref):
    x = x_ref[0]
    ms = jnp.mean(x * x, axis=-1, keepdims=True)
    sh = mod_ref[0, 0:1, :]
    sc = mod_ref[0, 1:2, :]
    h = x * lax.rsqrt(ms + NORM_EPS) * g_ref[...] * (1.0 + sc) + sh
    hb = h.astype(BF16)
    w = qd_ref.shape[1]

    def proj_t(sec):
        return _dot_nt(wt_ref[sec * w:(sec + 1) * w, :], hb)

    ang = inv_ref[...] * pos_ref[0].astype(F32)
    cos_t = jnp.cos(ang)
    sin_t = jnp.sin(ang)
    q_t = _norm_rope_t(proj_t(0), gq_ref[...], cos_t, sin_t) * (Q_SCALE * LOG2E)
    qd_ref[0] = q_t.astype(BF16)
    k_t = _norm_rope_t(proj_t(1), gk_ref[...], cos_t, sin_t)
    kd_ref[0] = k_t.T.astype(BF16)
    v_t = proj_t(2).astype(BF16)
    s_t = proj_t(4).astype(BF16)
    nh = vd_ref.shape[1]
    hw = vs_ref.shape[3]
    ones = jnp.ones((SUM_ROWS, TB), BF16)
    for hh in range(nh):
        for jj in range(vd_ref.shape[2]):
            vd_ref[0, hh, jj] = jnp.concatenate(
                [v_t[hh * hw:(hh + 1) * hw, jj * TB:(jj + 1) * TB], ones], axis=0)
            vs_ref[0, hh, jj] = s_t[hh * hw:(hh + 1) * hw, jj * TB:(jj + 1) * TB]
    qs_ref[0] = (proj_t(3) * Q_SCALE).astype(BF16)
    ks_ref[0] = _dot(hb, wsk_ref[...]).astype(BF16)


def _inproj(x, mod, g_mix, pos3, inv, g_q, g_k, w_t, w_sk):
    b, s, d = x.shape
    w = w_sk.shape[1]
    nh = w // 128
    nsb = TS // TB
    const = lambda shape: pl.BlockSpec(shape, lambda bi, si: (0,) * len(shape))
    t_spec = pl.BlockSpec((1, w, TS), lambda bi, si: (bi, 0, si))
    n_spec = pl.BlockSpec((1, TS, w), lambda bi, si: (bi, si, 0))
    v_spec = lambda rows: pl.BlockSpec((1, nh, nsb, rows, TB), lambda bi, si: (bi, 0, si, 0, 0))
    t_shape = jax.ShapeDtypeStruct((b, w, s), BF16)
    n_shape = jax.ShapeDtypeStruct((b, s, w), BF16)
    v_shape = lambda rows: jax.ShapeDtypeStruct((b, nh, s // TB, rows, TB), BF16)
    return pl.pallas_call(
        _inproj_kernel,
        out_shape=(t_shape, n_shape, v_shape(DV_ROWS), t_shape, n_shape, v_shape(128)),
        grid=(b, s // TS),
        in_specs=[pl.BlockSpec((1, TS, d), lambda bi, si: (bi, si, 0)),
                  pl.BlockSpec((1, N_MOD, d), lambda bi, si: (bi, 0, 0)),
                  const((1, d)),
                  pl.BlockSpec((1, 1, TS), lambda bi, si: (bi, 0, si)),
                  const((HALF, 1)), const((HEAD_DIM, 1)), const((HEAD_DIM, 1)),
                  const(w_t.shape), const(w_sk.shape)],
        out_specs=(t_spec, n_spec, v_spec(DV_ROWS), t_spec, n_spec, v_spec(128)),
        compiler_params=_cparams(("parallel", "parallel")),
        name="norm_inproj",
    )(x, mod, g_mix, pos3, inv, g_q, g_k, w_t, w_sk)


def _q_pair(q_t):
    row = lax.broadcasted_iota(I32, q_t.shape, 0)
    zero = jnp.zeros_like(q_t)
    return jnp.concatenate([jnp.where(row < HEAD_DIM, q_t, zero),
                            jnp.where(row >= HEAD_DIM, q_t, zero)], axis=1)


def _key_query_iotas(tk, tq):
    r = lax.broadcasted_iota(I32, (tk, 2 * tq), 0)
    c = lax.broadcasted_iota(I32, (tk, 2 * tq), 1)
    return r, jnp.where(c >= tq, c - tq, c)


def _diff_kernel(qt_ref, k_ref, vt_ref, lam_ref, gsub_ref, o_ref, m_ref, acc_ref):
    qi = pl.program_id(2)
    tq = qt_ref.shape[2]
    dv = 2 * HEAD_DIM
    qcats = [_q_pair(qt_ref[0, st * 128:(st + 1) * 128, :]) for st in range(NSTREAM)]
    m_ref[...] = jnp.full(m_ref.shape, -jnp.inf, F32)
    acc_ref[...] = jnp.zeros(acc_ref.shape, F32)

    def step(j, diagonal):
        row0 = pl.multiple_of(j * TB, TB)
        scores = [_dot(k_ref[0, pl.ds(row0, TB), st * 128:(st + 1) * 128], qcats[st])
                  for st in range(NSTREAM)]
        soft = []
        for st in range(NSTREAM):
            s = scores[st]
            if diagonal:
                r, c = _key_query_iotas(TB, tq)
                s = jnp.where(r <= c, s, -jnp.inf)
            m = m_ref[st]
            m_new = jnp.maximum(m, jnp.max(s, axis=0, keepdims=True))
            alpha = jnp.exp2(m - m_new)
            p = jnp.exp2(s - m_new)
            m_ref[st] = m_new
            soft.append((alpha, p.astype(BF16)))
        for st in range(NSTREAM):
            alpha, p = soft[st]
            acc_ref[st] = acc_ref[st] * alpha + _dot(vt_ref[0, st, j], p)

    def body(j, carry):
        step(j, False)
        return carry

    lax.fori_loop(0, qi, body, 0)
    step(qi, True)

    lam = lam_ref[...]
    l1 = jnp.sum(lam[0:1] * lam[1:2], axis=-1, keepdims=True)
    l2 = jnp.sum(lam[2:3] * lam[3:4], axis=-1, keepdims=True)
    lmbda = jnp.exp(l1) - jnp.exp(l2) + LAMBDA_INIT
    for st in range(NSTREAM):
        acc = acc_ref[st, :dv, :]
        l = acc_ref[st, dv:dv + 1, :]
        o = acc[:, :tq] / l[:, :tq] - lmbda * (acc[:, tq:] / l[:, tq:])
        ms = jnp.mean(o * o, axis=0, keepdims=True)
        y = o * lax.rsqrt(ms + NORM_EPS) * gsub_ref[...] * (1.0 - LAMBDA_INIT)
        o_ref[0, :, st * 128:(st + 1) * 128] = y.T.astype(BF16)


def _attention_specs(b, w, s, v_rows):
    gw = NSTREAM * 128
    in_specs = [pl.BlockSpec((1, gw, TB), lambda bi, gi, qi: (bi, gi, qi)),
                pl.BlockSpec((1, s, gw), lambda bi, gi, qi: (bi, 0, gi)),
                pl.BlockSpec((1, NSTREAM, s // TB, v_rows, TB), lambda bi, gi, qi: (bi, gi, 0, 0, 0))]
    out_spec = pl.BlockSpec((1, TB, gw), lambda bi, gi, qi: (bi, qi, gi))
    return (b, w // gw, s // TB), in_specs, out_spec


def _diff_attention(q_t, k, v_t, lam, g_sub):
    b, w, s = q_t.shape
    grid, in_specs, out_spec = _attention_specs(b, w, s, DV_ROWS)
    return pl.pallas_call(
        _diff_kernel,
        out_shape=jax.ShapeDtypeStruct((b, s, w), BF16),
        grid=grid,
        in_specs=in_specs + [pl.BlockSpec(lam.shape, lambda bi, gi, qi: (0, 0)),
                             pl.BlockSpec(g_sub.shape, lambda bi, gi, qi: (0, 0))],
        out_specs=out_spec,
        scratch_shapes=[pltpu.VMEM((NSTREAM, 1, 2 * TB), F32), pltpu.VMEM((NSTREAM, DV_ROWS, 2 * TB), F32)],
        compiler_params=_cparams(("parallel", "parallel", "parallel")),
        name="diff_attention",
    )(q_t, k, v_t, lam, g_sub)


def _sb_kernel(qt_ref, k_ref, vt_ref, o_ref, c_ref, acc_ref):
    qi = pl.program_id(2)
    tq = qt_ref.shape[2]
    qcats = [_q_pair(qt_ref[0, st * 128:(st + 1) * 128, :]) for st in range(NSTREAM)]
    ur = lax.broadcasted_iota(I32, (TB, TB), 0)
    uc = lax.broadcasted_iota(I32, (TB, TB), 1)
    u_neg = jnp.where(uc > ur, -1.0, 0.0).astype(BF16)
    c_ref[...] = jnp.zeros(c_ref.shape, F32)
    acc_ref[...] = jnp.zeros(acc_ref.shape, F32)

    def step(j, diagonal):
        row0 = pl.multiple_of(j * TB, TB)
        if diagonal:
            r, c = _key_query_iotas(TB, tq)
            valid = r < c
        zs = [_dot(k_ref[0, pl.ds(row0, TB), st * 128:(st + 1) * 128], qcats[st])
              for st in range(NSTREAM)]
        pre = []
        for st in range(NSTREAM):
            z = zs[st]
            sp = jnp.maximum(z, 0.0) + jnp.log(1.0 + jnp.exp(-jnp.abs(z)))
            drop = jnp.where(valid, sp, 0.0) if diagonal else sp
            pre.append((z - sp, drop.astype(BF16), jnp.sum(drop, axis=0, keepdims=True)))
        laters = [_dot(u_neg, pre[st][1]) for st in range(NSTREAM)]
        for st in range(NSTREAM):
            c_run = c_ref[st]
            w = jnp.exp(pre[st][0] + laters[st] + c_run)
            if diagonal:
                w = jnp.where(valid, w, 0.0)
            wb = w.astype(BF16)
            vj = vt_ref[0, st, j]
            acc_ref[st] += jnp.concatenate([_dot(vj[:HEAD_DIM], wb[:, :tq]),
                                            _dot(vj[HEAD_DIM:], wb[:, tq:])], axis=0)
            c_ref[st] = c_run - pre[st][2]

    def live():
        return jnp.max(c_ref[...]) > SB_LOG_FLOOR

    step(qi, True)

    def body(i):
        step(qi - 1 - i, False)
        return i + 1

    lax.while_loop(lambda i: (i < qi) & live(), body, jnp.int32(0))
    for st in range(NSTREAM):
        o_ref[0, :, st * 128:(st + 1) * 128] = acc_ref[st].T.astype(BF16)


def _sb_attention(q_t, k, v_t):
    b, w, s = q_t.shape
    grid, in_specs, out_spec = _attention_specs(b, w, s, 2 * HEAD_DIM)
    return pl.pallas_call(
        _sb_kernel,
        out_shape=jax.ShapeDtypeStruct((b, s, w), BF16),
        grid=grid,
        in_specs=in_specs,
        out_specs=out_spec,
        scratch_shapes=[pltpu.VMEM((NSTREAM, 1, 2 * TB), F32), pltpu.VMEM((NSTREAM, 2 * HEAD_DIM, TB), F32)],
        compiler_params=_cparams(("parallel", "parallel", "parallel")),
        name="sb_attention",
    )(q_t, k, v_t)


def _first_index_of_max(v, idx, size):
    m = jnp.max(v, axis=0, keepdims=True)
    first = jnp.min(jnp.where(v == m, idx, size), axis=0, keepdims=True)
    return m, first


def _route_kernel(md_ref, ms_ref, x_ref, mod_ref, wo_ref, g_ref, wrh_ref, wrl_ref, eb_ref,
                  r1_ref, h2a_ref, h2b_ref, idx_ref, wn_ref, rank_ref, cnt_ref, base_ref):
    i = pl.program_id(0)

    @pl.when(i == 0)
    def _():
        base_ref[...] = jnp.zeros_like(base_ref)

    half = md_ref.shape[1]
    attn = _dot(md_ref[...], wo_ref[:half, :]) + _dot(ms_ref[...], wo_ref[half:, :])
    r1 = x_ref[...] + mod_ref[0, 2:3, :] * attn
    r1_ref[...] = r1
    ms = jnp.mean(r1 * r1, axis=-1, keepdims=True)
    h2 = r1 * lax.rsqrt(ms + NORM_EPS) * g_ref[...] * (1.0 + mod_ref[0, 4:5, :]) + mod_ref[0, 3:4, :]
    h2a_ref[...], h2b_ref[...] = _pack_rows(h2)

    hh, hl = _split_bf16(h2)
    wh = wrh_ref[...]
    logits = _dot_nt(wh, hh) + _dot_nt(wh, hl) + _dot_nt(wrl_ref[...], hh)
    scores = 1.0 / (1.0 + jnp.exp(-logits))
    biased = scores + eb_ref[...]
    tt = scores.shape[1]

    gi = lax.broadcasted_iota(I32, (GROUP_SIZE, tt), 0)
    gs = []
    for g in range(N_GROUPS):
        blk = biased[g * GROUP_SIZE:(g + 1) * GROUP_SIZE, :]
        m1, first = _first_index_of_max(blk, gi, GROUP_SIZE)
        m2 = jnp.max(jnp.where(gi == first, -jnp.inf, blk), axis=0, keepdims=True)
        gs.append(m1 + m2)
    gscore = jnp.concatenate(gs, axis=0)
    gidx = lax.broadcasted_iota(I32, (N_GROUPS, tt), 0)
    gsel = jnp.zeros((N_GROUPS, tt), F32)
    for _ in range(TOPK_GROUPS):
        _, first = _first_index_of_max(gscore, gidx, N_GROUPS)
        hit = gidx == first
        gsel = jnp.where(hit, 1.0, gsel)
        gscore = jnp.where(hit, -jnp.inf, gscore)
    cand = jnp.concatenate(
        [jnp.where(gsel[g:g + 1, :] > 0.0, biased[g * GROUP_SIZE:(g + 1) * GROUP_SIZE, :], -jnp.inf)
         for g in range(N_GROUPS)], axis=0)

    ei = lax.broadcasted_iota(I32, (N_EXPERTS, tt), 0)
    picked = jnp.zeros((N_EXPERTS, tt), F32)
    idxs, ws, hits = [], [], []
    for _ in range(TOP_K):
        _, first = _first_index_of_max(cand, ei, N_EXPERTS)
        hit = ei == first
        idxs.append(first)
        ws.append(jnp.sum(jnp.where(hit, scores, 0.0), axis=0, keepdims=True))
        hits.append(hit)
        picked = jnp.where(hit, 1.0, picked)
        cand = jnp.where(hit, -jnp.inf, cand)
    idx_ref[...] = jnp.concatenate(idxs, axis=0)
    w_t = jnp.concatenate(ws, axis=0)
    w_t = w_t / jnp.sum(w_t, axis=0, keepdims=True) * ROUTED_SCALE
    pad = jnp.zeros((128 - TOP_K, tt), F32)
    wn_ref[...] = jnp.concatenate([w_t, pad], axis=0).T[:, :TOP_K]

    tr = lax.broadcasted_iota(I32, (tt, tt), 0)
    tc = lax.broadcasted_iota(I32, (tt, tt), 1)
    before = jnp.where(tr < tc, 1.0, 0.0).astype(BF16)
    seen = _dot(picked.astype(BF16), before) + base_ref[...]
    rank_ref[...] = jnp.concatenate(
        [jnp.sum(jnp.where(h, seen, 0.0), axis=0, keepdims=True) for h in hits], axis=0).astype(I32)
    base_ref[...] += jnp.sum(picked, axis=1, keepdims=True)
    cnt_ref[...] = base_ref[...].astype(I32)


def _outproj_route(mixed_d, mixed_s, x2, mod, w_out, g_ffn, wr_hi, wr_lo, e_bias, s):
    t, d = x2.shape
    half = mixed_d.shape[1]
    per_b = s // TT
    const = lambda shape: pl.BlockSpec(shape, lambda i: (0,) * len(shape))
    tok = lambda wd: pl.BlockSpec((TT, wd), lambda i: (i, 0))
    lane = pl.BlockSpec((TOP_K, TT), lambda i: (0, i))
    return pl.pallas_call(
        _route_kernel,
        out_shape=(jax.ShapeDtypeStruct((t, d), F32),
                   jax.ShapeDtypeStruct((t, d // 4), U32),
                   jax.ShapeDtypeStruct((t, d // 4), U32),
                   jax.ShapeDtypeStruct((TOP_K, t), I32),
                   jax.ShapeDtypeStruct((t, TOP_K), F32),
                   jax.ShapeDtypeStruct((TOP_K, t), I32),
                   jax.ShapeDtypeStruct((N_EXPERTS, 1), I32)),
        grid=(t // TT,),
        in_specs=[tok(half), tok(half), tok(d),
                  pl.BlockSpec((1, N_MOD, d), lambda i: (i // per_b, 0, 0)),
                  const(w_out.shape), const((1, d)), const(wr_hi.shape), const(wr_lo.shape),
                  const((N_EXPERTS, 1))],
        out_specs=(tok(d), tok(d // 4), tok(d // 4), lane, tok(TOP_K), lane, const((N_EXPERTS, 1))),
        scratch_shapes=[pltpu.VMEM((N_EXPERTS, 1), F32)],
        compiler_params=_cparams(("arbitrary",)),
        name="outproj_route",
    )(mixed_d, mixed_s, x2, mod, w_out, g_ffn, wr_hi, wr_lo, e_bias)


def _pos_kernel(idx_ref, rank_ref, pstart_ref, pos_ref):
    tt = idx_ref.shape[1]
    ei = lax.broadcasted_iota(I32, (N_EXPERTS, tt), 0)
    ps = pstart_ref[...]
    rows = []
    for k in range(TOP_K):
        hit = ei == idx_ref[k:k + 1, :]
        rows.append(jnp.sum(jnp.where(hit, ps, 0), axis=0, keepdims=True))
    pos_ref[...] = jnp.concatenate(rows, axis=0) + rank_ref[...]


def _slot_positions(idx_t, rank_t, pstart):
    k, t = idx_t.shape
    tp = 1024
    lane = pl.BlockSpec((k, tp), lambda i: (0, i))
    return pl.pallas_call(
        _pos_kernel,
        out_shape=jax.ShapeDtypeStruct((k, t), I32),
        grid=(t // tp,),
        in_specs=[lane, lane, pl.BlockSpec((N_EXPERTS, 1), lambda i: (0, 0))],
        out_specs=lane,
        compiler_params=_cparams(("parallel",)),
        name="slot_positions",
    )(idx_t, rank_t, pstart)


def _expert_kernel(be_ref, nv_ref, nu_ref, xa_ref, xb_ref, wgu_ref, wd_ref, ya_ref, yb_ref, wgu_b, wd_b):
    b = pl.program_id(0)
    nv = nv_ref[b]

    @pl.when(nv > 0)
    def _():
        prev = be_ref[jnp.maximum(b - 1, 0)]

        @pl.when((b == 0) | (be_ref[b] != prev))
        def _():
            wgu_b[...] = wgu_ref[0].astype(BF16)
            wd_b[...] = wd_ref[0].astype(BF16)

    for v in range(EXPERT_SUBS + 1):
        @pl.when(nv == v)
        def _(v=v):
            rows = v * EXPERT_ROWS
            if rows:
                x = _unpack_rows(xa_ref[:rows, :], xb_ref[:rows, :]).astype(BF16)
                gu = _dot(x, wgu_b[...])
                ff = gu.shape[1] // 2
                act = _silu(gu[:, :ff]) * gu[:, ff:]
                y = _dot(act.astype(BF16), wd_b[...])
                ya_ref[:rows, :], yb_ref[:rows, :] = _pack_rows(y)
            if rows < STEP_ROWS:
                ya_ref[rows:, :] = jnp.zeros((STEP_ROWS - rows, ya_ref.shape[1]), U32)
                yb_ref[rows:, :] = jnp.zeros((STEP_ROWS - rows, yb_ref.shape[1]), U32)


def _experts(block_e, n_valid, n_used, xs_a, xs_b, w_gu, w_down):
    p, dp = xs_a.shape
    n_steps = p // STEP_ROWS
    _, d, f2 = w_gu.shape
    _, f, _ = w_down.shape
    x_spec = pl.BlockSpec((STEP_ROWS, dp), lambda b, be, nv, nu: (jnp.minimum(b, nu[0] - 1), 0))
    y_spec = pl.BlockSpec((STEP_ROWS, dp), lambda b, be, nv, nu: (jnp.where(b < nu[0], b, n_steps), 0))
    grid_spec = pltpu.PrefetchScalarGridSpec(
        num_scalar_prefetch=3,
        grid=(n_steps,),
        in_specs=[x_spec, x_spec,
                  pl.BlockSpec((1, d, f2), lambda b, be, nv, nu: (be[b], 0, 0)),
                  pl.BlockSpec((1, f, d), lambda b, be, nv, nu: (be[b], 0, 0))],
        out_specs=(y_spec, y_spec),
        scratch_shapes=[pltpu.VMEM((d, f2), BF16), pltpu.VMEM((f, d), BF16)],
    )
    y_shape = jax.ShapeDtypeStruct((p + STEP_ROWS, dp), U32)
    return pl.pallas_call(
        _expert_kernel,
        out_shape=(y_shape, y_shape),
        grid_spec=grid_spec,
        compiler_params=_cparams(("arbitrary",)),
        name="routed_experts",
    )(block_e, n_valid, n_used, xs_a, xs_b, w_gu, w_down)


def _shared_kernel(r1_ref, h2a_ref, h2b_ref, mod_ref, wsg_ref, wsd_ref, o_ref):
    h2 = _unpack_rows(h2a_ref[...], h2b_ref[...]).astype(BF16)
    gu = _dot(h2, wsg_ref[...])
    ff = gu.shape[1] // 2
    y = _dot((_silu(gu[:, :ff]) * gu[:, ff:]).astype(BF16), wsd_ref[...])
    o_ref[...] = r1_ref[...] + mod_ref[0, 5:6, :] * y


def _shared(r1, h2a, h2b, mod, w_sh_gu, w_sh_down, s):
    t, d = r1.shape
    per_b = s // TT
    const = lambda shape: pl.BlockSpec(shape, lambda i: (0,) * len(shape))
    tok = lambda wd: pl.BlockSpec((TT, wd), lambda i: (i, 0))
    return pl.pallas_call(
        _shared_kernel,
        out_shape=jax.ShapeDtypeStruct((t, d), F32),
        grid=(t // TT,),
        in_specs=[tok(d), tok(d // 4), tok(d // 4),
                  pl.BlockSpec((1, N_MOD, d), lambda i: (i // per_b, 0, 0)),
                  const(w_sh_gu.shape), const(w_sh_down.shape)],
        out_specs=tok(d),
        compiler_params=_cparams(("parallel",)),
        name="shared_expert",
    )(r1, h2a, h2b, mod, w_sh_gu, w_sh_down)


def _combine_kernel(r2_ref, yga_ref, ygb_ref, wn_ref, mod_ref, o_ref):
    wn = wn_ref[...]
    y = wn[:, 0:1] * _unpack_rows(yga_ref[0], ygb_ref[0])
    for k in range(1, TOP_K):
        y = y + wn[:, k:k + 1] * _unpack_rows(yga_ref[k], ygb_ref[k])
    o_ref[...] = r2_ref[...] + mod_ref[0, 5:6, :] * y


def _combine(r2, yg_a, yg_b, w_nat, mod, s):
    t, d = r2.shape
    per_b = s // TT
    tok = lambda wd: pl.BlockSpec((TT, wd), lambda i: (i, 0))
    return pl.pallas_call(
        _combine_kernel,
        out_shape=jax.ShapeDtypeStruct((t, d), F32),
        grid=(t // TT,),
        in_specs=[tok(d),
                  pl.BlockSpec((TOP_K, TT, d // 4), lambda i: (0, i, 0)),
                  pl.BlockSpec((TOP_K, TT, d // 4), lambda i: (0, i, 0)),
                  tok(TOP_K),
                  pl.BlockSpec((1, N_MOD, d), lambda i: (i // per_b, 0, 0))],
        out_specs=tok(d),
        compiler_params=_cparams(("parallel",)),
        name="routed_combine",
    )(r2, yg_a, yg_b, w_nat, mod)


def _sc_mesh():
    return plsc.VectorSubcoreMesh(core_axis_name="c", subcore_axis_name="s")


def _sc_dispatch(xa, xb, pos_t, p):
    t, dc = xa.shape
    k = pos_t.shape[0]
    out = jax.ShapeDtypeStruct((p, dc), xa.dtype)

    @functools.partial(pl.kernel, out_type=(out, out), mesh=_sc_mesh())
    def kern(xa_hbm, xb_hbm, pos_hbm, oa_hbm, ob_hbm):
        for src, dst in ((xa_hbm, oa_hbm), (xb_hbm, ob_hbm)):
            def body(x_vmem, i_vmem, dst=dst):
                pltpu.sync_copy(x_vmem, dst.at[i_vmem.at[0]])

            pltpu.emit_pipeline(
                body,
                grid=(t // SC_WINDOW, k),
                in_specs=[pl.BlockSpec((SC_WINDOW, dc), lambda i, kk: (i, 0)),
                          pl.BlockSpec((1, SC_WINDOW), lambda i, kk: (kk, i))],
                out_specs=[],
                core_axis_name=("c", "s"),
                dimension_semantics=(pltpu.PARALLEL, pltpu.ARBITRARY),
            )(src, pos_hbm)

    return kern(xa, xb, pos_t)


def _sc_gather(ya, yb, pos_row):
    n = pos_row.shape[1]
    dc = ya.shape[1]
    out = jax.ShapeDtypeStruct((n, dc), ya.dtype)

    @functools.partial(pl.kernel, out_type=(out, out), mesh=_sc_mesh())
    def kern(ya_hbm, yb_hbm, pos_hbm, oa_hbm, ob_hbm):
        for src, dst in ((ya_hbm, oa_hbm), (yb_hbm, ob_hbm)):
            def body(i_vmem, o_vmem, src=src):
                pltpu.sync_copy(src.at[i_vmem.at[0]], o_vmem)

            pltpu.emit_pipeline(
                body,
                grid=(n // SC_WINDOW,),
                in_specs=[pl.BlockSpec((1, SC_WINDOW), lambda i: (0, i))],
                out_specs=[pl.BlockSpec((SC_WINDOW, dc), lambda i: (i, 0))],
                core_axis_name=("c", "s"),
                dimension_semantics=(pltpu.PARALLEL,),
            )(pos_hbm, dst)

    return kern(ya, yb, pos_row)


def kernel(x, c, positions, w_ada, b_ada, g_mix, w_in, g_q, g_k, lam, g_sub, w_out, g_ffn, w_router,
           e_bias, w_sh_gu, w_sh_down, w_gu, w_down):
    b, s, d = x.shape
    assert w_ada.shape[0] == 1 and s % TS == 0 and d % 256 == 0
    t = b * s
    wdt = w_in.shape[2] // 6

    c_pad = jnp.zeros((8, d), F32).at[:b].set(c.astype(F32))
    mod = _ada(c_pad, w_ada[0], b_ada)[:b].reshape(b, N_MOD, d)

    sec = [w_in[0][:, i * wdt:(i + 1) * wdt] for i in range(6)]
    w_t = jnp.concatenate([sec[0], sec[1], sec[2], sec[3], sec[5]], axis=1).T.astype(BF16)
    w_sk = sec[4].astype(BF16)
    inv = (ROPE_THETA ** (-jnp.arange(0, HEAD_DIM, 2, dtype=F32) / HEAD_DIM)).reshape(HALF, 1)
    pos3 = positions.astype(I32).reshape(b, 1, s)

    qd, kd, vd, qs, ks, vs = _inproj(x, mod, g_mix, pos3, inv, g_q.reshape(HEAD_DIM, 1),
                                     g_k.reshape(HEAD_DIM, 1), w_t, w_sk)
    mixed_d = _diff_attention(qd, kd, vd, lam[0], g_sub.reshape(2 * HEAD_DIM, 1))
    mixed_s = _sb_attention(qs, ks, vs)

    wr_t = w_router[0].T
    wr_hi = wr_t.astype(BF16)
    wr_lo = (wr_t - wr_hi.astype(F32)).astype(BF16)
    r1, h2a, h2b, idx_t, w_nat, rank_t, counts = _outproj_route(
        mixed_d.reshape(t, wdt), mixed_s.reshape(t, wdt), x.reshape(t, d), mod,
        w_out[0].astype(BF16), g_ffn, wr_hi, wr_lo, e_bias.reshape(N_EXPERTS, 1), s)

    counts = counts.reshape(N_EXPERTS)
    padded = (counts + STEP_ROWS - 1) // STEP_ROWS * STEP_ROWS
    pend = jnp.cumsum(padded)
    pstart = (pend - padded).astype(I32)
    n_steps = t * TOP_K // STEP_ROWS + N_EXPERTS
    step_row0 = jnp.arange(n_steps, dtype=pend.dtype) * STEP_ROWS
    block_e = jnp.minimum(jnp.sum(pend[None, :] <= step_row0[:, None], axis=1), N_EXPERTS - 1).astype(I32)
    rows_left = counts[block_e] - (step_row0 - pstart[block_e])
    n_valid = jnp.where(step_row0 < pend[-1],
                        jnp.clip((rows_left + EXPERT_ROWS - 1) // EXPERT_ROWS, 0, EXPERT_SUBS), 0).astype(I32)
    n_used = (pend[-1:] // STEP_ROWS).astype(I32)
    pos_t = _slot_positions(idx_t, rank_t, pstart.reshape(N_EXPERTS, 1))

    xs_a, xs_b = _sc_dispatch(h2a, h2b, pos_t, n_steps * STEP_ROWS)
    y_a, y_b = _experts(block_e, n_valid, n_used, xs_a, xs_b, w_gu[0], w_down[0])
    r2 = _shared(r1, h2a, h2b, mod, w_sh_gu[0].astype(BF16), w_sh_down[0].astype(BF16), s)
    yg_a, yg_b = _sc_gather(y_a, y_b, pos_t.reshape(1, TOP_K * t))
    dq4 = d // 4
    out = _combine(r2, yg_a.reshape(TOP_K, t, dq4), yg_b.reshape(TOP_K, t, dq4), w_nat, mod, s)
    return out.reshape(b, s, d).astype(x.dtype)
```

```python
import functools
import math

import jax
import jax.numpy as jnp
from jax import lax
from jax.experimental import pallas as pl
from jax.experimental.pallas import tpu as pltpu
from jax.experimental.pallas import tpu_sc as plsc

F32 = jnp.float32
BF16 = jnp.bfloat16
U32 = jnp.uint32
I32 = jnp.int32

HEAD_DIM = 64
HALF = HEAD_DIM // 2
ROPE_THETA = 10000.0
N_EXPERTS = 256
TOP_K = 8
N_GROUPS = 8
GROUP_SIZE = N_EXPERTS // N_GROUPS
TOPK_GROUPS = 4
EXPERT_FF = 256
ROUTED_SCALE = 2.5
EXPERT_ROWS = 128
EXPERT_SUBS = 5
STEP_ROWS = EXPERT_SUBS * EXPERT_ROWS
NORM_EPS = 1e-6
SB_LOG_FLOOR = -110.0
NSTREAM = 4
N_MOD = 6
LAMBDA_INIT = 0.8 - 0.6 * math.exp(-0.3 * 0)
Q_SCALE = HEAD_DIM ** -0.5
LOG2E = math.log2(math.e)
SUM_ROWS = 16
DV_ROWS = 2 * HEAD_DIM + SUM_ROWS

TS = 512
TB = 256
TT = 256
TT_SHARED = 1024
HI_MASK = 0xFFFF0000
SIGN_BIT = 0x80000000
SC_WINDOW = 128

VMEM_LIMIT = 48 * 1024 * 1024


def _cparams(sem):
    return pltpu.CompilerParams(dimension_semantics=sem, vmem_limit_bytes=VMEM_LIMIT)


def _split_bf16(a):
    hi = a.astype(BF16)
    lo = (a - hi.astype(F32)).astype(BF16)
    return hi, lo


def _dot(a, b):
    return jnp.dot(a, b, preferred_element_type=F32)


def _dot_nt(a, b):
    return lax.dot_general(a, b, (((1,), (1,)), ((), ())), preferred_element_type=F32)


def _silu(x):
    return x / (1.0 + jnp.exp(-x))


def _pack_bf16_pair(a, b):
    ab = pltpu.bitcast(a.astype(BF16).astype(F32), U32)
    bb = pltpu.bitcast(b.astype(BF16).astype(F32), U32)
    return (ab & jnp.uint32(HI_MASK)) | (bb >> 16)


def _unpack_bf16_pair(p):
    a = pltpu.bitcast(p & jnp.uint32(HI_MASK), F32)
    b = pltpu.bitcast(p << 16, F32)
    return a, b


def _pack_rows(v):
    q = v.shape[1] // 4
    return (_pack_bf16_pair(v[:, :q], v[:, q:2 * q]),
            _pack_bf16_pair(v[:, 2 * q:3 * q], v[:, 3 * q:]))


def _unpack_rows(pa, pb):
    return jnp.concatenate(_unpack_bf16_pair(pa) + _unpack_bf16_pair(pb), axis=1)


def _ada_kernel(c_ref, w_ref, b_ref, o_ref):
    cond = _silu(c_ref[...])
    ch, cl = _split_bf16(cond)
    wh, wl = _split_bf16(w_ref[...])
    o_ref[...] = _dot(ch, wh) + _dot(ch, wl) + _dot(cl, wh) + b_ref[...]


def _ada(c_pad, w_ada, b_ada):
    d, n = w_ada.shape
    tn = n // 4
    return pl.pallas_call(
        _ada_kernel,
        out_shape=jax.ShapeDtypeStruct((c_pad.shape[0], n), F32),
        grid=(n // tn,),
        in_specs=[pl.BlockSpec((c_pad.shape[0], d), lambda j: (0, 0)),
                  pl.BlockSpec((d, tn), lambda j: (0, j)),
                  pl.BlockSpec((1, tn), lambda j: (0, j))],
        out_specs=pl.BlockSpec((c_pad.shape[0], tn), lambda j: (0, j)),
        compiler_params=_cparams(("arbitrary",)),
        name="ada_mod",
    )(c_pad, w_ada, b_ada)


def _norm_rope_t(t, g_col, cos_t, sin_t):
    outs = []
    for i in range(t.shape[0] // HEAD_DIM):
        blk = t[i * HEAD_DIM:(i + 1) * HEAD_DIM, :]
        ms = jnp.mean(blk * blk, axis=0, keepdims=True)
        y = blk * lax.rsqrt(ms + NORM_EPS) * g_col
        y1 = y[:HALF]
        y2 = y[HALF:]
        outs.append(y1 * cos_t - y2 * sin_t)
        outs.append(y2 * cos_t + y1 * sin_t)
    return jnp.concatenate(outs, axis=0)


def _inproj_kernel(x_ref, mod_ref, g_ref, pos_ref, inv_ref, gq_ref, gk_ref, wt_ref, wsk_ref,
                   qd_ref, kd_ref, vd_ref, qs_ref, ks_ref, vs_ref):
    x = x_ref[0]
    ms = jnp.mean(x * x, axis=-1, keepdims=True)
    sh = mod_ref[0, 0:1, :]
    sc = mod_ref[0, 1:2, :]
    h = x * lax.rsqrt(ms + NORM_EPS) * g_ref[...] * (1.0 + sc) + sh
    hb = h.astype(BF16)
    w = qd_ref.shape[1]

    def proj_t(sec):
        return _dot_nt(wt_ref[sec * w:(sec + 1) * w, :], hb)

    ang = inv_ref[...] * pos_ref[0].astype(F32)
    cos_t = jnp.cos(ang)
    sin_t = jnp.sin(ang)
    q_t = _norm_rope_t(proj_t(0), gq_ref[...], cos_t, sin_t) * (Q_SCALE * LOG2E)
    qd_ref[0] = q_t.astype(BF16)
    k_t = _norm_rope_t(proj_t(1), gk_ref[...], cos_t, sin_t)
    kd_ref[0] = k_t.T.astype(BF16)
    v_t = proj_t(2).astype(BF16)
    s_t = proj_t(4).astype(BF16)
    nh = vd_ref.shape[1]
    hw = vs_ref.shape[3]
    ones = jnp.ones((SUM_ROWS, TB), BF16)
    for hh in range(nh):
        for jj in range(vd_ref.shape[2]):
            vd_ref[0, hh, jj] = jnp.concatenate(
                [v_t[hh * hw:(hh + 1) * hw, jj * TB:(jj + 1) * TB], ones], axis=0)
            vs_ref[0, hh, jj] = s_t[hh * hw:(hh + 1) * hw, jj * TB:(jj + 1) * TB]
    qs_ref[0] = (proj_t(3) * Q_SCALE).astype(BF16)
    ks_ref[0] = _dot(hb, wsk_ref[...]).astype(BF16)


def _inproj(x, mod, g_mix, pos3, inv, g_q, g_k, w_t, w_sk):
    b, s, d = x.shape
    w = w_sk.shape[1]
    nh = w // 128
    nsb = TS // TB
    const = lambda shape: pl.BlockSpec(shape, lambda bi, si: (0,) * len(shape))
    t_spec = pl.BlockSpec((1, w, TS), lambda bi, si: (bi, 0, si))
    n_spec = pl.BlockSpec((1, TS, w), lambda bi, si: (bi, si, 0))
    v_spec = lambda rows: pl.BlockSpec((1, nh, nsb, rows, TB), lambda bi, si: (bi, 0, si, 0, 0))
    t_shape = jax.ShapeDtypeStruct((b, w, s), BF16)
    n_shape = jax.ShapeDtypeStruct((b, s, w), BF16)
    v_shape = lambda rows: jax.ShapeDtypeStruct((b, nh, s // TB, rows, TB), BF16)
    return pl.pallas_call(
        _inproj_kernel,
        out_shape=(t_shape, n_shape, v_shape(DV_ROWS), t_shape, n_shape, v_shape(128)),
        grid=(b, s // TS),
        in_specs=[pl.BlockSpec((1, TS, d), lambda bi, si: (bi, si, 0)),
                  pl.BlockSpec((1, N_MOD, d), lambda bi, si: (bi, 0, 0)),
                  const((1, d)),
                  pl.BlockSpec((1, 1, TS), lambda bi, si: (bi, 0, si)),
                  const((HALF, 1)), const((HEAD_DIM, 1)), const((HEAD_DIM, 1)),
                  const(w_t.shape), const(w_sk.shape)],
        out_specs=(t_spec, n_spec, v_spec(DV_ROWS), t_spec, n_spec, v_spec(128)),
        compiler_params=_cparams(("parallel", "parallel")),
        name="norm_inproj",
    )(x, mod, g_mix, pos3, inv, g_q, g_k, w_t, w_sk)


def _q_pair(q_t):
    row = lax.broadcasted_iota(I32, q_t.shape, 0)
    zero = jnp.zeros_like(q_t)
    return jnp.concatenate([jnp.where(row < HEAD_DIM, q_t, zero),
                            jnp.where(row >= HEAD_DIM, q_t, zero)], axis=1)


def _key_query_iotas(tk, tq):
    r = lax.broadcasted_iota(I32, (tk, 2 * tq), 0)
    c = lax.broadcasted_iota(I32, (tk, 2 * tq), 1)
    return r, jnp.where(c >= tq, c - tq, c)


def _diff_kernel(qt_ref, k_ref, vt_ref, lam_ref, gsub_ref, o_ref, m_ref, acc_ref):
    qi = pl.program_id(2)
    tq = qt_ref.shape[2]
    dv = 2 * HEAD_DIM
    qcats = [_q_pair(qt_ref[0, st * 128:(st + 1) * 128, :]) for st in range(NSTREAM)]
    m_ref[...] = jnp.full(m_ref.shape, -jnp.inf, F32)
    acc_ref[...] = jnp.zeros(acc_ref.shape, F32)

    def step(j, diagonal):
        row0 = pl.multiple_of(j * TB, TB)
        scores = [_dot(k_ref[0, pl.ds(row0, TB), st * 128:(st + 1) * 128], qcats[st])
                  for st in range(NSTREAM)]
        soft = []
        for st in range(NSTREAM):
            s = scores[st]
            if diagonal:
                r, c = _key_query_iotas(TB, tq)
                s = jnp.where(r <= c, s, -jnp.inf)
            m = m_ref[st]
            m_new = jnp.maximum(m, jnp.max(s, axis=0, keepdims=True))
            alpha = jnp.exp2(m - m_new)
            p = jnp.exp2(s - m_new)
            m_ref[st] = m_new
            soft.append((alpha, p.astype(BF16)))
        for st in range(NSTREAM):
            alpha, p = soft[st]
            acc_ref[st] = acc_ref[st] * alpha + _dot(vt_ref[0, st, j], p)

    def body(j, carry):
        step(j, False)
        return carry

    lax.fori_loop(0, qi, body, 0)
    step(qi, True)

    lam = lam_ref[...]
    l1 = jnp.sum(lam[0:1] * lam[1:2], axis=-1, keepdims=True)
    l2 = jnp.sum(lam[2:3] * lam[3:4], axis=-1, keepdims=True)
    lmbda = jnp.exp(l1) - jnp.exp(l2) + LAMBDA_INIT
    for st in range(NSTREAM):
        acc = acc_ref[st, :dv, :]
        l = acc_ref[st, dv:dv + 1, :]
        o = acc[:, :tq] / l[:, :tq] - lmbda * (acc[:, tq:] / l[:, tq:])
        ms = jnp.mean(o * o, axis=0, keepdims=True)
        y = o * lax.rsqrt(ms + NORM_EPS) * gsub_ref[...] * (1.0 - LAMBDA_INIT)
        o_ref[0, :, st * 128:(st + 1) * 128] = y.T.astype(BF16)


def _attention_specs(b, w, s, v_rows):
    gw = NSTREAM * 128
    in_specs = [pl.BlockSpec((1, gw, TB), lambda bi, gi, qi: (bi, gi, qi)),
                pl.BlockSpec((1, s, gw), lambda bi, gi, qi: (bi, 0, gi)),
                pl.BlockSpec((1, NSTREAM, s // TB, v_rows, TB), lambda bi, gi, qi: (bi, gi, 0, 0, 0))]
    out_spec = pl.BlockSpec((1, TB, gw), lambda bi, gi, qi: (bi, qi, gi))
    return (b, w // gw, s // TB), in_specs, out_spec


def _diff_attention(q_t, k, v_t, lam, g_sub):
    b, w, s = q_t.shape
    grid, in_specs, out_spec = _attention_specs(b, w, s, DV_ROWS)
    return pl.pallas_call(
        _diff_kernel,
        out_shape=jax.ShapeDtypeStruct((b, s, w), BF16),
        grid=grid,
        in_specs=in_specs + [pl.BlockSpec(lam.shape, lambda bi, gi, qi: (0, 0)),
                             pl.BlockSpec(g_sub.shape, lambda bi, gi, qi: (0, 0))],
        out_specs=out_spec,
        scratch_shapes=[pltpu.VMEM((NSTREAM, 1, 2 * TB), F32), pltpu.VMEM((NSTREAM, DV_ROWS, 2 * TB), F32)],
        compiler_params=_cparams(("parallel", "parallel", "parallel")),
        name="diff_attention",
    )(q_t, k, v_t, lam, g_sub)


def _sb_kernel(qt_ref, k_ref, vt_ref, o_ref, c_ref, acc_ref):
    qi = pl.program_id(2)
    tq = qt_ref.shape[2]
    qcats = [_q_pair(qt_ref[0, st * 128:(st + 1) * 128, :]) for st in range(NSTREAM)]
    ur = lax.broadcasted_iota(I32, (TB, TB), 0)
    uc = lax.broadcasted_iota(I32, (TB, TB), 1)
    u_neg = jnp.where(uc > ur, -1.0, 0.0).astype(BF16)
    c_ref[...] = jnp.zeros(c_ref.shape, F32)
    acc_ref[...] = jnp.zeros(acc_ref.shape, F32)

    def step(j, diagonal):
        row0 = pl.multiple_of(j * TB, TB)
        if diagonal:
            r, c = _key_query_iotas(TB, tq)
            valid = r < c
        zs = [_dot(k_ref[0, pl.ds(row0, TB), st * 128:(st + 1) * 128], qcats[st])
              for st in range(NSTREAM)]
        pre = []
        for st in range(NSTREAM):
            z = zs[st]
            neg_abs = pltpu.bitcast(pltpu.bitcast(z, U32) | jnp.uint32(SIGN_BIT), F32)
            sp = jnp.maximum(z, 0.0) + jnp.log(1.0 + jnp.exp(neg_abs))
            drop = jnp.where(valid, sp, 0.0) if diagonal else sp
            pre.append((z - sp, drop.astype(BF16), drop[0:1, :]))
        laters = [_dot(u_neg, pre[st][1]) for st in range(NSTREAM)]
        for st in range(NSTREAM):
            c_run = c_ref[st]
            w = jnp.exp(pre[st][0] + laters[st] + c_run)
            if diagonal:
                w = jnp.where(valid, w, 0.0)
            wb = w.astype(BF16)
            vj = vt_ref[0, st, j]
            acc_ref[st] += jnp.concatenate([_dot(vj[:HEAD_DIM], wb[:, :tq]),
                                            _dot(vj[HEAD_DIM:], wb[:, tq:])], axis=0)
            c_ref[st] = c_run + laters[st][0:1, :] - pre[st][2]

    def live():
        return jnp.max(c_ref[...]) > SB_LOG_FLOOR

    step(qi, True)

    def body(i):
        step(qi - 1 - i, False)
        return i + 1

    lax.while_loop(lambda i: (i < qi) & live(), body, jnp.int32(0))
    for st in range(NSTREAM):
        o_ref[0, :, st * 128:(st + 1) * 128] = acc_ref[st].T.astype(BF16)


def _sb_attention(q_t, k, v_t):
    b, w, s = q_t.shape
    grid, in_specs, out_spec = _attention_specs(b, w, s, 2 * HEAD_DIM)
    return pl.pallas_call(
        _sb_kernel,
        out_shape=jax.ShapeDtypeStruct((b, s, w), BF16),
        grid=grid,
        in_specs=in_specs,
        out_specs=out_spec,
        scratch_shapes=[pltpu.VMEM((NSTREAM, 1, 2 * TB), F32), pltpu.VMEM((NSTREAM, 2 * HEAD_DIM, TB), F32)],
        compiler_params=_cparams(("parallel", "parallel", "parallel")),
        name="sb_attention",
    )(q_t, k, v_t)


def _first_index_of_max(v, idx, size):
    m = jnp.max(v, axis=0, keepdims=True)
    first = jnp.min(jnp.where(v == m, idx, size), axis=0, keepdims=True)
    return m, first


def _route_kernel(md_ref, ms_ref, x_ref, mod_ref, wo_ref, g_ref, wrh_ref, wrl_ref, eb_ref,
                  r1_ref, h2a_ref, h2b_ref, idx_ref, wn_ref, rank_ref, cnt_ref, base_ref):
    i = pl.program_id(0)

    @pl.when(i == 0)
    def _():
        base_ref[...] = jnp.zeros_like(base_ref)

    half = md_ref.shape[1]
    attn = _dot(md_ref[...], wo_ref[:half, :]) + _dot(ms_ref[...], wo_ref[half:, :])
    r1 = x_ref[...] + mod_ref[0, 2:3, :] * attn
    r1_ref[...] = r1
    ms = jnp.mean(r1 * r1, axis=-1, keepdims=True)
    h2 = r1 * lax.rsqrt(ms + NORM_EPS) * g_ref[...] * (1.0 + mod_ref[0, 4:5, :]) + mod_ref[0, 3:4, :]
    h2a_ref[...], h2b_ref[...] = _pack_rows(h2)

    hh, hl = _split_bf16(h2)
    wh = wrh_ref[...]
    logits = _dot_nt(wh, hh) + _dot_nt(wh, hl) + _dot_nt(wrl_ref[...], hh)
    scores = 1.0 / (1.0 + jnp.exp(-logits))
    biased = scores + eb_ref[...]
    tt = scores.shape[1]

    gi = lax.broadcasted_iota(I32, (GROUP_SIZE, tt), 0)
    gs = []
    for g in range(N_GROUPS):
        blk = biased[g * GROUP_SIZE:(g + 1) * GROUP_SIZE, :]
        m1, first = _first_index_of_max(blk, gi, GROUP_SIZE)
        m2 = jnp.max(jnp.where(gi == first, -jnp.inf, blk), axis=0, keepdims=True)
        gs.append(m1 + m2)
    gscore = jnp.concatenate(gs, axis=0)
    gidx = lax.broadcasted_iota(I32, (N_GROUPS, tt), 0)
    gsel = jnp.zeros((N_GROUPS, tt), F32)
    for _ in range(TOPK_GROUPS):
        _, first = _first_index_of_max(gscore, gidx, N_GROUPS)
        hit = gidx == first
        gsel = jnp.where(hit, 1.0, gsel)
        gscore = jnp.where(hit, -jnp.inf, gscore)
    cand = jnp.concatenate(
        [jnp.where(gsel[g:g + 1, :] > 0.0, biased[g * GROUP_SIZE:(g + 1) * GROUP_SIZE, :], -jnp.inf)
         for g in range(N_GROUPS)], axis=0)

    ei = lax.broadcasted_iota(I32, (N_EXPERTS, tt), 0)
    picked = jnp.zeros((N_EXPERTS, tt), F32)
    idxs, ws, hits = [], [], []
    for _ in range(TOP_K):
        _, first = _first_index_of_max(cand, ei, N_EXPERTS)
        hit = ei == first
        idxs.append(first)
        ws.append(jnp.sum(jnp.where(hit, scores, 0.0), axis=0, keepdims=True))
        hits.append(hit)
        picked = jnp.where(hit, 1.0, picked)
        cand = jnp.where(hit, -jnp.inf, cand)
    idx_ref[...] = jnp.concatenate(idxs, axis=0)
    w_t = jnp.concatenate(ws, axis=0)
    w_t = w_t / jnp.sum(w_t, axis=0, keepdims=True) * ROUTED_SCALE
    pad = jnp.zeros((128 - TOP_K, tt), F32)
    wn_ref[...] = jnp.concatenate([w_t, pad], axis=0).T[:, :TOP_K]

    tr = lax.broadcasted_iota(I32, (tt, tt), 0)
    tc = lax.broadcasted_iota(I32, (tt, tt), 1)
    before = jnp.where(tr < tc, 1.0, 0.0).astype(BF16)
    seen = _dot(picked.astype(BF16), before) + base_ref[...]
    rank_ref[...] = jnp.concatenate(
        [jnp.sum(jnp.where(h, seen, 0.0), axis=0, keepdims=True) for h in hits], axis=0).astype(I32)
    base_ref[...] += jnp.sum(picked, axis=1, keepdims=True)
    cnt_ref[...] = base_ref[...].astype(I32)


def _outproj_route(mixed_d, mixed_s, x2, mod, w_out, g_ffn, wr_hi, wr_lo, e_bias, s):
    t, d = x2.shape
    half = mixed_d.shape[1]
    per_b = s // TT
    const = lambda shape: pl.BlockSpec(shape, lambda i: (0,) * len(shape))
    tok = lambda wd: pl.BlockSpec((TT, wd), lambda i: (i, 0))
    lane = pl.BlockSpec((TOP_K, TT), lambda i: (0, i))
    return pl.pallas_call(
        _route_kernel,
        out_shape=(jax.ShapeDtypeStruct((t, d), F32),
                   jax.ShapeDtypeStruct((t, d // 4), U32),
                   jax.ShapeDtypeStruct((t, d // 4), U32),
                   jax.ShapeDtypeStruct((TOP_K, t), I32),
                   jax.ShapeDtypeStruct((t, TOP_K), F32),
                   jax.ShapeDtypeStruct((TOP_K, t), I32),
                   jax.ShapeDtypeStruct((N_EXPERTS, 1), I32)),
        grid=(t // TT,),
        in_specs=[tok(half), tok(half), tok(d),
                  pl.BlockSpec((1, N_MOD, d), lambda i: (i // per_b, 0, 0)),
                  const(w_out.shape), const((1, d)), const(wr_hi.shape), const(wr_lo.shape),
                  const((N_EXPERTS, 1))],
        out_specs=(tok(d), tok(d // 4), tok(d // 4), lane, tok(TOP_K), lane, const((N_EXPERTS, 1))),
        scratch_shapes=[pltpu.VMEM((N_EXPERTS, 1), F32)],
        compiler_params=_cparams(("arbitrary",)),
        name="outproj_route",
    )(mixed_d, mixed_s, x2, mod, w_out, g_ffn, wr_hi, wr_lo, e_bias)


def _pos_kernel(idx_ref, rank_ref, pstart_ref, pos_ref):
    tt = idx_ref.shape[1]
    ei = lax.broadcasted_iota(I32, (N_EXPERTS, tt), 0)
    ps = pstart_ref[...]
    rows = []
    for k in range(TOP_K):
        hit = ei == idx_ref[k:k + 1, :]
        rows.append(jnp.sum(jnp.where(hit, ps, 0), axis=0, keepdims=True))
    pos_ref[...] = jnp.concatenate(rows, axis=0) + rank_ref[...]


def _slot_positions(idx_t, rank_t, pstart):
    k, t = idx_t.shape
    tp = 1024
    lane = pl.BlockSpec((k, tp), lambda i: (0, i))
    return pl.pallas_call(
        _pos_kernel,
        out_shape=jax.ShapeDtypeStruct((k, t), I32),
        grid=(t // tp,),
        in_specs=[lane, lane, pl.BlockSpec((N_EXPERTS, 1), lambda i: (0, 0))],
        out_specs=lane,
        compiler_params=_cparams(("parallel",)),
        name="slot_positions",
    )(idx_t, rank_t, pstart)


def _expert_kernel(be_ref, nv_ref, nu_ref, xa_ref, xb_ref, wgu_ref, wd_ref, ya_ref, yb_ref, wgu_b, wd_b):
    b = pl.program_id(0)
    nv = nv_ref[b]

    @pl.when(nv > 0)
    def _():
        prev = be_ref[jnp.maximum(b - 1, 0)]

        @pl.when((b == 0) | (be_ref[b] != prev))
        def _():
            wgu_b[...] = wgu_ref[0].astype(BF16)
            wd_b[...] = wd_ref[0].astype(BF16)

    for v in range(EXPERT_SUBS + 1):
        @pl.when(nv == v)
        def _(v=v):
            rows = v * EXPERT_ROWS
            if rows:
                x = _unpack_rows(xa_ref[:rows, :], xb_ref[:rows, :]).astype(BF16)
                gu = _dot(x, wgu_b[...])
                ff = gu.shape[1] // 2
                act = _silu(gu[:, :ff]) * gu[:, ff:]
                y = _dot(act.astype(BF16), wd_b[...])
                ya_ref[:rows, :], yb_ref[:rows, :] = _pack_rows(y)
            if rows < STEP_ROWS:
                ya_ref[rows:, :] = jnp.zeros((STEP_ROWS - rows, ya_ref.shape[1]), U32)
                yb_ref[rows:, :] = jnp.zeros((STEP_ROWS - rows, yb_ref.shape[1]), U32)


def _experts(block_e, n_valid, n_used, xs_a, xs_b, w_gu, w_down):
    p, dp = xs_a.shape
    n_steps = p // STEP_ROWS
    _, d, f2 = w_gu.shape
    _, f, _ = w_down.shape
    x_spec = pl.BlockSpec((STEP_ROWS, dp), lambda b, be, nv, nu: (jnp.minimum(b, nu[0] - 1), 0))
    y_spec = pl.BlockSpec((STEP_ROWS, dp), lambda b, be, nv, nu: (jnp.where(b < nu[0], b, n_steps), 0))
    grid_spec = pltpu.PrefetchScalarGridSpec(
        num_scalar_prefetch=3,
        grid=(n_steps,),
        in_specs=[x_spec, x_spec,
                  pl.BlockSpec((1, d, f2), lambda b, be, nv, nu: (be[b], 0, 0)),
                  pl.BlockSpec((1, f, d), lambda b, be, nv, nu: (be[b], 0, 0))],
        out_specs=(y_spec, y_spec),
        scratch_shapes=[pltpu.VMEM((d, f2), BF16), pltpu.VMEM((f, d), BF16)],
    )
    y_shape = jax.ShapeDtypeStruct((p + STEP_ROWS, dp), U32)
    return pl.pallas_call(
        _expert_kernel,
        out_shape=(y_shape, y_shape),
        grid_spec=grid_spec,
        compiler_params=_cparams(("arbitrary",)),
        name="routed_experts",
    )(block_e, n_valid, n_used, xs_a, xs_b, w_gu, w_down)


def _shared_kernel(h2a_ref, h2b_ref, wsg_ref, wsd_ref, o_ref):
    h2 = _unpack_rows(h2a_ref[...], h2b_ref[...]).astype(BF16)
    gu = _dot(h2, wsg_ref[...])
    ff = gu.shape[1] // 2
    o_ref[...] = _dot((_silu(gu[:, :ff]) * gu[:, ff:]).astype(BF16), wsd_ref[...]).astype(BF16)


def _shared(h2a, h2b, w_sh_gu, w_sh_down):
    t, dq = h2a.shape
    d = w_sh_down.shape[1]
    const = lambda shape: pl.BlockSpec(shape, lambda i: (0,) * len(shape))
    tok = lambda wd: pl.BlockSpec((TT_SHARED, wd), lambda i: (i, 0))
    return pl.pallas_call(
        _shared_kernel,
        out_shape=jax.ShapeDtypeStruct((t, d), BF16),
        grid=(t // TT_SHARED,),
        in_specs=[tok(dq), tok(dq), const(w_sh_gu.shape), const(w_sh_down.shape)],
        out_specs=tok(d),
        compiler_params=_cparams(("parallel",)),
        name="shared_expert",
    )(h2a, h2b, w_sh_gu, w_sh_down)


def _combine_kernel(r1_ref, ysh_ref, yga_ref, ygb_ref, wn_ref, mod_ref, o_ref):
    wn = wn_ref[...]
    y = ysh_ref[...].astype(F32)
    for k in range(TOP_K):
        y = y + wn[:, k:k + 1] * _unpack_rows(yga_ref[k], ygb_ref[k])
    o_ref[...] = r1_ref[...] + mod_ref[0, 5:6, :] * y


def _combine(r1, y_sh, yg_a, yg_b, w_nat, mod, s):
    t, d = r1.shape
    per_b = s // TT
    tok = lambda wd: pl.BlockSpec((TT, wd), lambda i: (i, 0))
    return pl.pallas_call(
        _combine_kernel,
        out_shape=jax.ShapeDtypeStruct((t, d), F32),
        grid=(t // TT,),
        in_specs=[tok(d), tok(d),
                  pl.BlockSpec((TOP_K, TT, d // 4), lambda i: (0, i, 0)),
                  pl.BlockSpec((TOP_K, TT, d // 4), lambda i: (0, i, 0)),
                  tok(TOP_K),
                  pl.BlockSpec((1, N_MOD, d), lambda i: (i // per_b, 0, 0))],
        out_specs=tok(d),
        compiler_params=_cparams(("parallel",)),
        name="routed_combine",
    )(r1, y_sh, yg_a, yg_b, w_nat, mod)


def _sc_mesh():
    return plsc.VectorSubcoreMesh(core_axis_name="c", subcore_axis_name="s")


def _sc_dispatch(xa, xb, pos_t, p):
    t, dc = xa.shape
    k = pos_t.shape[0]
    out = jax.ShapeDtypeStruct((p, dc), xa.dtype)

    @functools.partial(pl.kernel, out_type=(out, out), mesh=_sc_mesh())
    def kern(xa_hbm, xb_hbm, pos_hbm, oa_hbm, ob_hbm):
        for src, dst in ((xa_hbm, oa_hbm), (xb_hbm, ob_hbm)):
            def body(x_vmem, i_vmem, dst=dst):
                pltpu.sync_copy(x_vmem, dst.at[i_vmem.at[0]])

            pltpu.emit_pipeline(
                body,
                grid=(t // SC_WINDOW, k),
                in_specs=[pl.BlockSpec((SC_WINDOW, dc), lambda i, kk: (i, 0)),
                          pl.BlockSpec((1, SC_WINDOW), lambda i, kk: (kk, i))],
                out_specs=[],
                core_axis_name=("c", "s"),
                dimension_semantics=(pltpu.PARALLEL, pltpu.ARBITRARY),
            )(src, pos_hbm)

    return kern(xa, xb, pos_t)


def _sc_gather(ya, yb, pos_row):
    n = pos_row.shape[1]
    dc = ya.shape[1]
    out = jax.ShapeDtypeStruct((n, dc), ya.dtype)

    @functools.partial(pl.kernel, out_type=(out, out), mesh=_sc_mesh())
    def kern(ya_hbm, yb_hbm, pos_hbm, oa_hbm, ob_hbm):
        for src, dst in ((ya_hbm, oa_hbm), (yb_hbm, ob_hbm)):
            def body(i_vmem, o_vmem, src=src):
                pltpu.sync_copy(src.at[i_vmem.at[0]], o_vmem)

            pltpu.emit_pipeline(
                body,
                grid=(n // SC_WINDOW,),
                in_specs=[pl.BlockSpec((1, SC_WINDOW), lambda i: (0, i))],
                out_specs=[pl.BlockSpec((SC_WINDOW, dc), lambda i: (i, 0))],
                core_axis_name=("c", "s"),
                dimension_semantics=(pltpu.PARALLEL,),
            )(pos_hbm, dst)

    return kern(ya, yb, pos_row)


def kernel(x, c, positions, w_ada, b_ada, g_mix, w_in, g_q, g_k, lam, g_sub, w_out, g_ffn, w_router,
           e_bias, w_sh_gu, w_sh_down, w_gu, w_down):
    b, s, d = x.shape
    assert w_ada.shape[0] == 1 and s % TS == 0 and d % 256 == 0 and (b * s) % TT_SHARED == 0
    t = b * s
    wdt = w_in.shape[2] // 6

    c_pad = jnp.zeros((8, d), F32).at[:b].set(c.astype(F32))
    mod = _ada(c_pad, w_ada[0], b_ada)[:b].reshape(b, N_MOD, d)

    sec = [w_in[0][:, i * wdt:(i + 1) * wdt] for i in range(6)]
    w_t = jnp.concatenate([sec[0], sec[1], sec[2], sec[3], sec[5]], axis=1).T.astype(BF16)
    w_sk = sec[4].astype(BF16)
    inv = (ROPE_THETA ** (-jnp.arange(0, HEAD_DIM, 2, dtype=F32) / HEAD_DIM)).reshape(HALF, 1)
    pos3 = positions.astype(I32).reshape(b, 1, s)

    qd, kd, vd, qs, ks, vs = _inproj(x, mod, g_mix, pos3, inv, g_q.reshape(HEAD_DIM, 1),
                                     g_k.reshape(HEAD_DIM, 1), w_t, w_sk)
    mixed_d = _diff_attention(qd, kd, vd, lam[0], g_sub.reshape(2 * HEAD_DIM, 1))
    mixed_s = _sb_attention(qs, ks, vs)

    wr_t = w_router[0].T
    wr_hi = wr_t.astype(BF16)
    wr_lo = (wr_t - wr_hi.astype(F32)).astype(BF16)
    r1, h2a, h2b, idx_t, w_nat, rank_t, counts = _outproj_route(
        mixed_d.reshape(t, wdt), mixed_s.reshape(t, wdt), x.reshape(t, d), mod,
        w_out[0].astype(BF16), g_ffn, wr_hi, wr_lo, e_bias.reshape(N_EXPERTS, 1), s)

    counts = counts.reshape(N_EXPERTS)
    padded = (counts + STEP_ROWS - 1) // STEP_ROWS * STEP_ROWS
    pend = jnp.cumsum(padded)
    pstart = (pend - padded).astype(I32)
    n_steps = t * TOP_K // STEP_ROWS + N_EXPERTS
    step_row0 = jnp.arange(n_steps, dtype=pend.dtype) * STEP_ROWS
    block_e = jnp.minimum(jnp.sum(pend[None, :] <= step_row0[:, None], axis=1), N_EXPERTS - 1).astype(I32)
    rows_left = counts[block_e] - (step_row0 - pstart[block_e])
    n_valid = jnp.where(step_row0 < pend[-1],
                        jnp.clip((rows_left + EXPERT_ROWS - 1) // EXPERT_ROWS, 0, EXPERT_SUBS), 0).astype(I32)
    n_used = (pend[-1:] // STEP_ROWS).astype(I32)
    pos_t = _slot_positions(idx_t, rank_t, pstart.reshape(N_EXPERTS, 1))

    xs_a, xs_b = _sc_dispatch(h2a, h2b, pos_t, n_steps * STEP_ROWS)
    y_a, y_b = _experts(block_e, n_valid, n_used, xs_a, xs_b, w_gu[0], w_down[0])
    y_sh = _shared(h2a, h2b, w_sh_gu[0].astype(BF16), w_sh_down[0].astype(BF16))
    yg_a, yg_b = _sc_gather(y_a, y_b, pos_t.reshape(1, TOP_K * t))
    dq4 = d // 4
    out = _combine(r1, y_sh, yg_a.reshape(TOP_K, t, dq4), yg_b.reshape(TOP_K, t, dq4), w_nat, mod, s)
    return out.reshape(b, s, d).astype(x.dtype)
```

```python
import functools
import math

import jax
import jax.numpy as jnp
from jax import lax
from jax.experimental import pallas as pl
from jax.experimental.pallas import tpu as pltpu
from jax.experimental.pallas import tpu_sc as plsc

F32 = jnp.float32
BF16 = jnp.bfloat16
U32 = jnp.uint32
I32 = jnp.int32

HEAD_DIM = 64
HALF = HEAD_DIM // 2
ROPE_THETA = 10000.0
N_EXPERTS = 256
TOP_K = 8
N_GROUPS = 8
GROUP_SIZE = N_EXPERTS // N_GROUPS
TOPK_GROUPS = 4
EXPERT_FF = 256
ROUTED_SCALE = 2.5
EXPERT_ROWS = 128
EXPERT_SUBS = 5
STEP_ROWS = EXPERT_SUBS * EXPERT_ROWS
NORM_EPS = 1e-6
SB_LOG_FLOOR = -110.0
NSTREAM = 4
N_MOD = 6
LAMBDA_INIT = 0.8 - 0.6 * math.exp(-0.3 * 0)
Q_SCALE = HEAD_DIM ** -0.5
LOG2E = math.log2(math.e)
SUM_ROWS = 16
DV_ROWS = 2 * HEAD_DIM + SUM_ROWS

TS = 512
TB = 256
TT = 256
TT_SHARED = 1024
HI_MASK = 0xFFFF0000
SIGN_BIT = 0x80000000
SC_WINDOW = 128

VMEM_LIMIT = 48 * 1024 * 1024


def _cparams(sem):
    return pltpu.CompilerParams(dimension_semantics=sem, vmem_limit_bytes=VMEM_LIMIT)


def _split_bf16(a):
    hi = a.astype(BF16)
    lo = (a - hi.astype(F32)).astype(BF16)
    return hi, lo


def _dot(a, b):
    return jnp.dot(a, b, preferred_element_type=F32)


def _dot_nt(a, b):
    return lax.dot_general(a, b, (((1,), (1,)), ((), ())), preferred_element_type=F32)


def _silu(x):
    return x / (1.0 + jnp.exp(-x))


def _pack_bf16_pair(a, b):
    ab = pltpu.bitcast(a.astype(BF16).astype(F32), U32)
    bb = pltpu.bitcast(b.astype(BF16).astype(F32), U32)
    return (ab & jnp.uint32(HI_MASK)) | (bb >> 16)


def _unpack_bf16_pair(p):
    a = pltpu.bitcast(p & jnp.uint32(HI_MASK), F32)
    b = pltpu.bitcast(p << 16, F32)
    return a, b


def _pack_rows(v):
    q = v.shape[1] // 4
    return (_pack_bf16_pair(v[:, :q], v[:, q:2 * q]),
            _pack_bf16_pair(v[:, 2 * q:3 * q], v[:, 3 * q:]))


def _unpack_rows(pa, pb):
    return jnp.concatenate(_unpack_bf16_pair(pa) + _unpack_bf16_pair(pb), axis=1)


def _ada_kernel(c_ref, w_ref, b_ref, o_ref):
    cond = _silu(c_ref[...])
    ch, cl = _split_bf16(cond)
    wh, wl = _split_bf16(w_ref[...])
    o_ref[...] = _dot(ch, wh) + _dot(ch, wl) + _dot(cl, wh) + b_ref[...]


def _ada(c_pad, w_ada, b_ada):
    d, n = w_ada.shape
    tn = n // 4
    return pl.pallas_call(
        _ada_kernel,
        out_shape=jax.ShapeDtypeStruct((c_pad.shape[0], n), F32),
        grid=(n // tn,),
        in_specs=[pl.BlockSpec((c_pad.shape[0], d), lambda j: (0, 0)),
                  pl.BlockSpec((d, tn), lambda j: (0, j)),
                  pl.BlockSpec((1, tn), lambda j: (0, j))],
        out_specs=pl.BlockSpec((c_pad.shape[0], tn), lambda j: (0, j)),
        compiler_params=_cparams(("arbitrary",)),
        name="ada_mod",
    )(c_pad, w_ada, b_ada)


def _norm_rope_t(t, g_col, cos_t, sin_t):
    outs = []
    for i in range(t.shape[0] // HEAD_DIM):
        blk = t[i * HEAD_DIM:(i + 1) * HEAD_DIM, :]
        ms = jnp.mean(blk * blk, axis=0, keepdims=True)
        y = blk * lax.rsqrt(ms + NORM_EPS) * g_col
        y1 = y[:HALF]
        y2 = y[HALF:]
        outs.append(y1 * cos_t - y2 * sin_t)
        outs.append(y2 * cos_t + y1 * sin_t)
    return jnp.concatenate(outs, axis=0)


def _inproj_kernel(x_ref, mod_ref, g_ref, pos_ref, inv_ref, gq_ref, gk_ref, wt_ref, wsk_ref,
                   qd_ref, kd_ref, vd_ref, qs_ref, ks_ref, vs_ref):
    x = x_ref[0]
    ms = jnp.mean(x * x, axis=-1, keepdims=True)
    sh = mod_ref[0, 0:1, :]
    sc = mod_ref[0, 1:2, :]
    h = x * lax.rsqrt(ms + NORM_EPS) * g_ref[...] * (1.0 + sc) + sh
    hb = h.astype(BF16)
    w = qd_ref.shape[1]

    def proj_t(sec):
        return _dot_nt(wt_ref[sec * w:(sec + 1) * w, :], hb)

    ang = inv_ref[...] * pos_ref[0].astype(F32)
    cos_t = jnp.cos(ang)
    sin_t = jnp.sin(ang)
    q_t = _norm_rope_t(proj_t(0), gq_ref[...], cos_t, sin_t) * (Q_SCALE * LOG2E)
    qd_ref[0] = q_t.astype(BF16)
    k_t = _norm_rope_t(proj_t(1), gk_ref[...], cos_t, sin_t)
    kd_ref[0] = k_t.T.astype(BF16)
    v_t = proj_t(2).astype(BF16)
    s_t = proj_t(4).astype(BF16)
    nh = vd_ref.shape[1]
    hw = vs_ref.shape[3]
    ones = jnp.ones((SUM_ROWS, TB), BF16)
    for hh in range(nh):
        for jj in range(vd_ref.shape[2]):
            vd_ref[0, hh, jj] = jnp.concatenate(
                [v_t[hh * hw:(hh + 1) * hw, jj * TB:(jj + 1) * TB], ones], axis=0)
            vs_ref[0, hh, jj] = s_t[hh * hw:(hh + 1) * hw, jj * TB:(jj + 1) * TB]
    qs_ref[0] = (proj_t(3) * Q_SCALE).astype(BF16)
    ks_ref[0] = _dot(hb, wsk_ref[...]).astype(BF16)


def _inproj(x, mod, g_mix, pos3, inv, g_q, g_k, w_t, w_sk):
    b, s, d = x.shape
    w = w_sk.shape[1]
    nh = w // 128
    nsb = TS // TB
    const = lambda shape: pl.BlockSpec(shape, lambda bi, si: (0,) * len(shape))
    t_spec = pl.BlockSpec((1, w, TS), lambda bi, si: (bi, 0, si))
    n_spec = pl.BlockSpec((1, TS, w), lambda bi, si: (bi, si, 0))
    v_spec = lambda rows: pl.BlockSpec((1, nh, nsb, rows, TB), lambda bi, si: (bi, 0, si, 0, 0))
    t_shape = jax.ShapeDtypeStruct((b, w, s), BF16)
    n_shape = jax.ShapeDtypeStruct((b, s, w), BF16)
    v_shape = lambda rows: jax.ShapeDtypeStruct((b, nh, s // TB, rows, TB), BF16)
    return pl.pallas_call(
        _inproj_kernel,
        out_shape=(t_shape, n_shape, v_shape(DV_ROWS), t_shape, n_shape, v_shape(128)),
        grid=(b, s // TS),
        in_specs=[pl.BlockSpec((1, TS, d), lambda bi, si: (bi, si, 0)),
                  pl.BlockSpec((1, N_MOD, d), lambda bi, si: (bi, 0, 0)),
                  const((1, d)),
                  pl.BlockSpec((1, 1, TS), lambda bi, si: (bi, 0, si)),
                  const((HALF, 1)), const((HEAD_DIM, 1)), const((HEAD_DIM, 1)),
                  const(w_t.shape), const(w_sk.shape)],
        out_specs=(t_spec, n_spec, v_spec(DV_ROWS), t_spec, n_spec, v_spec(128)),
        compiler_params=_cparams(("parallel", "parallel")),
        name="norm_inproj",
    )(x, mod, g_mix, pos3, inv, g_q, g_k, w_t, w_sk)


def _q_pair(q_t):
    row = lax.broadcasted_iota(I32, q_t.shape, 0)
    zero = jnp.zeros_like(q_t)
    return jnp.concatenate([jnp.where(row < HEAD_DIM, q_t, zero),
                            jnp.where(row >= HEAD_DIM, q_t, zero)], axis=1)


def _key_query_iotas(tk, tq):
    r = lax.broadcasted_iota(I32, (tk, 2 * tq), 0)
    c = lax.broadcasted_iota(I32, (tk, 2 * tq), 1)
    return r, jnp.where(c >= tq, c - tq, c)


def _diff_kernel(qt_ref, k_ref, vt_ref, lam_ref, gsub_ref, o_ref, m_ref, acc_ref, s0, s1, p0, p1, a0, a1):
    qi = pl.program_id(2)
    tq = qt_ref.shape[2]
    dv = 2 * HEAD_DIM
    qcats = [_q_pair(qt_ref[0, st * 128:(st + 1) * 128, :]) for st in range(NSTREAM)]
    m_ref[...] = jnp.full(m_ref.shape, -jnp.inf, F32)
    acc_ref[...] = jnp.zeros(acc_ref.shape, F32)
    s_slots, p_slots, a_slots = (s0, s1), (p0, p1), (a0, a1)
    p_slots[1][...] = jnp.zeros(p_slots[1].shape, BF16)
    a_slots[1][...] = jnp.ones(a_slots[1].shape, F32)

    def scores(j, slot):
        row0 = pl.multiple_of(j * TB, TB)
        for st in range(NSTREAM):
            s_slots[slot][st] = _dot(k_ref[0, pl.ds(row0, TB), st * 128:(st + 1) * 128], qcats[st])

    def softmax(slot, diagonal):
        for st in range(NSTREAM):
            s = s_slots[slot][st]
            if diagonal:
                r, c = _key_query_iotas(TB, tq)
                s = jnp.where(r <= c, s, -jnp.inf)
            m = m_ref[st]
            m_new = jnp.maximum(m, jnp.max(s, axis=0, keepdims=True))
            a_slots[slot][st] = jnp.exp2(m - m_new)
            p_slots[slot][st] = jnp.exp2(s - m_new).astype(BF16)
            m_ref[st] = m_new

    def weighted_values(j, slot):
        for st in range(NSTREAM):
            acc_ref[st] = (acc_ref[st] * a_slots[slot][st]
                           + _dot(vt_ref[0, st, j], p_slots[slot][st]))

    def trip(j, slot):
        scores(j + 1, 1 - slot)
        softmax(slot, False)
        weighted_values(jnp.maximum(j - 1, 0), 1 - slot)

    def body(i, carry):
        trip(2 * i, 0)
        trip(2 * i + 1, 1)
        return carry

    scores(0, 0)
    lax.fori_loop(0, qi // 2, body, 0)

    @pl.when(qi % 2 == 1)
    def _():
        trip(qi - 1, 0)
        softmax(1, True)
        weighted_values(qi - 1, 0)
        weighted_values(qi, 1)

    @pl.when(qi % 2 == 0)
    def _():
        softmax(0, True)
        weighted_values(jnp.maximum(qi - 1, 0), 1)
        weighted_values(qi, 0)

    lam = lam_ref[...]
    l1 = jnp.sum(lam[0:1] * lam[1:2], axis=-1, keepdims=True)
    l2 = jnp.sum(lam[2:3] * lam[3:4], axis=-1, keepdims=True)
    lmbda = jnp.exp(l1) - jnp.exp(l2) + LAMBDA_INIT
    for st in range(NSTREAM):
        acc = acc_ref[st, :dv, :]
        l = acc_ref[st, dv:dv + 1, :]
        o = acc[:, :tq] / l[:, :tq] - lmbda * (acc[:, tq:] / l[:, tq:])
        ms = jnp.mean(o * o, axis=0, keepdims=True)
        y = o * lax.rsqrt(ms + NORM_EPS) * gsub_ref[...] * (1.0 - LAMBDA_INIT)
        o_ref[0, :, st * 128:(st + 1) * 128] = y.T.astype(BF16)


def _attention_specs(b, w, s, v_rows):
    gw = NSTREAM * 128
    in_specs = [pl.BlockSpec((1, gw, TB), lambda bi, gi, qi: (bi, gi, qi)),
                pl.BlockSpec((1, s, gw), lambda bi, gi, qi: (bi, 0, gi)),
                pl.BlockSpec((1, NSTREAM, s // TB, v_rows, TB), lambda bi, gi, qi: (bi, gi, 0, 0, 0))]
    out_spec = pl.BlockSpec((1, TB, gw), lambda bi, gi, qi: (bi, qi, gi))
    return (b, w // gw, s // TB), in_specs, out_spec


def _diff_attention(q_t, k, v_t, lam, g_sub):
    b, w, s = q_t.shape
    grid, in_specs, out_spec = _attention_specs(b, w, s, DV_ROWS)
    return pl.pallas_call(
        _diff_kernel,
        out_shape=jax.ShapeDtypeStruct((b, s, w), BF16),
        grid=grid,
        in_specs=in_specs + [pl.BlockSpec(lam.shape, lambda bi, gi, qi: (0, 0)),
                             pl.BlockSpec(g_sub.shape, lambda bi, gi, qi: (0, 0))],
        out_specs=out_spec,
        scratch_shapes=[pltpu.VMEM((NSTREAM, 1, 2 * TB), F32), pltpu.VMEM((NSTREAM, DV_ROWS, 2 * TB), F32),
                        pltpu.VMEM((NSTREAM, TB, 2 * TB), F32), pltpu.VMEM((NSTREAM, TB, 2 * TB), F32),
                        pltpu.VMEM((NSTREAM, TB, 2 * TB), BF16), pltpu.VMEM((NSTREAM, TB, 2 * TB), BF16),
                        pltpu.VMEM((NSTREAM, 1, 2 * TB), F32), pltpu.VMEM((NSTREAM, 1, 2 * TB), F32)],
        compiler_params=_cparams(("parallel", "parallel", "parallel")),
        name="diff_attention",
    )(q_t, k, v_t, lam, g_sub)


def _sb_kernel(qt_ref, k_ref, vt_ref, o_ref, c_ref, acc_ref):
    qi = pl.program_id(2)
    tq = qt_ref.shape[2]
    qcats = [_q_pair(qt_ref[0, st * 128:(st + 1) * 128, :]) for st in range(NSTREAM)]
    ur = lax.broadcasted_iota(I32, (TB, TB), 0)
    uc = lax.broadcasted_iota(I32, (TB, TB), 1)
    u_neg = jnp.where(uc > ur, -1.0, 0.0).astype(BF16)
    c_ref[...] = jnp.zeros(c_ref.shape, F32)
    acc_ref[...] = jnp.zeros(acc_ref.shape, F32)

    def step(j, diagonal):
        row0 = pl.multiple_of(j * TB, TB)
        if diagonal:
            r, c = _key_query_iotas(TB, tq)
            valid = r < c
        zs = [_dot(k_ref[0, pl.ds(row0, TB), st * 128:(st + 1) * 128], qcats[st])
              for st in range(NSTREAM)]
        pre = []
        for st in range(NSTREAM):
            z = zs[st]
            neg_abs = pltpu.bitcast(pltpu.bitcast(z, U32) | jnp.uint32(SIGN_BIT), F32)
            sp = jnp.maximum(z, 0.0) + jnp.log(1.0 + jnp.exp(neg_abs))
            drop = jnp.where(valid, sp, 0.0) if diagonal else sp
            pre.append((z - sp, drop.astype(BF16), drop[0:1, :]))
        laters = [_dot(u_neg, pre[st][1]) for st in range(NSTREAM)]
        for st in range(NSTREAM):
            c_run = c_ref[st]
            w = jnp.exp(pre[st][0] + laters[st] + c_run)
            if diagonal:
                w = jnp.where(valid, w, 0.0)
            wb = w.astype(BF16)
            vj = vt_ref[0, st, j]
            acc_ref[st] += jnp.concatenate([_dot(vj[:HEAD_DIM], wb[:, :tq]),
                                            _dot(vj[HEAD_DIM:], wb[:, tq:])], axis=0)
            c_ref[st] = c_run + laters[st][0:1, :] - pre[st][2]

    def live():
        return jnp.max(c_ref[...]) > SB_LOG_FLOOR

    step(qi, True)

    def body(i):
        step(qi - 1 - i, False)
        return i + 1

    lax.while_loop(lambda i: (i < qi) & live(), body, jnp.int32(0))
    for st in range(NSTREAM):
        o_ref[0, :, st * 128:(st + 1) * 128] = acc_ref[st].T.astype(BF16)


def _sb_attention(q_t, k, v_t):
    b, w, s = q_t.shape
    grid, in_specs, out_spec = _attention_specs(b, w, s, 2 * HEAD_DIM)
    return pl.pallas_call(
        _sb_kernel,
        out_shape=jax.ShapeDtypeStruct((b, s, w), BF16),
        grid=grid,
        in_specs=in_specs,
        out_specs=out_spec,
        scratch_shapes=[pltpu.VMEM((NSTREAM, 1, 2 * TB), F32), pltpu.VMEM((NSTREAM, 2 * HEAD_DIM, TB), F32)],
        compiler_params=_cparams(("parallel", "parallel", "parallel")),
        name="sb_attention",
    )(q_t, k, v_t)


def _first_index_of_max(v, idx, size):
    m = jnp.max(v, axis=0, keepdims=True)
    first = jnp.min(jnp.where(v == m, idx, size), axis=0, keepdims=True)
    return m, first


def _route_kernel(md_ref, ms_ref, x_ref, mod_ref, wo_ref, g_ref, wrh_ref, wrl_ref, eb_ref,
                  r1_ref, h2a_ref, h2b_ref, idx_ref, wn_ref, rank_ref, cnt_ref, base_ref):
    i = pl.program_id(0)

    @pl.when(i == 0)
    def _():
        base_ref[...] = jnp.zeros_like(base_ref)

    half = md_ref.shape[1]
    attn = _dot(md_ref[...], wo_ref[:half, :]) + _dot(ms_ref[...], wo_ref[half:, :])
    r1 = x_ref[...] + mod_ref[0, 2:3, :] * attn
    r1_ref[...] = r1
    ms = jnp.mean(r1 * r1, axis=-1, keepdims=True)
    h2 = r1 * lax.rsqrt(ms + NORM_EPS) * g_ref[...] * (1.0 + mod_ref[0, 4:5, :]) + mod_ref[0, 3:4, :]
    h2a_ref[...], h2b_ref[...] = _pack_rows(h2)

    hh, hl = _split_bf16(h2)
    wh = wrh_ref[...]
    logits = _dot_nt(wh, hh) + _dot_nt(wh, hl) + _dot_nt(wrl_ref[...], hh)
    scores = 1.0 / (1.0 + jnp.exp(-logits))
    biased = scores + eb_ref[...]
    tt = scores.shape[1]

    gi = lax.broadcasted_iota(I32, (GROUP_SIZE, tt), 0)
    gs = []
    for g in range(N_GROUPS):
        blk = biased[g * GROUP_SIZE:(g + 1) * GROUP_SIZE, :]
        m1, first = _first_index_of_max(blk, gi, GROUP_SIZE)
        m2 = jnp.max(jnp.where(gi == first, -jnp.inf, blk), axis=0, keepdims=True)
        gs.append(m1 + m2)
    gscore = jnp.concatenate(gs, axis=0)
    gidx = lax.broadcasted_iota(I32, (N_GROUPS, tt), 0)
    gsel = jnp.zeros((N_GROUPS, tt), F32)
    for _ in range(TOPK_GROUPS):
        _, first = _first_index_of_max(gscore, gidx, N_GROUPS)
        hit = gidx == first
        gsel = jnp.where(hit, 1.0, gsel)
        gscore = jnp.where(hit, -jnp.inf, gscore)
    cand = jnp.concatenate(
        [jnp.where(gsel[g:g + 1, :] > 0.0, biased[g * GROUP_SIZE:(g + 1) * GROUP_SIZE, :], -jnp.inf)
         for g in range(N_GROUPS)], axis=0)

    ei = lax.broadcasted_iota(I32, (N_EXPERTS, tt), 0)
    picked = jnp.zeros((N_EXPERTS, tt), F32)
    idxs, ws, hits = [], [], []
    for _ in range(TOP_K):
        _, first = _first_index_of_max(cand, ei, N_EXPERTS)
        hit = ei == first
        idxs.append(first)
        ws.append(jnp.sum(jnp.where(hit, scores, 0.0), axis=0, keepdims=True))
        hits.append(hit)
        picked = jnp.where(hit, 1.0, picked)
        cand = jnp.where(hit, -jnp.inf, cand)
    idx_ref[...] = jnp.concatenate(idxs, axis=0)
    w_t = jnp.concatenate(ws, axis=0)
    w_t = w_t / jnp.sum(w_t, axis=0, keepdims=True) * ROUTED_SCALE
    pad = jnp.zeros((128 - TOP_K, tt), F32)
    wn_ref[...] = jnp.concatenate([w_t, pad], axis=0).T[:, :TOP_K]

    tr = lax.broadcasted_iota(I32, (tt, tt), 0)
    tc = lax.broadcasted_iota(I32, (tt, tt), 1)
    before = jnp.where(tr < tc, 1.0, 0.0).astype(BF16)
    seen = _dot(picked.astype(BF16), before) + base_ref[...]
    rank_ref[...] = jnp.concatenate(
        [jnp.sum(jnp.where(h, seen, 0.0), axis=0, keepdims=True) for h in hits], axis=0).astype(I32)
    base_ref[...] += jnp.sum(picked, axis=1, keepdims=True)
    cnt_ref[...] = base_ref[...].astype(I32)


def _outproj_route(mixed_d, mixed_s, x2, mod, w_out, g_ffn, wr_hi, wr_lo, e_bias, s):
    t, d = x2.shape
    half = mixed_d.shape[1]
    per_b = s // TT
    const = lambda shape: pl.BlockSpec(shape, lambda i: (0,) * len(shape))
    tok = lambda wd: pl.BlockSpec((TT, wd), lambda i: (i, 0))
    lane = pl.BlockSpec((TOP_K, TT), lambda i: (0, i))
    return pl.pallas_call(
        _route_kernel,
        out_shape=(jax.ShapeDtypeStruct((t, d), F32),
                   jax.ShapeDtypeStruct((t, d // 4), U32),
                   jax.ShapeDtypeStruct((t, d // 4), U32),
                   jax.ShapeDtypeStruct((TOP_K, t), I32),
                   jax.ShapeDtypeStruct((t, TOP_K), F32),
                   jax.ShapeDtypeStruct((TOP_K, t), I32),
                   jax.ShapeDtypeStruct((N_EXPERTS, 1), I32)),
        grid=(t // TT,),
        in_specs=[tok(half), tok(half), tok(d),
                  pl.BlockSpec((1, N_MOD, d), lambda i: (i // per_b, 0, 0)),
                  const(w_out.shape), const((1, d)), const(wr_hi.shape), const(wr_lo.shape),
                  const((N_EXPERTS, 1))],
        out_specs=(tok(d), tok(d // 4), tok(d // 4), lane, tok(TOP_K), lane, const((N_EXPERTS, 1))),
        scratch_shapes=[pltpu.VMEM((N_EXPERTS, 1), F32)],
        compiler_params=_cparams(("arbitrary",)),
        name="outproj_route",
    )(mixed_d, mixed_s, x2, mod, w_out, g_ffn, wr_hi, wr_lo, e_bias)


def _pos_kernel(idx_ref, rank_ref, pstart_ref, pos_ref):
    tt = idx_ref.shape[1]
    ei = lax.broadcasted_iota(I32, (N_EXPERTS, tt), 0)
    ps = pstart_ref[...]
    rows = []
    for k in range(TOP_K):
        hit = ei == idx_ref[k:k + 1, :]
        rows.append(jnp.sum(jnp.where(hit, ps, 0), axis=0, keepdims=True))
    pos_ref[...] = jnp.concatenate(rows, axis=0) + rank_ref[...]


def _slot_positions(idx_t, rank_t, pstart):
    k, t = idx_t.shape
    tp = 1024
    lane = pl.BlockSpec((k, tp), lambda i: (0, i))
    return pl.pallas_call(
        _pos_kernel,
        out_shape=jax.ShapeDtypeStruct((k, t), I32),
        grid=(t // tp,),
        in_specs=[lane, lane, pl.BlockSpec((N_EXPERTS, 1), lambda i: (0, 0))],
        out_specs=lane,
        compiler_params=_cparams(("parallel",)),
        name="slot_positions",
    )(idx_t, rank_t, pstart)


def _expert_kernel(be_ref, nv_ref, nu_ref, xa_ref, xb_ref, wgu_ref, wd_ref, ya_ref, yb_ref, wgu_b, wd_b):
    b = pl.program_id(0)
    nv = nv_ref[b]

    @pl.when(nv > 0)
    def _():
        prev = be_ref[jnp.maximum(b - 1, 0)]

        @pl.when((b == 0) | (be_ref[b] != prev))
        def _():
            wgu_b[...] = wgu_ref[0].astype(BF16)
            wd_b[...] = wd_ref[0].astype(BF16)

    for v in range(EXPERT_SUBS + 1):
        @pl.when(nv == v)
        def _(v=v):
            rows = v * EXPERT_ROWS
            if rows:
                x = _unpack_rows(xa_ref[:rows, :], xb_ref[:rows, :]).astype(BF16)
                gu = _dot(x, wgu_b[...])
                ff = gu.shape[1] // 2
                act = _silu(gu[:, :ff]) * gu[:, ff:]
                y = _dot(act.astype(BF16), wd_b[...])
                ya_ref[:rows, :], yb_ref[:rows, :] = _pack_rows(y)
            if rows < STEP_ROWS:
                ya_ref[rows:, :] = jnp.zeros((STEP_ROWS - rows, ya_ref.shape[1]), U32)
                yb_ref[rows:, :] = jnp.zeros((STEP_ROWS - rows, yb_ref.shape[1]), U32)


def _experts(block_e, n_valid, n_used, xs_a, xs_b, w_gu, w_down):
    p, dp = xs_a.shape
    n_steps = p // STEP_ROWS
    _, d, f2 = w_gu.shape
    _, f, _ = w_down.shape
    x_spec = pl.BlockSpec((STEP_ROWS, dp), lambda b, be, nv, nu: (jnp.minimum(b, nu[0] - 1), 0))
    y_spec = pl.BlockSpec((STEP_ROWS, dp), lambda b, be, nv, nu: (jnp.where(b < nu[0], b, n_steps), 0))
    grid_spec = pltpu.PrefetchScalarGridSpec(
        num_scalar_prefetch=3,
        grid=(n_steps,),
        in_specs=[x_spec, x_spec,
                  pl.BlockSpec((1, d, f2), lambda b, be, nv, nu: (be[b], 0, 0)),
                  pl.BlockSpec((1, f, d), lambda b, be, nv, nu: (be[b], 0, 0))],
        out_specs=(y_spec, y_spec),
        scratch_shapes=[pltpu.VMEM((d, f2), BF16), pltpu.VMEM((f, d), BF16)],
    )
    y_shape = jax.ShapeDtypeStruct((p + STEP_ROWS, dp), U32)
    return pl.pallas_call(
        _expert_kernel,
        out_shape=(y_shape, y_shape),
        grid_spec=grid_spec,
        compiler_params=_cparams(("arbitrary",)),
        name="routed_experts",
    )(block_e, n_valid, n_used, xs_a, xs_b, w_gu, w_down)


def _shared_kernel(h2a_ref, h2b_ref, wsg_ref, wsd_ref, o_ref):
    h2 = _unpack_rows(h2a_ref[...], h2b_ref[...]).astype(BF16)
    gu = _dot(h2, wsg_ref[...])
    ff = gu.shape[1] // 2
    o_ref[...] = _dot((_silu(gu[:, :ff]) * gu[:, ff:]).astype(BF16), wsd_ref[...]).astype(BF16)


def _shared(h2a, h2b, w_sh_gu, w_sh_down):
    t, dq = h2a.shape
    d = w_sh_down.shape[1]
    const = lambda shape: pl.BlockSpec(shape, lambda i: (0,) * len(shape))
    tok = lambda wd: pl.BlockSpec((TT_SHARED, wd), lambda i: (i, 0))
    return pl.pallas_call(
        _shared_kernel,
        out_shape=jax.ShapeDtypeStruct((t, d), BF16),
        grid=(t // TT_SHARED,),
        in_specs=[tok(dq), tok(dq), const(w_sh_gu.shape), const(w_sh_down.shape)],
        out_specs=tok(d),
        compiler_params=_cparams(("parallel",)),
        name="shared_expert",
    )(h2a, h2b, w_sh_gu, w_sh_down)


def _combine_kernel(r1_ref, ysh_ref, yga_ref, ygb_ref, wn_ref, mod_ref, o_ref):
    wn = wn_ref[...]
    y = ysh_ref[...].astype(F32)
    for k in range(TOP_K):
        y = y + wn[:, k:k + 1] * _unpack_rows(yga_ref[k], ygb_ref[k])
    o_ref[...] = r1_ref[...] + mod_ref[0, 5:6, :] * y


def _combine(r1, y_sh, yg_a, yg_b, w_nat, mod, s):
    t, d = r1.shape
    per_b = s // TT
    tok = lambda wd: pl.BlockSpec((TT, wd), lambda i: (i, 0))
    return pl.pallas_call(
        _combine_kernel,
        out_shape=jax.ShapeDtypeStruct((t, d), F32),
        grid=(t // TT,),
        in_specs=[tok(d), tok(d),
                  pl.BlockSpec((TOP_K, TT, d // 4), lambda i: (0, i, 0)),
                  pl.BlockSpec((TOP_K, TT, d // 4), lambda i: (0, i, 0)),
                  tok(TOP_K),
                  pl.BlockSpec((1, N_MOD, d), lambda i: (i // per_b, 0, 0))],
        out_specs=tok(d),
        compiler_params=_cparams(("parallel",)),
        name="routed_combine",
    )(r1, y_sh, yg_a, yg_b, w_nat, mod)


def _sc_mesh():
    return plsc.VectorSubcoreMesh(core_axis_name="c", subcore_axis_name="s")


def _sc_dispatch(xa, xb, pos_t, p):
    t, dc = xa.shape
    k = pos_t.shape[0]
    out = jax.ShapeDtypeStruct((p, dc), xa.dtype)

    @functools.partial(pl.kernel, out_type=(out, out), mesh=_sc_mesh())
    def kern(xa_hbm, xb_hbm, pos_hbm, oa_hbm, ob_hbm):
        for src, dst in ((xa_hbm, oa_hbm), (xb_hbm, ob_hbm)):
            def body(x_vmem, i_vmem, dst=dst):
                pltpu.sync_copy(x_vmem, dst.at[i_vmem.at[0]])

            pltpu.emit_pipeline(
                body,
                grid=(t // SC_WINDOW, k),
                in_specs=[pl.BlockSpec((SC_WINDOW, dc), lambda i, kk: (i, 0)),
                          pl.BlockSpec((1, SC_WINDOW), lambda i, kk: (kk, i))],
                out_specs=[],
                core_axis_name=("c", "s"),
                dimension_semantics=(pltpu.PARALLEL, pltpu.ARBITRARY),
            )(src, pos_hbm)

    return kern(xa, xb, pos_t)


def _sc_gather(ya, yb, pos_row):
    n = pos_row.shape[1]
    dc = ya.shape[1]
    out = jax.ShapeDtypeStruct((n, dc), ya.dtype)

    @functools.partial(pl.kernel, out_type=(out, out), mesh=_sc_mesh())
    def kern(ya_hbm, yb_hbm, pos_hbm, oa_hbm, ob_hbm):
        for src, dst in ((ya_hbm, oa_hbm), (yb_hbm, ob_hbm)):
            def body(i_vmem, o_vmem, src=src):
                pltpu.sync_copy(src.at[i_vmem.at[0]], o_vmem)

            pltpu.emit_pipeline(
                body,
                grid=(n // SC_WINDOW,),
                in_specs=[pl.BlockSpec((1, SC_WINDOW), lambda i: (0, i))],
                out_specs=[pl.BlockSpec((SC_WINDOW, dc), lambda i: (i, 0))],
                core_axis_name=("c", "s"),
                dimension_semantics=(pltpu.PARALLEL,),
            )(pos_hbm, dst)

    return kern(ya, yb, pos_row)


def kernel(x, c, positions, w_ada, b_ada, g_mix, w_in, g_q, g_k, lam, g_sub, w_out, g_ffn, w_router,
           e_bias, w_sh_gu, w_sh_down, w_gu, w_down):
    b, s, d = x.shape
    assert w_ada.shape[0] == 1 and s % TS == 0 and d % 256 == 0 and (b * s) % TT_SHARED == 0
    t = b * s
    wdt = w_in.shape[2] // 6

    c_pad = jnp.zeros((8, d), F32).at[:b].set(c.astype(F32))
    mod = _ada(c_pad, w_ada[0], b_ada)[:b].reshape(b, N_MOD, d)

    sec = [w_in[0][:, i * wdt:(i + 1) * wdt] for i in range(6)]
    w_t = jnp.concatenate([sec[0], sec[1], sec[2], sec[3], sec[5]], axis=1).T.astype(BF16)
    w_sk = sec[4].astype(BF16)
    inv = (ROPE_THETA ** (-jnp.arange(0, HEAD_DIM, 2, dtype=F32) / HEAD_DIM)).reshape(HALF, 1)
    pos3 = positions.astype(I32).reshape(b, 1, s)

    qd, kd, vd, qs, ks, vs = _inproj(x, mod, g_mix, pos3, inv, g_q.reshape(HEAD_DIM, 1),
                                     g_k.reshape(HEAD_DIM, 1), w_t, w_sk)
    mixed_d = _diff_attention(qd, kd, vd, lam[0], g_sub.reshape(2 * HEAD_DIM, 1))
    mixed_s = _sb_attention(qs, ks, vs)

    wr_t = w_router[0].T
    wr_hi = wr_t.astype(BF16)
    wr_lo = (wr_t - wr_hi.astype(F32)).astype(BF16)
    r1, h2a, h2b, idx_t, w_nat, rank_t, counts = _outproj_route(
        mixed_d.reshape(t, wdt), mixed_s.reshape(t, wdt), x.reshape(t, d), mod,
        w_out[0].astype(BF16), g_ffn, wr_hi, wr_lo, e_bias.reshape(N_EXPERTS, 1), s)

    counts = counts.reshape(N_EXPERTS)
    padded = (counts + STEP_ROWS - 1) // STEP_ROWS * STEP_ROWS
    pend = jnp.cumsum(padded)
    pstart = (pend - padded).astype(I32)
    n_steps = t * TOP_K // STEP_ROWS + N_EXPERTS
    step_row0 = jnp.arange(n_steps, dtype=pend.dtype) * STEP_ROWS
    block_e = jnp.minimum(jnp.sum(pend[None, :] <= step_row0[:, None], axis=1), N_EXPERTS - 1).astype(I32)
    rows_left = counts[block_e] - (step_row0 - pstart[block_e])
    n_valid = jnp.where(step_row0 < pend[-1],
                        jnp.clip((rows_left + EXPERT_ROWS - 1) // EXPERT_ROWS, 0, EXPERT_SUBS), 0).astype(I32)
    n_used = (pend[-1:] // STEP_ROWS).astype(I32)
    pos_t = _slot_positions(idx_t, rank_t, pstart.reshape(N_EXPERTS, 1))

    xs_a, xs_b = _sc_dispatch(h2a, h2b, pos_t, n_steps * STEP_ROWS)
    y_a, y_b = _experts(block_e, n_valid, n_used, xs_a, xs_b, w_gu[0], w_down[0])
    y_sh = _shared(h2a, h2b, w_sh_gu[0].astype(BF16), w_sh_down[0].astype(BF16))
    yg_a, yg_b = _sc_gather(y_a, y_b, pos_t.reshape(1, TOP_K * t))
    dq4 = d // 4
    out = _combine(r1, y_sh, yg_a.reshape(TOP_K, t, dq4), yg_b.reshape(TOP_K, t, dq4), w_nat, mod, s)
    return out.reshape(b, s, d).astype(x.dtype)
```

```python
import functools
import math

import jax
import jax.numpy as jnp
from jax import lax
from jax.experimental import pallas as pl
from jax.experimental.pallas import tpu as pltpu
from jax.experimental.pallas import tpu_sc as plsc

F32 = jnp.float32
BF16 = jnp.bfloat16
U32 = jnp.uint32
I32 = jnp.int32

HEAD_DIM = 64
HALF = HEAD_DIM // 2
ROPE_THETA = 10000.0
N_EXPERTS = 256
TOP_K = 8
N_GROUPS = 8
GROUP_SIZE = N_EXPERTS // N_GROUPS
TOPK_GROUPS = 4
EXPERT_FF = 256
ROUTED_SCALE = 2.5
EXPERT_ROWS = 128
EXPERT_SUBS = 5
STEP_ROWS = EXPERT_SUBS * EXPERT_ROWS
NORM_EPS = 1e-6
SB_LOG_FLOOR = -110.0
NSTREAM = 4
N_MOD = 6
LAMBDA_INIT = 0.8 - 0.6 * math.exp(-0.3 * 0)
Q_SCALE = HEAD_DIM ** -0.5
LOG2E = math.log2(math.e)
SUM_ROWS = 16
DV_ROWS = 2 * HEAD_DIM + SUM_ROWS

TS = 512
TB = 256
TT = 256
TT_SHARED = 1024
HI_MASK = 0xFFFF0000
SIGN_BIT = 0x80000000
SC_WINDOW = 128

VMEM_LIMIT = 48 * 1024 * 1024


def _cparams(sem):
    return pltpu.CompilerParams(dimension_semantics=sem, vmem_limit_bytes=VMEM_LIMIT)


def _split_bf16(a):
    hi = a.astype(BF16)
    lo = (a - hi.astype(F32)).astype(BF16)
    return hi, lo


def _dot(a, b):
    return jnp.dot(a, b, preferred_element_type=F32)


def _dot_nt(a, b):
    return lax.dot_general(a, b, (((1,), (1,)), ((), ())), preferred_element_type=F32)


def _silu(x):
    return x / (1.0 + jnp.exp(-x))


def _pack_bf16_pair(a, b):
    ab = pltpu.bitcast(a.astype(BF16).astype(F32), U32)
    bb = pltpu.bitcast(b.astype(BF16).astype(F32), U32)
    return (ab & jnp.uint32(HI_MASK)) | (bb >> 16)


def _unpack_bf16_pair(p):
    a = pltpu.bitcast(p & jnp.uint32(HI_MASK), F32)
    b = pltpu.bitcast(p << 16, F32)
    return a, b


def _pack_rows(v):
    q = v.shape[1] // 4
    return jnp.concatenate([_pack_bf16_pair(v[:, :q], v[:, q:2 * q]),
                            _pack_bf16_pair(v[:, 2 * q:3 * q], v[:, 3 * q:])], axis=1)


def _unpack_rows(p):
    h = p.shape[1] // 2
    return jnp.concatenate(_unpack_bf16_pair(p[:, :h]) + _unpack_bf16_pair(p[:, h:]), axis=1)


def _ada_kernel(c_ref, w_ref, b_ref, o_ref):
    cond = _silu(c_ref[...])
    ch, cl = _split_bf16(cond)
    wh, wl = _split_bf16(w_ref[...])
    o_ref[...] = _dot(ch, wh) + _dot(ch, wl) + _dot(cl, wh) + b_ref[...]


def _ada(c_pad, w_ada, b_ada):
    d, n = w_ada.shape
    tn = n // 4
    return pl.pallas_call(
        _ada_kernel,
        out_shape=jax.ShapeDtypeStruct((c_pad.shape[0], n), F32),
        grid=(n // tn,),
        in_specs=[pl.BlockSpec((c_pad.shape[0], d), lambda j: (0, 0)),
                  pl.BlockSpec((d, tn), lambda j: (0, j)),
                  pl.BlockSpec((1, tn), lambda j: (0, j))],
        out_specs=pl.BlockSpec((c_pad.shape[0], tn), lambda j: (0, j)),
        compiler_params=_cparams(("arbitrary",)),
        name="ada_mod",
    )(c_pad, w_ada, b_ada)


def _norm_rope_t(t, g_col, cos_t, sin_t):
    outs = []
    for i in range(t.shape[0] // HEAD_DIM):
        blk = t[i * HEAD_DIM:(i + 1) * HEAD_DIM, :]
        ms = jnp.mean(blk * blk, axis=0, keepdims=True)
        y = blk * lax.rsqrt(ms + NORM_EPS) * g_col
        y1 = y[:HALF]
        y2 = y[HALF:]
        outs.append(y1 * cos_t - y2 * sin_t)
        outs.append(y2 * cos_t + y1 * sin_t)
    return jnp.concatenate(outs, axis=0)


def _inproj_kernel(x_ref, mod_ref, g_ref, pos_ref, inv_ref, gq_ref, gk_ref, wt_ref, wsk_ref,
                   qd_ref, kd_ref, vd_ref, qs_ref, ks_ref, vs_ref):
    x = x_ref[0]
    ms = jnp.mean(x * x, axis=-1, keepdims=True)
    sh = mod_ref[0, 0:1, :]
    sc = mod_ref[0, 1:2, :]
    h = x * lax.rsqrt(ms + NORM_EPS) * g_ref[...] * (1.0 + sc) + sh
    hb = h.astype(BF16)
    w = qd_ref.shape[1]

    def proj_t(sec):
        return _dot_nt(wt_ref[sec * w:(sec + 1) * w, :], hb)

    ang = inv_ref[...] * pos_ref[0].astype(F32)
    cos_t = jnp.cos(ang)
    sin_t = jnp.sin(ang)
    q_t = _norm_rope_t(proj_t(0), gq_ref[...], cos_t, sin_t) * (Q_SCALE * LOG2E)
    qd_ref[0] = q_t.astype(BF16)
    k_t = _norm_rope_t(proj_t(1), gk_ref[...], cos_t, sin_t)
    kd_ref[0] = k_t.T.astype(BF16)
    v_t = proj_t(2).astype(BF16)
    s_t = proj_t(4).astype(BF16)
    nh = vd_ref.shape[1]
    hw = vs_ref.shape[3]
    ones = jnp.ones((SUM_ROWS, TB), BF16)
    for hh in range(nh):
        for jj in range(vd_ref.shape[2]):
            vd_ref[0, hh, jj] = jnp.concatenate(
                [v_t[hh * hw:(hh + 1) * hw, jj * TB:(jj + 1) * TB], ones], axis=0)
            vs_ref[0, hh, jj] = s_t[hh * hw:(hh + 1) * hw, jj * TB:(jj + 1) * TB]
    qs_ref[0] = (proj_t(3) * Q_SCALE).astype(BF16)
    ks_ref[0] = _dot(hb, wsk_ref[...]).astype(BF16)


def _inproj(x, mod, g_mix, pos3, inv, g_q, g_k, w_t, w_sk):
    b, s, d = x.shape
    w = w_sk.shape[1]
    nh = w // 128
    nsb = TS // TB
    const = lambda shape: pl.BlockSpec(shape, lambda bi, si: (0,) * len(shape))
    t_spec = pl.BlockSpec((1, w, TS), lambda bi, si: (bi, 0, si))
    n_spec = pl.BlockSpec((1, TS, w), lambda bi, si: (bi, si, 0))
    v_spec = lambda rows: pl.BlockSpec((1, nh, nsb, rows, TB), lambda bi, si: (bi, 0, si, 0, 0))
    t_shape = jax.ShapeDtypeStruct((b, w, s), BF16)
    n_shape = jax.ShapeDtypeStruct((b, s, w), BF16)
    v_shape = lambda rows: jax.ShapeDtypeStruct((b, nh, s // TB, rows, TB), BF16)
    return pl.pallas_call(
        _inproj_kernel,
        out_shape=(t_shape, n_shape, v_shape(DV_ROWS), t_shape, n_shape, v_shape(128)),
        grid=(b, s // TS),
        in_specs=[pl.BlockSpec((1, TS, d), lambda bi, si: (bi, si, 0)),
                  pl.BlockSpec((1, N_MOD, d), lambda bi, si: (bi, 0, 0)),
                  const((1, d)),
                  pl.BlockSpec((1, 1, TS), lambda bi, si: (bi, 0, si)),
                  const((HALF, 1)), const((HEAD_DIM, 1)), const((HEAD_DIM, 1)),
                  const(w_t.shape), const(w_sk.shape)],
        out_specs=(t_spec, n_spec, v_spec(DV_ROWS), t_spec, n_spec, v_spec(128)),
        compiler_params=_cparams(("parallel", "parallel")),
        name="norm_inproj",
    )(x, mod, g_mix, pos3, inv, g_q, g_k, w_t, w_sk)


def _q_pair(q_t):
    row = lax.broadcasted_iota(I32, q_t.shape, 0)
    zero = jnp.zeros_like(q_t)
    return jnp.concatenate([jnp.where(row < HEAD_DIM, q_t, zero),
                            jnp.where(row >= HEAD_DIM, q_t, zero)], axis=1)


def _key_query_iotas(tk, tq):
    r = lax.broadcasted_iota(I32, (tk, 2 * tq), 0)
    c = lax.broadcasted_iota(I32, (tk, 2 * tq), 1)
    return r, jnp.where(c >= tq, c - tq, c)


def _diff_kernel(qt_ref, k_ref, vt_ref, lam_ref, gsub_ref, o_ref, m_ref, acc_ref, s0, s1, p0, p1, a0, a1):
    qi = pl.program_id(2)
    tq = qt_ref.shape[2]
    dv = 2 * HEAD_DIM
    qcats = [_q_pair(qt_ref[0, st * 128:(st + 1) * 128, :]) for st in range(NSTREAM)]
    m_ref[...] = jnp.full(m_ref.shape, -jnp.inf, F32)
    acc_ref[...] = jnp.zeros(acc_ref.shape, F32)
    s_slots, p_slots, a_slots = (s0, s1), (p0, p1), (a0, a1)
    p_slots[1][...] = jnp.zeros(p_slots[1].shape, BF16)
    a_slots[1][...] = jnp.ones(a_slots[1].shape, F32)

    def scores(j, slot):
        row0 = pl.multiple_of(j * TB, TB)
        for st in range(NSTREAM):
            s_slots[slot][st] = _dot(k_ref[0, pl.ds(row0, TB), st * 128:(st + 1) * 128], qcats[st])

    def softmax(slot, diagonal):
        for st in range(NSTREAM):
            s = s_slots[slot][st]
            if diagonal:
                r, c = _key_query_iotas(TB, tq)
                s = jnp.where(r <= c, s, -jnp.inf)
            m = m_ref[st]
            m_new = jnp.maximum(m, jnp.max(s, axis=0, keepdims=True))
            a_slots[slot][st] = jnp.exp2(m - m_new)
            p_slots[slot][st] = jnp.exp2(s - m_new).astype(BF16)
            m_ref[st] = m_new

    def weighted_values(j, slot):
        for st in range(NSTREAM):
            acc_ref[st] = (acc_ref[st] * a_slots[slot][st]
                           + _dot(vt_ref[0, st, j], p_slots[slot][st]))

    def trip(j, slot):
        scores(j + 1, 1 - slot)
        softmax(slot, False)
        weighted_values(jnp.maximum(j - 1, 0), 1 - slot)

    def body(i, carry):
        trip(2 * i, 0)
        trip(2 * i + 1, 1)
        return carry

    scores(0, 0)
    lax.fori_loop(0, qi // 2, body, 0)

    @pl.when(qi % 2 == 1)
    def _():
        trip(qi - 1, 0)
        softmax(1, True)
        weighted_values(qi - 1, 0)
        weighted_values(qi, 1)

    @pl.when(qi % 2 == 0)
    def _():
        softmax(0, True)
        weighted_values(jnp.maximum(qi - 1, 0), 1)
        weighted_values(qi, 0)

    lam = lam_ref[...]
    l1 = jnp.sum(lam[0:1] * lam[1:2], axis=-1, keepdims=True)
    l2 = jnp.sum(lam[2:3] * lam[3:4], axis=-1, keepdims=True)
    lmbda = jnp.exp(l1) - jnp.exp(l2) + LAMBDA_INIT
    for st in range(NSTREAM):
        acc = acc_ref[st, :dv, :]
        l = acc_ref[st, dv:dv + 1, :]
        o = acc[:, :tq] / l[:, :tq] - lmbda * (acc[:, tq:] / l[:, tq:])
        ms = jnp.mean(o * o, axis=0, keepdims=True)
        y = o * lax.rsqrt(ms + NORM_EPS) * gsub_ref[...] * (1.0 - LAMBDA_INIT)
        o_ref[0, :, st * 128:(st + 1) * 128] = y.T.astype(BF16)


def _attention_specs(b, w, s, v_rows):
    gw = NSTREAM * 128
    in_specs = [pl.BlockSpec((1, gw, TB), lambda bi, gi, qi: (bi, gi, qi)),
                pl.BlockSpec((1, s, gw), lambda bi, gi, qi: (bi, 0, gi)),
                pl.BlockSpec((1, NSTREAM, s // TB, v_rows, TB), lambda bi, gi, qi: (bi, gi, 0, 0, 0))]
    out_spec = pl.BlockSpec((1, TB, gw), lambda bi, gi, qi: (bi, qi, gi))
    return (b, w // gw, s // TB), in_specs, out_spec


def _diff_attention(q_t, k, v_t, lam, g_sub):
    b, w, s = q_t.shape
    grid, in_specs, out_spec = _attention_specs(b, w, s, DV_ROWS)
    return pl.pallas_call(
        _diff_kernel,
        out_shape=jax.ShapeDtypeStruct((b, s, w), BF16),
        grid=grid,
        in_specs=in_specs + [pl.BlockSpec(lam.shape, lambda bi, gi, qi: (0, 0)),
                             pl.BlockSpec(g_sub.shape, lambda bi, gi, qi: (0, 0))],
        out_specs=out_spec,
        scratch_shapes=[pltpu.VMEM((NSTREAM, 1, 2 * TB), F32), pltpu.VMEM((NSTREAM, DV_ROWS, 2 * TB), F32),
                        pltpu.VMEM((NSTREAM, TB, 2 * TB), F32), pltpu.VMEM((NSTREAM, TB, 2 * TB), F32),
                        pltpu.VMEM((NSTREAM, TB, 2 * TB), BF16), pltpu.VMEM((NSTREAM, TB, 2 * TB), BF16),
                        pltpu.VMEM((NSTREAM, 1, 2 * TB), F32), pltpu.VMEM((NSTREAM, 1, 2 * TB), F32)],
        compiler_params=_cparams(("parallel", "parallel", "parallel")),
        name="diff_attention",
    )(q_t, k, v_t, lam, g_sub)


def _sb_kernel(qt_ref, k_ref, vt_ref, o_ref, c_ref, acc_ref):
    qi = pl.program_id(2)
    tq = qt_ref.shape[2]
    qcats = [_q_pair(qt_ref[0, st * 128:(st + 1) * 128, :]) for st in range(NSTREAM)]
    ur = lax.broadcasted_iota(I32, (TB, TB), 0)
    uc = lax.broadcasted_iota(I32, (TB, TB), 1)
    u_neg = jnp.where(uc > ur, -1.0, 0.0).astype(BF16)
    c_ref[...] = jnp.zeros(c_ref.shape, F32)
    acc_ref[...] = jnp.zeros(acc_ref.shape, F32)

    def step(j, diagonal):
        row0 = pl.multiple_of(j * TB, TB)
        if diagonal:
            r, c = _key_query_iotas(TB, tq)
            valid = r < c
        zs = [_dot(k_ref[0, pl.ds(row0, TB), st * 128:(st + 1) * 128], qcats[st])
              for st in range(NSTREAM)]
        pre = []
        for st in range(NSTREAM):
            z = zs[st]
            neg_abs = pltpu.bitcast(pltpu.bitcast(z, U32) | jnp.uint32(SIGN_BIT), F32)
            sp = jnp.maximum(z, 0.0) + jnp.log(1.0 + jnp.exp(neg_abs))
            drop = jnp.where(valid, sp, 0.0) if diagonal else sp
            pre.append((z - sp, drop.astype(BF16), drop[0:1, :]))
        laters = [_dot(u_neg, pre[st][1]) for st in range(NSTREAM)]
        for st in range(NSTREAM):
            c_run = c_ref[st]
            w = jnp.exp(pre[st][0] + laters[st] + c_run)
            if diagonal:
                w = jnp.where(valid, w, 0.0)
            wb = w.astype(BF16)
            vj = vt_ref[0, st, j]
            acc_ref[st] += jnp.concatenate([_dot(vj[:HEAD_DIM], wb[:, :tq]),
                                            _dot(vj[HEAD_DIM:], wb[:, tq:])], axis=0)
            c_ref[st] = c_run + laters[st][0:1, :] - pre[st][2]

    def live():
        return jnp.max(c_ref[...]) > SB_LOG_FLOOR

    step(qi, True)

    def body(i):
        step(qi - 1 - i, False)
        return i + 1

    lax.while_loop(lambda i: (i < qi) & live(), body, jnp.int32(0))
    for st in range(NSTREAM):
        o_ref[0, :, st * 128:(st + 1) * 128] = acc_ref[st].T.astype(BF16)


def _sb_attention(q_t, k, v_t):
    b, w, s = q_t.shape
    grid, in_specs, out_spec = _attention_specs(b, w, s, 2 * HEAD_DIM)
    return pl.pallas_call(
        _sb_kernel,
        out_shape=jax.ShapeDtypeStruct((b, s, w), BF16),
        grid=grid,
        in_specs=in_specs,
        out_specs=out_spec,
        scratch_shapes=[pltpu.VMEM((NSTREAM, 1, 2 * TB), F32), pltpu.VMEM((NSTREAM, 2 * HEAD_DIM, TB), F32)],
        compiler_params=_cparams(("parallel", "parallel", "parallel")),
        name="sb_attention",
    )(q_t, k, v_t)


def _first_index_of_max(v, idx, size):
    m = jnp.max(v, axis=0, keepdims=True)
    first = jnp.min(jnp.where(v == m, idx, size), axis=0, keepdims=True)
    return m, first


def _route_kernel(md_ref, ms_ref, x_ref, mod_ref, wo_ref, g_ref, wrh_ref, wrl_ref, eb_ref,
                  r1_ref, h2p_ref, idx_ref, wn_ref, rank_ref, cnt_ref, base_ref):
    i = pl.program_id(0)

    @pl.when(i == 0)
    def _():
        base_ref[...] = jnp.zeros_like(base_ref)

    half = md_ref.shape[1]
    attn = _dot(md_ref[...], wo_ref[:half, :]) + _dot(ms_ref[...], wo_ref[half:, :])
    r1 = x_ref[...] + mod_ref[0, 2:3, :] * attn
    r1_ref[...] = r1
    ms = jnp.mean(r1 * r1, axis=-1, keepdims=True)
    h2 = r1 * lax.rsqrt(ms + NORM_EPS) * g_ref[...] * (1.0 + mod_ref[0, 4:5, :]) + mod_ref[0, 3:4, :]
    h2p_ref[...] = _pack_rows(h2)

    hh, hl = _split_bf16(h2)
    wh = wrh_ref[...]
    logits = _dot_nt(wh, hh) + _dot_nt(wh, hl) + _dot_nt(wrl_ref[...], hh)
    scores = 1.0 / (1.0 + jnp.exp(-logits))
    biased = scores + eb_ref[...]
    tt = scores.shape[1]

    gi = lax.broadcasted_iota(I32, (GROUP_SIZE, tt), 0)
    gs = []
    for g in range(N_GROUPS):
        blk = biased[g * GROUP_SIZE:(g + 1) * GROUP_SIZE, :]
        m1, first = _first_index_of_max(blk, gi, GROUP_SIZE)
        m2 = jnp.max(jnp.where(gi == first, -jnp.inf, blk), axis=0, keepdims=True)
        gs.append(m1 + m2)
    gscore = jnp.concatenate(gs, axis=0)
    gidx = lax.broadcasted_iota(I32, (N_GROUPS, tt), 0)
    gsel = jnp.zeros((N_GROUPS, tt), F32)
    for _ in range(TOPK_GROUPS):
        _, first = _first_index_of_max(gscore, gidx, N_GROUPS)
        hit = gidx == first
        gsel = jnp.where(hit, 1.0, gsel)
        gscore = jnp.where(hit, -jnp.inf, gscore)
    cand = jnp.concatenate(
        [jnp.where(gsel[g:g + 1, :] > 0.0, biased[g * GROUP_SIZE:(g + 1) * GROUP_SIZE, :], -jnp.inf)
         for g in range(N_GROUPS)], axis=0)

    ei = lax.broadcasted_iota(I32, (N_EXPERTS, tt), 0)
    picked = jnp.zeros((N_EXPERTS, tt), F32)
    idxs, ws, hits = [], [], []
    for _ in range(TOP_K):
        _, first = _first_index_of_max(cand, ei, N_EXPERTS)
        hit = ei == first
        idxs.append(first)
        ws.append(jnp.sum(jnp.where(hit, scores, 0.0), axis=0, keepdims=True))
        hits.append(hit)
        picked = jnp.where(hit, 1.0, picked)
        cand = jnp.where(hit, -jnp.inf, cand)
    idx_ref[...] = jnp.concatenate(idxs, axis=0)
    w_t = jnp.concatenate(ws, axis=0)
    w_t = w_t / jnp.sum(w_t, axis=0, keepdims=True) * ROUTED_SCALE
    pad = jnp.zeros((128 - TOP_K, tt), F32)
    wn_ref[...] = jnp.concatenate([w_t, pad], axis=0).T[:, :TOP_K]

    tr = lax.broadcasted_iota(I32, (tt, tt), 0)
    tc = lax.broadcasted_iota(I32, (tt, tt), 1)
    before = jnp.where(tr < tc, 1.0, 0.0).astype(BF16)
    seen = _dot(picked.astype(BF16), before) + base_ref[...]
    rank_ref[...] = jnp.concatenate(
        [jnp.sum(jnp.where(h, seen, 0.0), axis=0, keepdims=True) for h in hits], axis=0).astype(I32)
    base_ref[...] += jnp.sum(picked, axis=1, keepdims=True)
    cnt_ref[...] = base_ref[...].astype(I32)


def _outproj_route(mixed_d, mixed_s, x2, mod, w_out, g_ffn, wr_hi, wr_lo, e_bias, s):
    t, d = x2.shape
    half = mixed_d.shape[1]
    per_b = s // TT
    const = lambda shape: pl.BlockSpec(shape, lambda i: (0,) * len(shape))
    tok = lambda wd: pl.BlockSpec((TT, wd), lambda i: (i, 0))
    lane = pl.BlockSpec((TOP_K, TT), lambda i: (0, i))
    return pl.pallas_call(
        _route_kernel,
        out_shape=(jax.ShapeDtypeStruct((t, d), F32),
                   jax.ShapeDtypeStruct((t, d // 2), U32),
                   jax.ShapeDtypeStruct((TOP_K, t), I32),
                   jax.ShapeDtypeStruct((t, TOP_K), F32),
                   jax.ShapeDtypeStruct((TOP_K, t), I32),
                   jax.ShapeDtypeStruct((N_EXPERTS, 1), I32)),
        grid=(t // TT,),
        in_specs=[tok(half), tok(half), tok(d),
                  pl.BlockSpec((1, N_MOD, d), lambda i: (i // per_b, 0, 0)),
                  const(w_out.shape), const((1, d)), const(wr_hi.shape), const(wr_lo.shape),
                  const((N_EXPERTS, 1))],
        out_specs=(tok(d), tok(d // 2), lane, tok(TOP_K), lane, const((N_EXPERTS, 1))),
        scratch_shapes=[pltpu.VMEM((N_EXPERTS, 1), F32)],
        compiler_params=_cparams(("arbitrary",)),
        name="outproj_route",
    )(mixed_d, mixed_s, x2, mod, w_out, g_ffn, wr_hi, wr_lo, e_bias)


def _pos_kernel(idx_ref, rank_ref, pstart_ref, pos_ref):
    tt = idx_ref.shape[1]
    ei = lax.broadcasted_iota(I32, (N_EXPERTS, tt), 0)
    ps = pstart_ref[...]
    rows = []
    for k in range(TOP_K):
        hit = ei == idx_ref[k:k + 1, :]
        rows.append(jnp.sum(jnp.where(hit, ps, 0), axis=0, keepdims=True))
    pos_ref[...] = jnp.concatenate(rows, axis=0) + rank_ref[...]


def _slot_positions(idx_t, rank_t, pstart):
    k, t = idx_t.shape
    tp = 1024
    lane = pl.BlockSpec((k, tp), lambda i: (0, i))
    return pl.pallas_call(
        _pos_kernel,
        out_shape=jax.ShapeDtypeStruct((k, t), I32),
        grid=(t // tp,),
        in_specs=[lane, lane, pl.BlockSpec((N_EXPERTS, 1), lambda i: (0, 0))],
        out_specs=lane,
        compiler_params=_cparams(("parallel",)),
        name="slot_positions",
    )(idx_t, rank_t, pstart)


def _expert_kernel(be_ref, nv_ref, nu_ref, x_ref, wgu_ref, wd_ref, y_ref, wgu_b, wd_b):
    b = pl.program_id(0)
    nv = nv_ref[b]

    @pl.when(nv > 0)
    def _():
        prev = be_ref[jnp.maximum(b - 1, 0)]

        @pl.when((b == 0) | (be_ref[b] != prev))
        def _():
            wgu_b[...] = wgu_ref[0].astype(BF16)
            wd_b[...] = wd_ref[0].astype(BF16)

    for v in range(EXPERT_SUBS + 1):
        @pl.when(nv == v)
        def _(v=v):
            rows = v * EXPERT_ROWS
            if rows:
                x = _unpack_rows(x_ref[:rows, :]).astype(BF16)
                gu = _dot(x, wgu_b[...])
                ff = gu.shape[1] // 2
                act = _silu(gu[:, :ff]) * gu[:, ff:]
                y = _dot(act.astype(BF16), wd_b[...])
                y_ref[:rows, :] = _pack_rows(y)
            if rows < STEP_ROWS:
                y_ref[rows:, :] = jnp.zeros((STEP_ROWS - rows, y_ref.shape[1]), U32)


def _experts(block_e, n_valid, n_used, xs, w_gu, w_down):
    p, dp = xs.shape
    n_steps = p // STEP_ROWS
    _, d, f2 = w_gu.shape
    _, f, _ = w_down.shape
    x_spec = pl.BlockSpec((STEP_ROWS, dp), lambda b, be, nv, nu: (jnp.minimum(b, nu[0] - 1), 0))
    y_spec = pl.BlockSpec((STEP_ROWS, dp), lambda b, be, nv, nu: (jnp.where(b < nu[0], b, n_steps), 0))
    grid_spec = pltpu.PrefetchScalarGridSpec(
        num_scalar_prefetch=3,
        grid=(n_steps,),
        in_specs=[x_spec,
                  pl.BlockSpec((1, d, f2), lambda b, be, nv, nu: (be[b], 0, 0)),
                  pl.BlockSpec((1, f, d), lambda b, be, nv, nu: (be[b], 0, 0))],
        out_specs=y_spec,
        scratch_shapes=[pltpu.VMEM((d, f2), BF16), pltpu.VMEM((f, d), BF16)],
    )
    return pl.pallas_call(
        _expert_kernel,
        out_shape=jax.ShapeDtypeStruct((p + STEP_ROWS, dp), U32),
        grid_spec=grid_spec,
        compiler_params=_cparams(("arbitrary",)),
        name="routed_experts",
    )(block_e, n_valid, n_used, xs, w_gu, w_down)


def _shared_kernel(h2p_ref, wsg_ref, wsd_ref, o_ref):
    h2 = _unpack_rows(h2p_ref[...]).astype(BF16)
    gu = _dot(h2, wsg_ref[...])
    ff = gu.shape[1] // 2
    o_ref[...] = _dot((_silu(gu[:, :ff]) * gu[:, ff:]).astype(BF16), wsd_ref[...]).astype(BF16)


def _shared(h2p, w_sh_gu, w_sh_down):
    t, dp = h2p.shape
    d = w_sh_down.shape[1]
    const = lambda shape: pl.BlockSpec(shape, lambda i: (0,) * len(shape))
    tok = lambda wd: pl.BlockSpec((TT_SHARED, wd), lambda i: (i, 0))
    return pl.pallas_call(
        _shared_kernel,
        out_shape=jax.ShapeDtypeStruct((t, d), BF16),
        grid=(t // TT_SHARED,),
        in_specs=[tok(dp), const(w_sh_gu.shape), const(w_sh_down.shape)],
        out_specs=tok(d),
        compiler_params=_cparams(("parallel",)),
        name="shared_expert",
    )(h2p, w_sh_gu, w_sh_down)


def _combine_kernel(r1_ref, ysh_ref, yg_ref, wn_ref, mod_ref, o_ref):
    wn = wn_ref[...]
    y = ysh_ref[...].astype(F32)
    for k in range(TOP_K):
        y = y + wn[:, k:k + 1] * _unpack_rows(yg_ref[k])
    o_ref[...] = r1_ref[...] + mod_ref[0, 5:6, :] * y


def _combine(r1, y_sh, yg, w_nat, mod, s):
    t, d = r1.shape
    per_b = s // TT
    tok = lambda wd: pl.BlockSpec((TT, wd), lambda i: (i, 0))
    return pl.pallas_call(
        _combine_kernel,
        out_shape=jax.ShapeDtypeStruct((t, d), F32),
        grid=(t // TT,),
        in_specs=[tok(d), tok(d),
                  pl.BlockSpec((TOP_K, TT, d // 2), lambda i: (0, i, 0)),
                  tok(TOP_K),
                  pl.BlockSpec((1, N_MOD, d), lambda i: (i // per_b, 0, 0))],
        out_specs=tok(d),
        compiler_params=_cparams(("parallel",)),
        name="routed_combine",
    )(r1, y_sh, yg, w_nat, mod)


def _sc_mesh():
    return plsc.VectorSubcoreMesh(core_axis_name="c", subcore_axis_name="s")


def _half_row_index(pos):
    return jnp.stack([2 * pos, 2 * pos + 1], axis=-1).reshape(pos.shape[:-1] + (2 * pos.shape[-1],))


def _sc_dispatch(x, pos_t, p):
    t, dc = x.shape
    k = pos_t.shape[0]
    hw = dc // 2

    @functools.partial(pl.kernel, out_type=jax.ShapeDtypeStruct((2 * p, hw), x.dtype), mesh=_sc_mesh())
    def kern(x_hbm, pos_hbm, o_hbm):
        def body(x_vmem, i_vmem):
            pltpu.sync_copy(x_vmem, o_hbm.at[i_vmem.at[0]])

        pltpu.emit_pipeline(
            body,
            grid=(2 * t // SC_WINDOW, k),
            in_specs=[pl.BlockSpec((SC_WINDOW, hw), lambda i, kk: (i, 0)),
                      pl.BlockSpec((1, SC_WINDOW), lambda i, kk: (kk, i))],
            out_specs=[],
            core_axis_name=("c", "s"),
            dimension_semantics=(pltpu.PARALLEL, pltpu.ARBITRARY),
        )(x_hbm, pos_hbm)

    return kern(x.reshape(2 * t, hw), _half_row_index(pos_t)).reshape(p, dc)


def _sc_gather(y, pos_row):
    n = pos_row.shape[1]
    p, dc = y.shape
    hw = dc // 2

    @functools.partial(pl.kernel, out_type=jax.ShapeDtypeStruct((2 * n, hw), y.dtype), mesh=_sc_mesh())
    def kern(y_hbm, pos_hbm, o_hbm):
        def body(i_vmem, o_vmem):
            pltpu.sync_copy(y_hbm.at[i_vmem.at[0]], o_vmem)

        pltpu.emit_pipeline(
            body,
            grid=(2 * n // SC_WINDOW,),
            in_specs=[pl.BlockSpec((1, SC_WINDOW), lambda i: (0, i))],
            out_specs=[pl.BlockSpec((SC_WINDOW, hw), lambda i: (i, 0))],
            core_axis_name=("c", "s"),
            dimension_semantics=(pltpu.PARALLEL,),
        )(pos_hbm, o_hbm)

    return kern(y.reshape(2 * p, hw), _half_row_index(pos_row)).reshape(n, dc)


def kernel(x, c, positions, w_ada, b_ada, g_mix, w_in, g_q, g_k, lam, g_sub, w_out, g_ffn, w_router,
           e_bias, w_sh_gu, w_sh_down, w_gu, w_down):
    b, s, d = x.shape
    assert w_ada.shape[0] == 1 and s % TS == 0 and d % 256 == 0 and (b * s) % TT_SHARED == 0
    t = b * s
    wdt = w_in.shape[2] // 6

    c_pad = jnp.zeros((8, d), F32).at[:b].set(c.astype(F32))
    mod = _ada(c_pad, w_ada[0], b_ada)[:b].reshape(b, N_MOD, d)

    sec = [w_in[0][:, i * wdt:(i + 1) * wdt] for i in range(6)]
    w_t = jnp.concatenate([sec[0], sec[1], sec[2], sec[3], sec[5]], axis=1).T.astype(BF16)
    w_sk = sec[4].astype(BF16)
    inv = (ROPE_THETA ** (-jnp.arange(0, HEAD_DIM, 2, dtype=F32) / HEAD_DIM)).reshape(HALF, 1)
    pos3 = positions.astype(I32).reshape(b, 1, s)

    qd, kd, vd, qs, ks, vs = _inproj(x, mod, g_mix, pos3, inv, g_q.reshape(HEAD_DIM, 1),
                                     g_k.reshape(HEAD_DIM, 1), w_t, w_sk)
    mixed_d = _diff_attention(qd, kd, vd, lam[0], g_sub.reshape(2 * HEAD_DIM, 1))
    mixed_s = _sb_attention(qs, ks, vs)

    wr_t = w_router[0].T
    wr_hi = wr_t.astype(BF16)
    wr_lo = (wr_t - wr_hi.astype(F32)).astype(BF16)
    r1, h2p, idx_t, w_nat, rank_t, counts = _outproj_route(
        mixed_d.reshape(t, wdt), mixed_s.reshape(t, wdt), x.reshape(t, d), mod,
        w_out[0].astype(BF16), g_ffn, wr_hi, wr_lo, e_bias.reshape(N_EXPERTS, 1), s)

    counts = counts.reshape(N_EXPERTS)
    padded = (counts + STEP_ROWS - 1) // STEP_ROWS * STEP_ROWS
    pend = jnp.cumsum(padded)
    pstart = (pend - padded).astype(I32)
    n_steps = t * TOP_K // STEP_ROWS + N_EXPERTS
    step_row0 = jnp.arange(n_steps, dtype=pend.dtype) * STEP_ROWS
    block_e = jnp.minimum(jnp.sum(pend[None, :] <= step_row0[:, None], axis=1), N_EXPERTS - 1).astype(I32)
    rows_left = counts[block_e] - (step_row0 - pstart[block_e])
    n_valid = jnp.where(step_row0 < pend[-1],
                        jnp.clip((rows_left + EXPERT_ROWS - 1) // EXPERT_ROWS, 0, EXPERT_SUBS), 0).astype(I32)
    n_used = (pend[-1:] // STEP_ROWS).astype(I32)
    pos_t = _slot_positions(idx_t, rank_t, pstart.reshape(N_EXPERTS, 1))

    xs = _sc_dispatch(h2p, pos_t, n_steps * STEP_ROWS)
    y = _experts(block_e, n_valid, n_used, xs, w_gu[0], w_down[0])
    y_sh = _shared(h2p, w_sh_gu[0].astype(BF16), w_sh_down[0].astype(BF16))
    yg = _sc_gather(y, pos_t.reshape(1, TOP_K * t))
    out = _combine(r1, y_sh, yg.reshape(TOP_K, t, d // 2), w_nat, mod, s)
    return out.reshape(b, s, d).astype(x.dtype)
```

```python
import functools
import math

import jax
import jax.numpy as jnp
from jax import lax
from jax.experimental import pallas as pl
from jax.experimental.pallas import tpu as pltpu
from jax.experimental.pallas import tpu_sc as plsc

F32 = jnp.float32
BF16 = jnp.bfloat16
U32 = jnp.uint32
I32 = jnp.int32

HEAD_DIM = 64
HALF = HEAD_DIM // 2
ROPE_THETA = 10000.0
N_EXPERTS = 256
TOP_K = 8
N_GROUPS = 8
GROUP_SIZE = N_EXPERTS // N_GROUPS
TOPK_GROUPS = 4
EXPERT_FF = 256
ROUTED_SCALE = 2.5
EXPERT_ROWS = 128
EXPERT_SUBS = 5
STEP_ROWS = EXPERT_SUBS * EXPERT_ROWS
NORM_EPS = 1e-6
SB_LOG_FLOOR = -110.0
NSTREAM = 4
N_MOD = 6
LAMBDA_INIT = 0.8 - 0.6 * math.exp(-0.3 * 0)
Q_SCALE = HEAD_DIM ** -0.5
LOG2E = math.log2(math.e)
SUM_ROWS = 16
DV_ROWS = 2 * HEAD_DIM + SUM_ROWS

TS = 512
TB = 256
TT = 256
TT_SHARED = 1024
HI_MASK = 0xFFFF0000
SIGN_BIT = 0x80000000
SC_WINDOW = 128

VMEM_LIMIT = 48 * 1024 * 1024


def _cparams(sem):
    return pltpu.CompilerParams(dimension_semantics=sem, vmem_limit_bytes=VMEM_LIMIT)


def _split_bf16(a):
    hi = a.astype(BF16)
    lo = (a - hi.astype(F32)).astype(BF16)
    return hi, lo


def _dot(a, b):
    return jnp.dot(a, b, preferred_element_type=F32)


def _dot_nt(a, b):
    return lax.dot_general(a, b, (((1,), (1,)), ((), ())), preferred_element_type=F32)


def _silu(x):
    return x / (1.0 + jnp.exp(-x))


def _pack_bf16_pair(a, b):
    ab = pltpu.bitcast(a.astype(BF16).astype(F32), U32)
    bb = pltpu.bitcast(b.astype(BF16).astype(F32), U32)
    return (ab & jnp.uint32(HI_MASK)) | (bb >> 16)


def _unpack_bf16_pair(p):
    a = pltpu.bitcast(p & jnp.uint32(HI_MASK), F32)
    b = pltpu.bitcast(p << 16, F32)
    return a, b


def _pack_rows(v):
    q = v.shape[1] // 4
    return (_pack_bf16_pair(v[:, :q], v[:, q:2 * q]),
            _pack_bf16_pair(v[:, 2 * q:3 * q], v[:, 3 * q:]))


def _unpack_rows(pa, pb):
    return jnp.concatenate(_unpack_bf16_pair(pa) + _unpack_bf16_pair(pb), axis=1)


def _ada_kernel(c_ref, w_ref, b_ref, o_ref):
    cond = _silu(c_ref[...])
    ch, cl = _split_bf16(cond)
    wh, wl = _split_bf16(w_ref[...])
    o_ref[...] = _dot(ch, wh) + _dot(ch, wl) + _dot(cl, wh) + b_ref[...]


def _ada(c_pad, w_ada, b_ada):
    d, n = w_ada.shape
    tn = n // 4
    return pl.pallas_call(
        _ada_kernel,
        out_shape=jax.ShapeDtypeStruct((c_pad.shape[0], n), F32),
        grid=(n // tn,),
        in_specs=[pl.BlockSpec((c_pad.shape[0], d), lambda j: (0, 0)),
                  pl.BlockSpec((d, tn), lambda j: (0, j)),
                  pl.BlockSpec((1, tn), lambda j: (0, j))],
        out_specs=pl.BlockSpec((c_pad.shape[0], tn), lambda j: (0, j)),
        compiler_params=_cparams(("arbitrary",)),
        name="ada_mod",
    )(c_pad, w_ada, b_ada)


def _norm_rope_t(t, g_col, cos_t, sin_t):
    outs = []
    for i in range(t.shape[0] // HEAD_DIM):
        blk = t[i * HEAD_DIM:(i + 1) * HEAD_DIM, :]
        ms = jnp.mean(blk * blk, axis=0, keepdims=True)
        y = blk * lax.rsqrt(ms + NORM_EPS) * g_col
        y1 = y[:HALF]
        y2 = y[HALF:]
        outs.append(y1 * cos_t - y2 * sin_t)
        outs.append(y2 * cos_t + y1 * sin_t)
    return jnp.concatenate(outs, axis=0)


def _inproj_kernel(x_ref, mod_ref, g_ref, pos_ref, inv_ref, gq_ref, gk_ref, wt_ref, wsk_ref,
                   qd_ref, kd_ref, vd_ref, qs_ref, ks_ref, vs_ref):
    x = x_ref[0]
    ms = jnp.mean(x * x, axis=-1, keepdims=True)
    sh = mod_ref[0, 0:1, :]
    sc = mod_ref[0, 1:2, :]
    h = x * lax.rsqrt(ms + NORM_EPS) * g_ref[...] * (1.0 + sc) + sh
    hb = h.astype(BF16)
    w = qd_ref.shape[1]

    def proj_t(sec):
        return _dot_nt(wt_ref[sec * w:(sec + 1) * w, :], hb)

    ang = inv_ref[...] * pos_ref[0].astype(F32)
    cos_t = jnp.cos(ang)
    sin_t = jnp.sin(ang)
    q_t = _norm_rope_t(proj_t(0), gq_ref[...], cos_t, sin_t) * (Q_SCALE * LOG2E)
    qd_ref[0] = q_t.astype(BF16)
    k_t = _norm_rope_t(proj_t(1), gk_ref[...], cos_t, sin_t)
    kd_ref[0] = k_t.T.astype(BF16)
    v_t = proj_t(2).astype(BF16)
    s_t = proj_t(4).astype(BF16)
    nh = vd_ref.shape[1]
    hw = vs_ref.shape[3]
    ones = jnp.ones((SUM_ROWS, TB), BF16)
    for hh in range(nh):
        for jj in range(vd_ref.shape[2]):
            vd_ref[0, hh, jj] = jnp.concatenate(
                [v_t[hh * hw:(hh + 1) * hw, jj * TB:(jj + 1) * TB], ones], axis=0)
            vs_ref[0, hh, jj] = s_t[hh * hw:(hh + 1) * hw, jj * TB:(jj + 1) * TB]
    qs_ref[0] = (proj_t(3) * Q_SCALE).astype(BF16)
    ks_ref[0] = _dot(hb, wsk_ref[...]).astype(BF16)


def _inproj(x, mod, g_mix, pos3, inv, g_q, g_k, w_t, w_sk):
    b, s, d = x.shape
    w = w_sk.shape[1]
    nh = w // 128
    nsb = TS // TB
    const = lambda shape: pl.BlockSpec(shape, lambda bi, si: (0,) * len(shape))
    t_spec = pl.BlockSpec((1, w, TS), lambda bi, si: (bi, 0, si))
    n_spec = pl.BlockSpec((1, TS, w), lambda bi, si: (bi, si, 0))
    v_spec = lambda rows: pl.BlockSpec((1, nh, nsb, rows, TB), lambda bi, si: (bi, 0, si, 0, 0))
    t_shape = jax.ShapeDtypeStruct((b, w, s), BF16)
    n_shape = jax.ShapeDtypeStruct((b, s, w), BF16)
    v_shape = lambda rows: jax.ShapeDtypeStruct((b, nh, s // TB, rows, TB), BF16)
    return pl.pallas_call(
        _inproj_kernel,
        out_shape=(t_shape, n_shape, v_shape(DV_ROWS), t_shape, n_shape, v_shape(128)),
        grid=(b, s // TS),
        in_specs=[pl.BlockSpec((1, TS, d), lambda bi, si: (bi, si, 0)),
                  pl.BlockSpec((1, N_MOD, d), lambda bi, si: (bi, 0, 0)),
                  const((1, d)),
                  pl.BlockSpec((1, 1, TS), lambda bi, si: (bi, 0, si)),
                  const((HALF, 1)), const((HEAD_DIM, 1)), const((HEAD_DIM, 1)),
                  const(w_t.shape), const(w_sk.shape)],
        out_specs=(t_spec, n_spec, v_spec(DV_ROWS), t_spec, n_spec, v_spec(128)),
        compiler_params=_cparams(("parallel", "parallel")),
        name="norm_inproj",
    )(x, mod, g_mix, pos3, inv, g_q, g_k, w_t, w_sk)


def _q_pair(q_t):
    row = lax.broadcasted_iota(I32, q_t.shape, 0)
    zero = jnp.zeros_like(q_t)
    return jnp.concatenate([jnp.where(row < HEAD_DIM, q_t, zero),
                            jnp.where(row >= HEAD_DIM, q_t, zero)], axis=1)


def _key_query_iotas(tk, tq):
    r = lax.broadcasted_iota(I32, (tk, 2 * tq), 0)
    c = lax.broadcasted_iota(I32, (tk, 2 * tq), 1)
    return r, jnp.where(c >= tq, c - tq, c)


def _linear_step():
    n = pl.program_id(0)
    for ax in (1, 2):
        n = n * pl.num_programs(ax) + pl.program_id(ax)
    last = pl.num_programs(0) * pl.num_programs(1) * pl.num_programs(2) - 1
    return n, last


def _cast_copies(n, src_hbm, dst_hbm, stage_in, stage_out, sem_in, sem_out):
    per = stage_in.shape[0]
    e0 = n * per
    return (pltpu.make_async_copy(src_hbm.at[pl.ds(e0, per)], stage_in, sem_in),
            pltpu.make_async_copy(stage_out, dst_hbm.at[pl.ds(e0, per)], sem_out))


def _cast_begin(n, src_hbm, dst_hbm, stage_in, stage_out, sem_in, sem_out):
    _cast_copies(n, src_hbm, dst_hbm, stage_in, stage_out, sem_in, sem_out)[0].start()


def _cast_end(n, last, src_hbm, dst_hbm, stage_in, stage_out, sem_in, sem_out):
    load, store = _cast_copies(n, src_hbm, dst_hbm, stage_in, stage_out, sem_in, sem_out)
    load.wait()

    @pl.when(n > 0)
    def _():
        _cast_copies(n - 1, src_hbm, dst_hbm, stage_in, stage_out, sem_in, sem_out)[1].wait()

    for i in range(stage_in.shape[0]):
        stage_out[i] = stage_in[i].astype(BF16)
    store.start()

    @pl.when(n == last)
    def _():
        store.wait()


def _cast_plumbing(w_src, n_steps):
    e = w_src.shape[0]
    assert e % n_steps == 0
    per = e // n_steps
    any_spec = pl.BlockSpec(memory_space=pl.ANY)
    out_shape = jax.ShapeDtypeStruct(w_src.shape, BF16)
    scratch = [pltpu.VMEM((per,) + w_src.shape[1:], F32), pltpu.VMEM((per,) + w_src.shape[1:], BF16),
               pltpu.SemaphoreType.DMA, pltpu.SemaphoreType.DMA]
    return any_spec, out_shape, scratch


def _diff_kernel(qt_ref, k_ref, vt_ref, lam_ref, gsub_ref, wsrc_ref, o_ref, wdst_ref,
                 m_ref, acc_ref, s0, s1, p0, p1, a0, a1, *cast_scratch):
    qi = pl.program_id(2)
    tq = qt_ref.shape[2]
    dv = 2 * HEAD_DIM
    n_step, last_step = _linear_step()
    _cast_begin(n_step, wsrc_ref, wdst_ref, *cast_scratch)
    qcats = [_q_pair(qt_ref[0, st * 128:(st + 1) * 128, :]) for st in range(NSTREAM)]
    m_ref[...] = jnp.full(m_ref.shape, -jnp.inf, F32)
    acc_ref[...] = jnp.zeros(acc_ref.shape, F32)
    s_slots, p_slots, a_slots = (s0, s1), (p0, p1), (a0, a1)
    p_slots[1][...] = jnp.zeros(p_slots[1].shape, BF16)
    a_slots[1][...] = jnp.ones(a_slots[1].shape, F32)

    def scores(j, slot):
        row0 = pl.multiple_of(j * TB, TB)
        for st in range(NSTREAM):
            s_slots[slot][st] = _dot(k_ref[0, pl.ds(row0, TB), st * 128:(st + 1) * 128], qcats[st])

    def softmax(slot, diagonal):
        for st in range(NSTREAM):
            s = s_slots[slot][st]
            if diagonal:
                r, c = _key_query_iotas(TB, tq)
                s = jnp.where(r <= c, s, -jnp.inf)
            m = m_ref[st]
            m_new = jnp.maximum(m, jnp.max(s, axis=0, keepdims=True))
            a_slots[slot][st] = jnp.exp2(m - m_new)
            p_slots[slot][st] = jnp.exp2(s - m_new).astype(BF16)
            m_ref[st] = m_new

    def weighted_values(j, slot):
        for st in range(NSTREAM):
            acc_ref[st] = (acc_ref[st] * a_slots[slot][st]
                           + _dot(vt_ref[0, st, j], p_slots[slot][st]))

    def trip(j, slot):
        scores(j + 1, 1 - slot)
        softmax(slot, False)
        weighted_values(jnp.maximum(j - 1, 0), 1 - slot)

    def body(i, carry):
        trip(2 * i, 0)
        trip(2 * i + 1, 1)
        return carry

    scores(0, 0)
    lax.fori_loop(0, qi // 2, body, 0)

    @pl.when(qi % 2 == 1)
    def _():
        trip(qi - 1, 0)
        softmax(1, True)
        weighted_values(qi - 1, 0)
        weighted_values(qi, 1)

    @pl.when(qi % 2 == 0)
    def _():
        softmax(0, True)
        weighted_values(jnp.maximum(qi - 1, 0), 1)
        weighted_values(qi, 0)

    lam = lam_ref[...]
    l1 = jnp.sum(lam[0:1] * lam[1:2], axis=-1, keepdims=True)
    l2 = jnp.sum(lam[2:3] * lam[3:4], axis=-1, keepdims=True)
    lmbda = jnp.exp(l1) - jnp.exp(l2) + LAMBDA_INIT
    for st in range(NSTREAM):
        acc = acc_ref[st, :dv, :]
        l = acc_ref[st, dv:dv + 1, :]
        o = acc[:, :tq] / l[:, :tq] - lmbda * (acc[:, tq:] / l[:, tq:])
        ms = jnp.mean(o * o, axis=0, keepdims=True)
        y = o * lax.rsqrt(ms + NORM_EPS) * gsub_ref[...] * (1.0 - LAMBDA_INIT)
        o_ref[0, :, st * 128:(st + 1) * 128] = y.T.astype(BF16)
    _cast_end(n_step, last_step, wsrc_ref, wdst_ref, *cast_scratch)


def _attention_specs(b, w, s, v_rows):
    gw = NSTREAM * 128
    in_specs = [pl.BlockSpec((1, gw, TB), lambda bi, gi, qi: (bi, gi, qi)),
                pl.BlockSpec((1, s, gw), lambda bi, gi, qi: (bi, 0, gi)),
                pl.BlockSpec((1, NSTREAM, s // TB, v_rows, TB), lambda bi, gi, qi: (bi, gi, 0, 0, 0))]
    out_spec = pl.BlockSpec((1, TB, gw), lambda bi, gi, qi: (bi, qi, gi))
    return (b, w // gw, s // TB), in_specs, out_spec


def _diff_attention(q_t, k, v_t, lam, g_sub, w_cast):
    b, w, s = q_t.shape
    grid, in_specs, out_spec = _attention_specs(b, w, s, DV_ROWS)
    any_spec, cast_shape, cast_scratch = _cast_plumbing(w_cast, math.prod(grid))
    return pl.pallas_call(
        _diff_kernel,
        out_shape=(jax.ShapeDtypeStruct((b, s, w), BF16), cast_shape),
        grid=grid,
        in_specs=in_specs + [pl.BlockSpec(lam.shape, lambda bi, gi, qi: (0, 0)),
                             pl.BlockSpec(g_sub.shape, lambda bi, gi, qi: (0, 0)), any_spec],
        out_specs=(out_spec, any_spec),
        scratch_shapes=[pltpu.VMEM((NSTREAM, 1, 2 * TB), F32), pltpu.VMEM((NSTREAM, DV_ROWS, 2 * TB), F32),
                        pltpu.VMEM((NSTREAM, TB, 2 * TB), F32), pltpu.VMEM((NSTREAM, TB, 2 * TB), F32),
                        pltpu.VMEM((NSTREAM, TB, 2 * TB), BF16), pltpu.VMEM((NSTREAM, TB, 2 * TB), BF16),
                        pltpu.VMEM((NSTREAM, 1, 2 * TB), F32), pltpu.VMEM((NSTREAM, 1, 2 * TB), F32)]
        + cast_scratch,
        compiler_params=_cparams(("arbitrary", "arbitrary", "arbitrary")),
        name="diff_attention",
    )(q_t, k, v_t, lam, g_sub, w_cast)


def _sb_kernel(qt_ref, k_ref, vt_ref, wsrc_ref, o_ref, wdst_ref, c_ref, acc_ref, *cast_scratch):
    qi = pl.program_id(2)
    tq = qt_ref.shape[2]
    qcats = [_q_pair(qt_ref[0, st * 128:(st + 1) * 128, :]) for st in range(NSTREAM)]
    ur = lax.broadcasted_iota(I32, (TB, TB), 0)
    uc = lax.broadcasted_iota(I32, (TB, TB), 1)
    u_neg = jnp.where(uc > ur, -1.0, 0.0).astype(BF16)
    c_ref[...] = jnp.zeros(c_ref.shape, F32)
    acc_ref[...] = jnp.zeros(acc_ref.shape, F32)
    n_step, last_step = _linear_step()
    _cast_begin(n_step, wsrc_ref, wdst_ref, *cast_scratch)

    def step(j, diagonal):
        row0 = pl.multiple_of(j * TB, TB)
        if diagonal:
            r, c = _key_query_iotas(TB, tq)
            valid = r < c
        zs = [_dot(k_ref[0, pl.ds(row0, TB), st * 128:(st + 1) * 128], qcats[st])
              for st in range(NSTREAM)]
        pre = []
        for st in range(NSTREAM):
            z = zs[st]
            neg_abs = pltpu.bitcast(pltpu.bitcast(z, U32) | jnp.uint32(SIGN_BIT), F32)
            sp = jnp.maximum(z, 0.0) + jnp.log(1.0 + jnp.exp(neg_abs))
            drop = jnp.where(valid, sp, 0.0) if diagonal else sp
            pre.append((z - sp, drop.astype(BF16), drop[0:1, :]))
        laters = [_dot(u_neg, pre[st][1]) for st in range(NSTREAM)]
        for st in range(NSTREAM):
            c_run = c_ref[st]
            w = jnp.exp(pre[st][0] + laters[st] + c_run)
            if diagonal:
                w = jnp.where(valid, w, 0.0)
            wb = w.astype(BF16)
            vj = vt_ref[0, st, j]
            acc_ref[st] += jnp.concatenate([_dot(vj[:HEAD_DIM], wb[:, :tq]),
                                            _dot(vj[HEAD_DIM:], wb[:, tq:])], axis=0)
            c_ref[st] = c_run + laters[st][0:1, :] - pre[st][2]

    def live():
        return jnp.max(c_ref[...]) > SB_LOG_FLOOR

    step(qi, True)

    def body(i):
        step(qi - 1 - i, False)
        return i + 1

    lax.while_loop(lambda i: (i < qi) & live(), body, jnp.int32(0))
    for st in range(NSTREAM):
        o_ref[0, :, st * 128:(st + 1) * 128] = acc_ref[st].T.astype(BF16)
    _cast_end(n_step, last_step, wsrc_ref, wdst_ref, *cast_scratch)


def _sb_attention(q_t, k, v_t, w_cast):
    b, w, s = q_t.shape
    grid, in_specs, out_spec = _attention_specs(b, w, s, 2 * HEAD_DIM)
    any_spec, cast_shape, cast_scratch = _cast_plumbing(w_cast, math.prod(grid))
    return pl.pallas_call(
        _sb_kernel,
        out_shape=(jax.ShapeDtypeStruct((b, s, w), BF16), cast_shape),
        grid=grid,
        in_specs=in_specs + [any_spec],
        out_specs=(out_spec, any_spec),
        scratch_shapes=[pltpu.VMEM((NSTREAM, 1, 2 * TB), F32), pltpu.VMEM((NSTREAM, 2 * HEAD_DIM, TB), F32)]
        + cast_scratch,
        compiler_params=_cparams(("arbitrary", "arbitrary", "arbitrary")),
        name="sb_attention",
    )(q_t, k, v_t, w_cast)


def _first_index_of_max(v, idx, size):
    m = jnp.max(v, axis=0, keepdims=True)
    first = jnp.min(jnp.where(v == m, idx, size), axis=0, keepdims=True)
    return m, first


def _route_kernel(md_ref, ms_ref, x_ref, mod_ref, wo_ref, g_ref, wrh_ref, wrl_ref, eb_ref,
                  r1_ref, h2a_ref, h2b_ref, idx_ref, wn_ref, rank_ref, cnt_ref, base_ref):
    i = pl.program_id(0)

    @pl.when(i == 0)
    def _():
        base_ref[...] = jnp.zeros_like(base_ref)

    half = md_ref.shape[1]
    attn = _dot(md_ref[...], wo_ref[:half, :]) + _dot(ms_ref[...], wo_ref[half:, :])
    r1 = x_ref[...] + mod_ref[0, 2:3, :] * attn
    r1_ref[...] = r1
    ms = jnp.mean(r1 * r1, axis=-1, keepdims=True)
    h2 = r1 * lax.rsqrt(ms + NORM_EPS) * g_ref[...] * (1.0 + mod_ref[0, 4:5, :]) + mod_ref[0, 3:4, :]
    h2a_ref[...], h2b_ref[...] = _pack_rows(h2)

    hh, hl = _split_bf16(h2)
    wh = wrh_ref[...]
    logits = _dot_nt(wh, hh) + _dot_nt(wh, hl) + _dot_nt(wrl_ref[...], hh)
    scores = 1.0 / (1.0 + jnp.exp(-logits))
    biased = scores + eb_ref[...]
    tt = scores.shape[1]

    gi = lax.broadcasted_iota(I32, (GROUP_SIZE, tt), 0)
    gs = []
    for g in range(N_GROUPS):
        blk = biased[g * GROUP_SIZE:(g + 1) * GROUP_SIZE, :]
        m1, first = _first_index_of_max(blk, gi, GROUP_SIZE)
        m2 = jnp.max(jnp.where(gi == first, -jnp.inf, blk), axis=0, keepdims=True)
        gs.append(m1 + m2)
    gscore = jnp.concatenate(gs, axis=0)
    gidx = lax.broadcasted_iota(I32, (N_GROUPS, tt), 0)
    gsel = jnp.zeros((N_GROUPS, tt), F32)
    for _ in range(TOPK_GROUPS):
        _, first = _first_index_of_max(gscore, gidx, N_GROUPS)
        hit = gidx == first
        gsel = jnp.where(hit, 1.0, gsel)
        gscore = jnp.where(hit, -jnp.inf, gscore)
    cand = jnp.concatenate(
        [jnp.where(gsel[g:g + 1, :] > 0.0, biased[g * GROUP_SIZE:(g + 1) * GROUP_SIZE, :], -jnp.inf)
         for g in range(N_GROUPS)], axis=0)

    ei = lax.broadcasted_iota(I32, (N_EXPERTS, tt), 0)
    picked = jnp.zeros((N_EXPERTS, tt), F32)
    idxs, ws, hits = [], [], []
    for _ in range(TOP_K):
        _, first = _first_index_of_max(cand, ei, N_EXPERTS)
        hit = ei == first
        idxs.append(first)
        ws.append(jnp.sum(jnp.where(hit, scores, 0.0), axis=0, keepdims=True))
        hits.append(hit)
        picked = jnp.where(hit, 1.0, picked)
        cand = jnp.where(hit, -jnp.inf, cand)
    idx_ref[...] = jnp.concatenate(idxs, axis=0)
    w_t = jnp.concatenate(ws, axis=0)
    w_t = w_t / jnp.sum(w_t, axis=0, keepdims=True) * ROUTED_SCALE
    pad = jnp.zeros((128 - TOP_K, tt), F32)
    wn_ref[...] = jnp.concatenate([w_t, pad], axis=0).T[:, :TOP_K]

    tr = lax.broadcasted_iota(I32, (tt, tt), 0)
    tc = lax.broadcasted_iota(I32, (tt, tt), 1)
    before = jnp.where(tr < tc, 1.0, 0.0).astype(BF16)
    seen = _dot(picked.astype(BF16), before) + base_ref[...]
    rank_ref[...] = jnp.concatenate(
        [jnp.sum(jnp.where(h, seen, 0.0), axis=0, keepdims=True) for h in hits], axis=0).astype(I32)
    base_ref[...] += jnp.sum(picked, axis=1, keepdims=True)
    cnt_ref[...] = base_ref[...].astype(I32)


def _outproj_route(mixed_d, mixed_s, x2, mod, w_out, g_ffn, wr_hi, wr_lo, e_bias, s):
    t, d = x2.shape
    half = mixed_d.shape[1]
    per_b = s // TT
    const = lambda shape: pl.BlockSpec(shape, lambda i: (0,) * len(shape))
    tok = lambda wd: pl.BlockSpec((TT, wd), lambda i: (i, 0))
    lane = pl.BlockSpec((TOP_K, TT), lambda i: (0, i))
    return pl.pallas_call(
        _route_kernel,
        out_shape=(jax.ShapeDtypeStruct((t, d), F32),
                   jax.ShapeDtypeStruct((t, d // 4), U32),
                   jax.ShapeDtypeStruct((t, d // 4), U32),
                   jax.ShapeDtypeStruct((TOP_K, t), I32),
                   jax.ShapeDtypeStruct((t, TOP_K), F32),
                   jax.ShapeDtypeStruct((TOP_K, t), I32),
                   jax.ShapeDtypeStruct((N_EXPERTS, 1), I32)),
        grid=(t // TT,),
        in_specs=[tok(half), tok(half), tok(d),
                  pl.BlockSpec((1, N_MOD, d), lambda i: (i // per_b, 0, 0)),
                  const(w_out.shape), const((1, d)), const(wr_hi.shape), const(wr_lo.shape),
                  const((N_EXPERTS, 1))],
        out_specs=(tok(d), tok(d // 4), tok(d // 4), lane, tok(TOP_K), lane, const((N_EXPERTS, 1))),
        scratch_shapes=[pltpu.VMEM((N_EXPERTS, 1), F32)],
        compiler_params=_cparams(("arbitrary",)),
        name="outproj_route",
    )(mixed_d, mixed_s, x2, mod, w_out, g_ffn, wr_hi, wr_lo, e_bias)


def _pos_kernel(idx_ref, rank_ref, pstart_ref, pos_ref):
    tt = idx_ref.shape[1]
    ei = lax.broadcasted_iota(I32, (N_EXPERTS, tt), 0)
    ps = pstart_ref[...]
    rows = []
    for k in range(TOP_K):
        hit = ei == idx_ref[k:k + 1, :]
        rows.append(jnp.sum(jnp.where(hit, ps, 0), axis=0, keepdims=True))
    pos_ref[...] = jnp.concatenate(rows, axis=0) + rank_ref[...]


def _slot_positions(idx_t, rank_t, pstart):
    k, t = idx_t.shape
    tp = 1024
    lane = pl.BlockSpec((k, tp), lambda i: (0, i))
    return pl.pallas_call(
        _pos_kernel,
        out_shape=jax.ShapeDtypeStruct((k, t), I32),
        grid=(t // tp,),
        in_specs=[lane, lane, pl.BlockSpec((N_EXPERTS, 1), lambda i: (0, 0))],
        out_specs=lane,
        compiler_params=_cparams(("parallel",)),
        name="slot_positions",
    )(idx_t, rank_t, pstart)


def _expert_kernel(be_ref, nv_ref, nu_ref, xa_ref, xb_ref, wgu_ref, wd_ref, ya_ref, yb_ref):
    b = pl.program_id(0)
    nv = nv_ref[b]

    for v in range(EXPERT_SUBS + 1):
        @pl.when(nv == v)
        def _(v=v):
            rows = v * EXPERT_ROWS
            if rows:
                x = _unpack_rows(xa_ref[:rows, :], xb_ref[:rows, :]).astype(BF16)
                gu = _dot(x, wgu_ref[0])
                ff = gu.shape[1] // 2
                act = _silu(gu[:, :ff]) * gu[:, ff:]
                y = _dot(act.astype(BF16), wd_ref[0])
                ya_ref[:rows, :], yb_ref[:rows, :] = _pack_rows(y)
            if rows < STEP_ROWS:
                ya_ref[rows:, :] = jnp.zeros((STEP_ROWS - rows, ya_ref.shape[1]), U32)
                yb_ref[rows:, :] = jnp.zeros((STEP_ROWS - rows, yb_ref.shape[1]), U32)


def _experts(block_e, n_valid, n_used, xs_a, xs_b, w_gu, w_down):
    p, dp = xs_a.shape
    n_steps = p // STEP_ROWS
    _, d, f2 = w_gu.shape
    _, f, _ = w_down.shape
    x_spec = pl.BlockSpec((STEP_ROWS, dp), lambda b, be, nv, nu: (jnp.minimum(b, nu[0] - 1), 0))
    y_spec = pl.BlockSpec((STEP_ROWS, dp), lambda b, be, nv, nu: (jnp.where(b < nu[0], b, n_steps), 0))
    grid_spec = pltpu.PrefetchScalarGridSpec(
        num_scalar_prefetch=3,
        grid=(n_steps,),
        in_specs=[x_spec, x_spec,
                  pl.BlockSpec((1, d, f2), lambda b, be, nv, nu: (be[b], 0, 0)),
                  pl.BlockSpec((1, f, d), lambda b, be, nv, nu: (be[b], 0, 0))],
        out_specs=(y_spec, y_spec),
    )
    y_shape = jax.ShapeDtypeStruct((p + STEP_ROWS, dp), U32)
    return pl.pallas_call(
        _expert_kernel,
        out_shape=(y_shape, y_shape),
        grid_spec=grid_spec,
        compiler_params=_cparams(("arbitrary",)),
        name="routed_experts",
    )(block_e, n_valid, n_used, xs_a, xs_b, w_gu, w_down)


def _shared_kernel(h2a_ref, h2b_ref, wsg_ref, wsd_ref, o_ref):
    h2 = _unpack_rows(h2a_ref[...], h2b_ref[...]).astype(BF16)
    gu = _dot(h2, wsg_ref[...])
    ff = gu.shape[1] // 2
    o_ref[...] = _dot((_silu(gu[:, :ff]) * gu[:, ff:]).astype(BF16), wsd_ref[...]).astype(BF16)


def _shared(h2a, h2b, w_sh_gu, w_sh_down):
    t, dq = h2a.shape
    d = w_sh_down.shape[1]
    const = lambda shape: pl.BlockSpec(shape, lambda i: (0,) * len(shape))
    tok = lambda wd: pl.BlockSpec((TT_SHARED, wd), lambda i: (i, 0))
    return pl.pallas_call(
        _shared_kernel,
        out_shape=jax.ShapeDtypeStruct((t, d), BF16),
        grid=(t // TT_SHARED,),
        in_specs=[tok(dq), tok(dq), const(w_sh_gu.shape), const(w_sh_down.shape)],
        out_specs=tok(d),
        compiler_params=_cparams(("parallel",)),
        name="shared_expert",
    )(h2a, h2b, w_sh_gu, w_sh_down)


def _combine_kernel(r1_ref, ysh_ref, yga_ref, ygb_ref, wn_ref, mod_ref, o_ref):
    wn = wn_ref[...]
    y = ysh_ref[...].astype(F32)
    for k in range(TOP_K):
        y = y + wn[:, k:k + 1] * _unpack_rows(yga_ref[k], ygb_ref[k])
    o_ref[...] = r1_ref[...] + mod_ref[0, 5:6, :] * y


def _combine(r1, y_sh, yg_a, yg_b, w_nat, mod, s):
    t, d = r1.shape
    per_b = s // TT
    tok = lambda wd: pl.BlockSpec((TT, wd), lambda i: (i, 0))
    return pl.pallas_call(
        _combine_kernel,
        out_shape=jax.ShapeDtypeStruct((t, d), F32),
        grid=(t // TT,),
        in_specs=[tok(d), tok(d),
                  pl.BlockSpec((TOP_K, TT, d // 4), lambda i: (0, i, 0)),
                  pl.BlockSpec((TOP_K, TT, d // 4), lambda i: (0, i, 0)),
                  tok(TOP_K),
                  pl.BlockSpec((1, N_MOD, d), lambda i: (i // per_b, 0, 0))],
        out_specs=tok(d),
        compiler_params=_cparams(("parallel",)),
        name="routed_combine",
    )(r1, y_sh, yg_a, yg_b, w_nat, mod)


def _sc_mesh():
    return plsc.VectorSubcoreMesh(core_axis_name="c", subcore_axis_name="s")


def _sc_dispatch(xa, xb, pos_t, p):
    t, dc = xa.shape
    k = pos_t.shape[0]
    out = jax.ShapeDtypeStruct((p, dc), xa.dtype)

    @functools.partial(pl.kernel, out_type=(out, out), mesh=_sc_mesh())
    def kern(xa_hbm, xb_hbm, pos_hbm, oa_hbm, ob_hbm):
        for src, dst in ((xa_hbm, oa_hbm), (xb_hbm, ob_hbm)):
            def body(x_vmem, i_vmem, dst=dst):
                pltpu.sync_copy(x_vmem, dst.at[i_vmem.at[0]])

            pltpu.emit_pipeline(
                body,
                grid=(t // SC_WINDOW, k),
                in_specs=[pl.BlockSpec((SC_WINDOW, dc), lambda i, kk: (i, 0)),
                          pl.BlockSpec((1, SC_WINDOW), lambda i, kk: (kk, i))],
                out_specs=[],
                core_axis_name=("c", "s"),
                dimension_semantics=(pltpu.PARALLEL, pltpu.ARBITRARY),
            )(src, pos_hbm)

    return kern(xa, xb, pos_t)


def _sc_gather(ya, yb, pos_row):
    n = pos_row.shape[1]
    dc = ya.shape[1]
    out = jax.ShapeDtypeStruct((n, dc), ya.dtype)

    @functools.partial(pl.kernel, out_type=(out, out), mesh=_sc_mesh())
    def kern(ya_hbm, yb_hbm, pos_hbm, oa_hbm, ob_hbm):
        for src, dst in ((ya_hbm, oa_hbm), (yb_hbm, ob_hbm)):
            def body(i_vmem, o_vmem, src=src):
                pltpu.sync_copy(src.at[i_vmem.at[0]], o_vmem)

            pltpu.emit_pipeline(
                body,
                grid=(n // SC_WINDOW,),
                in_specs=[pl.BlockSpec((1, SC_WINDOW), lambda i: (0, i))],
                out_specs=[pl.BlockSpec((SC_WINDOW, dc), lambda i: (i, 0))],
                core_axis_name=("c", "s"),
                dimension_semantics=(pltpu.PARALLEL,),
            )(pos_hbm, dst)

    return kern(ya, yb, pos_row)


def kernel(x, c, positions, w_ada, b_ada, g_mix, w_in, g_q, g_k, lam, g_sub, w_out, g_ffn, w_router,
           e_bias, w_sh_gu, w_sh_down, w_gu, w_down):
    b, s, d = x.shape
    assert w_ada.shape[0] == 1 and s % TS == 0 and d % 256 == 0 and (b * s) % TT_SHARED == 0
    t = b * s
    wdt = w_in.shape[2] // 6

    c_pad = jnp.zeros((8, d), F32).at[:b].set(c.astype(F32))
    mod = _ada(c_pad, w_ada[0], b_ada)[:b].reshape(b, N_MOD, d)

    sec = [w_in[0][:, i * wdt:(i + 1) * wdt] for i in range(6)]
    w_t = jnp.concatenate([sec[0], sec[1], sec[2], sec[3], sec[5]], axis=1).T.astype(BF16)
    w_sk = sec[4].astype(BF16)
    inv = (ROPE_THETA ** (-jnp.arange(0, HEAD_DIM, 2, dtype=F32) / HEAD_DIM)).reshape(HALF, 1)
    pos3 = positions.astype(I32).reshape(b, 1, s)

    qd, kd, vd, qs, ks, vs = _inproj(x, mod, g_mix, pos3, inv, g_q.reshape(HEAD_DIM, 1),
                                     g_k.reshape(HEAD_DIM, 1), w_t, w_sk)
    mixed_d, w_gu_b = _diff_attention(qd, kd, vd, lam[0], g_sub.reshape(2 * HEAD_DIM, 1), w_gu[0])
    mixed_s, w_down_b = _sb_attention(qs, ks, vs, w_down[0])

    wr_t = w_router[0].T
    wr_hi = wr_t.astype(BF16)
    wr_lo = (wr_t - wr_hi.astype(F32)).astype(BF16)
    r1, h2a, h2b, idx_t, w_nat, rank_t, counts = _outproj_route(
        mixed_d.reshape(t, wdt), mixed_s.reshape(t, wdt), x.reshape(t, d), mod,
        w_out[0].astype(BF16), g_ffn, wr_hi, wr_lo, e_bias.reshape(N_EXPERTS, 1), s)

    counts = counts.reshape(N_EXPERTS)
    padded = (counts + STEP_ROWS - 1) // STEP_ROWS * STEP_ROWS
    pend = jnp.cumsum(padded)
    pstart = (pend - padded).astype(I32)
    n_steps = t * TOP_K // STEP_ROWS + N_EXPERTS
    step_row0 = jnp.arange(n_steps, dtype=pend.dtype) * STEP_ROWS
    block_e = jnp.minimum(jnp.sum(pend[None, :] <= step_row0[:, None], axis=1), N_EXPERTS - 1).astype(I32)
    rows_left = counts[block_e] - (step_row0 - pstart[block_e])
    n_valid = jnp.where(step_row0 < pend[-1],
                        jnp.clip((rows_left + EXPERT_ROWS - 1) // EXPERT_ROWS, 0, EXPERT_SUBS), 0).astype(I32)
    n_used = (pend[-1:] // STEP_ROWS).astype(I32)
    pos_t = _slot_positions(idx_t, rank_t, pstart.reshape(N_EXPERTS, 1))

    xs_a, xs_b = _sc_dispatch(h2a, h2b, pos_t, n_steps * STEP_ROWS)
    y_a, y_b = _experts(block_e, n_valid, n_used, xs_a, xs_b, w_gu_b, w_down_b)
    y_sh = _shared(h2a, h2b, w_sh_gu[0].astype(BF16), w_sh_down[0].astype(BF16))
    yg_a, yg_b = _sc_gather(y_a, y_b, pos_t.reshape(1, TOP_K * t))
    dq4 = d // 4
    out = _combine(r1, y_sh, yg_a.reshape(TOP_K, t, dq4), yg_b.reshape(TOP_K, t, dq4), w_nat, mod, s)
    return out.reshape(b, s, d).astype(x.dtype)
```

```python
import functools
import math

import jax
import jax.numpy as jnp
from jax import lax
from jax.experimental import pallas as pl
from jax.experimental.pallas import tpu as pltpu
from jax.experimental.pallas import tpu_sc as plsc

F32 = jnp.float32
BF16 = jnp.bfloat16
U32 = jnp.uint32
I32 = jnp.int32

HEAD_DIM = 64
HALF = HEAD_DIM // 2
ROPE_THETA = 10000.0
N_EXPERTS = 256
TOP_K = 8
N_GROUPS = 8
GROUP_SIZE = N_EXPERTS // N_GROUPS
TOPK_GROUPS = 4
EXPERT_FF = 256
ROUTED_SCALE = 2.5
EXPERT_ROWS = 128
EXPERT_SUBS = 5
STEP_ROWS = EXPERT_SUBS * EXPERT_ROWS
NORM_EPS = 1e-6
SB_LOG_FLOOR = -110.0
NSTREAM = 4
N_MOD = 6
LAMBDA_INIT = 0.8 - 0.6 * math.exp(-0.3 * 0)
Q_SCALE = HEAD_DIM ** -0.5
LOG2E = math.log2(math.e)
SUM_ROWS = 16
DV_ROWS = 2 * HEAD_DIM + SUM_ROWS

TS = 512
TB = 256
TT = 256
TT_SHARED = 1024
HI_MASK = 0xFFFF0000
SIGN_BIT = 0x80000000
SC_WINDOW = 128

VMEM_LIMIT = 48 * 1024 * 1024


def _cparams(sem):
    return pltpu.CompilerParams(dimension_semantics=sem, vmem_limit_bytes=VMEM_LIMIT)


def _split_bf16(a):
    hi = a.astype(BF16)
    lo = (a - hi.astype(F32)).astype(BF16)
    return hi, lo


def _dot(a, b):
    return jnp.dot(a, b, preferred_element_type=F32)


def _dot_nt(a, b):
    return lax.dot_general(a, b, (((1,), (1,)), ((), ())), preferred_element_type=F32)


def _silu(x):
    return x / (1.0 + jnp.exp(-x))


def _pack_bf16_pair(a, b):
    ab = pltpu.bitcast(a.astype(BF16).astype(F32), U32)
    bb = pltpu.bitcast(b.astype(BF16).astype(F32), U32)
    return (ab & jnp.uint32(HI_MASK)) | (bb >> 16)


def _unpack_bf16_pair(p):
    a = pltpu.bitcast(p & jnp.uint32(HI_MASK), F32)
    b = pltpu.bitcast(p << 16, F32)
    return a, b


def _pack_rows(v):
    q = v.shape[1] // 4
    return (_pack_bf16_pair(v[:, :q], v[:, q:2 * q]),
            _pack_bf16_pair(v[:, 2 * q:3 * q], v[:, 3 * q:]))


def _unpack_rows(pa, pb):
    return jnp.concatenate(_unpack_bf16_pair(pa) + _unpack_bf16_pair(pb), axis=1)


def _ada_kernel(c_ref, w_ref, b_ref, o_ref):
    cond = _silu(c_ref[...])
    ch, cl = _split_bf16(cond)
    wh, wl = _split_bf16(w_ref[...])
    o_ref[...] = _dot(ch, wh) + _dot(ch, wl) + _dot(cl, wh) + b_ref[...]


def _ada(c_pad, w_ada, b_ada):
    d, n = w_ada.shape
    tn = n // 4
    return pl.pallas_call(
        _ada_kernel,
        out_shape=jax.ShapeDtypeStruct((c_pad.shape[0], n), F32),
        grid=(n // tn,),
        in_specs=[pl.BlockSpec((c_pad.shape[0], d), lambda j: (0, 0)),
                  pl.BlockSpec((d, tn), lambda j: (0, j)),
                  pl.BlockSpec((1, tn), lambda j: (0, j))],
        out_specs=pl.BlockSpec((c_pad.shape[0], tn), lambda j: (0, j)),
        compiler_params=_cparams(("arbitrary",)),
        name="ada_mod",
    )(c_pad, w_ada, b_ada)


def _norm_rope_t(t, g_col, cos_t, sin_t):
    outs = []
    for i in range(t.shape[0] // HEAD_DIM):
        blk = t[i * HEAD_DIM:(i + 1) * HEAD_DIM, :]
        ms = jnp.mean(blk * blk, axis=0, keepdims=True)
        y = blk * lax.rsqrt(ms + NORM_EPS) * g_col
        y1 = y[:HALF]
        y2 = y[HALF:]
        outs.append(y1 * cos_t - y2 * sin_t)
        outs.append(y2 * cos_t + y1 * sin_t)
    return jnp.concatenate(outs, axis=0)


def _inproj_kernel(x_ref, mod_ref, g_ref, pos_ref, inv_ref, gq_ref, gk_ref, wt_ref, wsk_ref,
                   qd_ref, kd_ref, vd_ref, qs_ref, ks_ref, vs_ref):
    x = x_ref[0]
    ms = jnp.mean(x * x, axis=-1, keepdims=True)
    sh = mod_ref[0, 0:1, :]
    sc = mod_ref[0, 1:2, :]
    h = x * lax.rsqrt(ms + NORM_EPS) * g_ref[...] * (1.0 + sc) + sh
    hb = h.astype(BF16)
    w = qd_ref.shape[1]

    def proj_t(sec):
        return _dot_nt(wt_ref[sec * w:(sec + 1) * w, :], hb)

    ang = inv_ref[...] * pos_ref[0].astype(F32)
    cos_t = jnp.cos(ang)
    sin_t = jnp.sin(ang)
    q_t = _norm_rope_t(proj_t(0), gq_ref[...], cos_t, sin_t) * (Q_SCALE * LOG2E)
    qd_ref[0] = q_t.astype(BF16)
    k_t = _norm_rope_t(proj_t(1), gk_ref[...], cos_t, sin_t)
    kd_ref[0] = k_t.T.astype(BF16)
    v_t = proj_t(2).astype(BF16)
    s_t = proj_t(4).astype(BF16)
    nh = vd_ref.shape[1]
    hw = vs_ref.shape[3]
    ones = jnp.ones((SUM_ROWS, TB), BF16)
    for hh in range(nh):
        for jj in range(vd_ref.shape[2]):
            vd_ref[0, hh, jj] = jnp.concatenate(
                [v_t[hh * hw:(hh + 1) * hw, jj * TB:(jj + 1) * TB], ones], axis=0)
            vs_ref[0, hh, jj] = s_t[hh * hw:(hh + 1) * hw, jj * TB:(jj + 1) * TB]
    qs_ref[0] = (proj_t(3) * Q_SCALE).astype(BF16)
    ks_ref[0] = _dot(hb, wsk_ref[...]).astype(BF16)


def _inproj(x, mod, g_mix, pos3, inv, g_q, g_k, w_t, w_sk):
    b, s, d = x.shape
    w = w_sk.shape[1]
    nh = w // 128
    nsb = TS // TB
    const = lambda shape: pl.BlockSpec(shape, lambda bi, si: (0,) * len(shape))
    t_spec = pl.BlockSpec((1, w, TS), lambda bi, si: (bi, 0, si))
    n_spec = pl.BlockSpec((1, TS, w), lambda bi, si: (bi, si, 0))
    v_spec = lambda rows: pl.BlockSpec((1, nh, nsb, rows, TB), lambda bi, si: (bi, 0, si, 0, 0))
    t_shape = jax.ShapeDtypeStruct((b, w, s), BF16)
    n_shape = jax.ShapeDtypeStruct((b, s, w), BF16)
    v_shape = lambda rows: jax.ShapeDtypeStruct((b, nh, s // TB, rows, TB), BF16)
    return pl.pallas_call(
        _inproj_kernel,
        out_shape=(t_shape, n_shape, v_shape(DV_ROWS), t_shape, n_shape, v_shape(128)),
        grid=(b, s // TS),
        in_specs=[pl.BlockSpec((1, TS, d), lambda bi, si: (bi, si, 0)),
                  pl.BlockSpec((1, N_MOD, d), lambda bi, si: (bi, 0, 0)),
                  const((1, d)),
                  pl.BlockSpec((1, 1, TS), lambda bi, si: (bi, 0, si)),
                  const((HALF, 1)), const((HEAD_DIM, 1)), const((HEAD_DIM, 1)),
                  const(w_t.shape), const(w_sk.shape)],
        out_specs=(t_spec, n_spec, v_spec(DV_ROWS), t_spec, n_spec, v_spec(128)),
        compiler_params=_cparams(("parallel", "parallel")),
        name="norm_inproj",
    )(x, mod, g_mix, pos3, inv, g_q, g_k, w_t, w_sk)


def _q_pair(q_t):
    row = lax.broadcasted_iota(I32, q_t.shape, 0)
    zero = jnp.zeros_like(q_t)
    return jnp.concatenate([jnp.where(row < HEAD_DIM, q_t, zero),
                            jnp.where(row >= HEAD_DIM, q_t, zero)], axis=1)


def _key_query_iotas(tk, tq):
    r = lax.broadcasted_iota(I32, (tk, 2 * tq), 0)
    c = lax.broadcasted_iota(I32, (tk, 2 * tq), 1)
    return r, jnp.where(c >= tq, c - tq, c)


def _diff_kernel(qt_ref, k_ref, vt_ref, lam_ref, gsub_ref, o_ref, m_ref, acc_ref, s0, s1, p0, p1, a0, a1):
    qi = pl.program_id(2)
    tq = qt_ref.shape[2]
    dv = 2 * HEAD_DIM
    qcats = [_q_pair(qt_ref[0, st * 128:(st + 1) * 128, :]) for st in range(NSTREAM)]
    m_ref[...] = jnp.full(m_ref.shape, -jnp.inf, F32)
    acc_ref[...] = jnp.zeros(acc_ref.shape, F32)
    s_slots, p_slots, a_slots = (s0, s1), (p0, p1), (a0, a1)
    p_slots[1][...] = jnp.zeros(p_slots[1].shape, BF16)
    a_slots[1][...] = jnp.ones(a_slots[1].shape, F32)

    def scores(j, slot):
        row0 = pl.multiple_of(j * TB, TB)
        for st in range(NSTREAM):
            s_slots[slot][st] = _dot(k_ref[0, pl.ds(row0, TB), st * 128:(st + 1) * 128], qcats[st])

    def softmax(slot, diagonal):
        for st in range(NSTREAM):
            s = s_slots[slot][st]
            if diagonal:
                r, c = _key_query_iotas(TB, tq)
                s = jnp.where(r <= c, s, -jnp.inf)
            m = m_ref[st]
            m_new = jnp.maximum(m, jnp.max(s, axis=0, keepdims=True))
            a_slots[slot][st] = jnp.exp2(m - m_new)
            p_slots[slot][st] = jnp.exp2(s - m_new).astype(BF16)
            m_ref[st] = m_new

    def weighted_values(j, slot):
        for st in range(NSTREAM):
            acc_ref[st] = (acc_ref[st] * a_slots[slot][st]
                           + _dot(vt_ref[0, st, j], p_slots[slot][st]))

    def trip(j, slot):
        scores(j + 1, 1 - slot)
        softmax(slot, False)
        weighted_values(jnp.maximum(j - 1, 0), 1 - slot)

    def body(i, carry):
        trip(2 * i, 0)
        trip(2 * i + 1, 1)
        return carry

    scores(0, 0)
    lax.fori_loop(0, qi // 2, body, 0)

    @pl.when(qi % 2 == 1)
    def _():
        trip(qi - 1, 0)
        softmax(1, True)
        weighted_values(qi - 1, 0)
        weighted_values(qi, 1)

    @pl.when(qi % 2 == 0)
    def _():
        softmax(0, True)
        weighted_values(jnp.maximum(qi - 1, 0), 1)
        weighted_values(qi, 0)

    lam = lam_ref[...]
    l1 = jnp.sum(lam[0:1] * lam[1:2], axis=-1, keepdims=True)
    l2 = jnp.sum(lam[2:3] * lam[3:4], axis=-1, keepdims=True)
    lmbda = jnp.exp(l1) - jnp.exp(l2) + LAMBDA_INIT
    for st in range(NSTREAM):
        acc = acc_ref[st, :dv, :]
        l = acc_ref[st, dv:dv + 1, :]
        o = acc[:, :tq] / l[:, :tq] - lmbda * (acc[:, tq:] / l[:, tq:])
        ms = jnp.mean(o * o, axis=0, keepdims=True)
        y = o * lax.rsqrt(ms + NORM_EPS) * gsub_ref[...] * (1.0 - LAMBDA_INIT)
        o_ref[0, :, st * 128:(st + 1) * 128] = y.T.astype(BF16)


def _attention_specs(b, w, s, v_rows):
    gw = NSTREAM * 128
    in_specs = [pl.BlockSpec((1, gw, TB), lambda bi, gi, qi: (bi, gi, qi)),
                pl.BlockSpec((1, s, gw), lambda bi, gi, qi: (bi, 0, gi)),
                pl.BlockSpec((1, NSTREAM, s // TB, v_rows, TB), lambda bi, gi, qi: (bi, gi, 0, 0, 0))]
    out_spec = pl.BlockSpec((1, TB, gw), lambda bi, gi, qi: (bi, qi, gi))
    return (b, w // gw, s // TB), in_specs, out_spec


def _diff_attention(q_t, k, v_t, lam, g_sub):
    b, w, s = q_t.shape
    grid, in_specs, out_spec = _attention_specs(b, w, s, DV_ROWS)
    return pl.pallas_call(
        _diff_kernel,
        out_shape=jax.ShapeDtypeStruct((b, s, w), BF16),
        grid=grid,
        in_specs=in_specs + [pl.BlockSpec(lam.shape, lambda bi, gi, qi: (0, 0)),
                             pl.BlockSpec(g_sub.shape, lambda bi, gi, qi: (0, 0))],
        out_specs=out_spec,
        scratch_shapes=[pltpu.VMEM((NSTREAM, 1, 2 * TB), F32), pltpu.VMEM((NSTREAM, DV_ROWS, 2 * TB), F32),
                        pltpu.VMEM((NSTREAM, TB, 2 * TB), F32), pltpu.VMEM((NSTREAM, TB, 2 * TB), F32),
                        pltpu.VMEM((NSTREAM, TB, 2 * TB), BF16), pltpu.VMEM((NSTREAM, TB, 2 * TB), BF16),
                        pltpu.VMEM((NSTREAM, 1, 2 * TB), F32), pltpu.VMEM((NSTREAM, 1, 2 * TB), F32)],
        compiler_params=_cparams(("parallel", "parallel", "parallel")),
        name="diff_attention",
    )(q_t, k, v_t, lam, g_sub)


def _sb_kernel(qt_ref, k_ref, vt_ref, o_ref, c_ref, acc_ref):
    qi = pl.program_id(2)
    tq = qt_ref.shape[2]
    qcats = [_q_pair(qt_ref[0, st * 128:(st + 1) * 128, :]) for st in range(NSTREAM)]
    ur = lax.broadcasted_iota(I32, (TB, TB), 0)
    uc = lax.broadcasted_iota(I32, (TB, TB), 1)
    u_neg = jnp.where(uc > ur, -1.0, 0.0).astype(BF16)
    c_ref[...] = jnp.zeros(c_ref.shape, F32)
    acc_ref[...] = jnp.zeros(acc_ref.shape, F32)

    def step(j, diagonal):
        row0 = pl.multiple_of(j * TB, TB)
        if diagonal:
            r, c = _key_query_iotas(TB, tq)
            valid = r < c
        zs = [_dot(k_ref[0, pl.ds(row0, TB), st * 128:(st + 1) * 128], qcats[st])
              for st in range(NSTREAM)]
        pre = []
        for st in range(NSTREAM):
            z = zs[st]
            neg_abs = pltpu.bitcast(pltpu.bitcast(z, U32) | jnp.uint32(SIGN_BIT), F32)
            sp = jnp.maximum(z, 0.0) + jnp.log(1.0 + jnp.exp(neg_abs))
            drop = jnp.where(valid, sp, 0.0) if diagonal else sp
            pre.append((z - sp, drop.astype(BF16), drop[0:1, :]))
        laters = [_dot(u_neg, pre[st][1]) for st in range(NSTREAM)]
        for st in range(NSTREAM):
            c_run = c_ref[st]
            w = jnp.exp(pre[st][0] + laters[st] + c_run)
            if diagonal:
                w = jnp.where(valid, w, 0.0)
            wb = w.astype(BF16)
            vj = vt_ref[0, st, j]
            acc_ref[st] += jnp.concatenate([_dot(vj[:HEAD_DIM], wb[:, :tq]),
                                            _dot(vj[HEAD_DIM:], wb[:, tq:])], axis=0)
            c_ref[st] = c_run + laters[st][0:1, :] - pre[st][2]

    def live():
        return jnp.max(c_ref[...]) > SB_LOG_FLOOR

    step(qi, True)

    def body(i):
        step(qi - 1 - i, False)
        return i + 1

    lax.while_loop(lambda i: (i < qi) & live(), body, jnp.int32(0))
    for st in range(NSTREAM):
        o_ref[0, :, st * 128:(st + 1) * 128] = acc_ref[st].T.astype(BF16)


def _sb_attention(q_t, k, v_t):
    b, w, s = q_t.shape
    grid, in_specs, out_spec = _attention_specs(b, w, s, 2 * HEAD_DIM)
    return pl.pallas_call(
        _sb_kernel,
        out_shape=jax.ShapeDtypeStruct((b, s, w), BF16),
        grid=grid,
        in_specs=in_specs,
        out_specs=out_spec,
        scratch_shapes=[pltpu.VMEM((NSTREAM, 1, 2 * TB), F32), pltpu.VMEM((NSTREAM, 2 * HEAD_DIM, TB), F32)],
        compiler_params=_cparams(("parallel", "parallel", "parallel")),
        name="sb_attention",
    )(q_t, k, v_t)


def _first_index_of_max(v, idx, size):
    m = jnp.max(v, axis=0, keepdims=True)
    first = jnp.min(jnp.where(v == m, idx, size), axis=0, keepdims=True)
    return m, first


def _route_kernel(md_ref, ms_ref, x_ref, mod_ref, wo_ref, g_ref, wrh_ref, wrl_ref, eb_ref,
                  r1_ref, h2a_ref, h2b_ref, idx_ref, wn_ref, rank_ref, cnt_ref, base_ref):
    i = pl.program_id(0)

    @pl.when(i == 0)
    def _():
        base_ref[...] = jnp.zeros_like(base_ref)

    half = md_ref.shape[1]
    attn = _dot(md_ref[...], wo_ref[:half, :]) + _dot(ms_ref[...], wo_ref[half:, :])
    r1 = x_ref[...] + mod_ref[0, 2:3, :] * attn
    r1_ref[...] = r1
    ms = jnp.mean(r1 * r1, axis=-1, keepdims=True)
    h2 = r1 * lax.rsqrt(ms + NORM_EPS) * g_ref[...] * (1.0 + mod_ref[0, 4:5, :]) + mod_ref[0, 3:4, :]
    h2a_ref[...], h2b_ref[...] = _pack_rows(h2)

    hh, hl = _split_bf16(h2)
    wh = wrh_ref[...]
    logits = _dot_nt(wh, hh) + _dot_nt(wh, hl) + _dot_nt(wrl_ref[...], hh)
    scores = 1.0 / (1.0 + jnp.exp(-logits))
    biased = scores + eb_ref[...]
    tt = scores.shape[1]

    gi = lax.broadcasted_iota(I32, (GROUP_SIZE, tt), 0)
    gs = []
    for g in range(N_GROUPS):
        blk = biased[g * GROUP_SIZE:(g + 1) * GROUP_SIZE, :]
        m1, first = _first_index_of_max(blk, gi, GROUP_SIZE)
        m2 = jnp.max(jnp.where(gi == first, -jnp.inf, blk), axis=0, keepdims=True)
        gs.append(m1 + m2)
    gscore = jnp.concatenate(gs, axis=0)
    gidx = lax.broadcasted_iota(I32, (N_GROUPS, tt), 0)
    gsel = jnp.zeros((N_GROUPS, tt), F32)
    for _ in range(TOPK_GROUPS):
        _, first = _first_index_of_max(gscore, gidx, N_GROUPS)
        hit = gidx == first
        gsel = jnp.where(hit, 1.0, gsel)
        gscore = jnp.where(hit, -jnp.inf, gscore)
    cand = jnp.concatenate(
        [jnp.where(gsel[g:g + 1, :] > 0.0, biased[g * GROUP_SIZE:(g + 1) * GROUP_SIZE, :], -jnp.inf)
         for g in range(N_GROUPS)], axis=0)

    ei = lax.broadcasted_iota(I32, (N_EXPERTS, tt), 0)
    picked = jnp.zeros((N_EXPERTS, tt), F32)
    idxs, ws, hits = [], [], []
    for _ in range(TOP_K):
        _, first = _first_index_of_max(cand, ei, N_EXPERTS)
        hit = ei == first
        idxs.append(first)
        ws.append(jnp.sum(jnp.where(hit, scores, 0.0), axis=0, keepdims=True))
        hits.append(hit)
        picked = jnp.where(hit, 1.0, picked)
        cand = jnp.where(hit, -jnp.inf, cand)
    idx_ref[...] = jnp.concatenate(idxs, axis=0)
    w_t = jnp.concatenate(ws, axis=0)
    w_t = w_t / jnp.sum(w_t, axis=0, keepdims=True) * ROUTED_SCALE
    pad = jnp.zeros((128 - TOP_K, tt), F32)
    wn_ref[...] = jnp.concatenate([w_t, pad], axis=0).T[:, :TOP_K]

    tr = lax.broadcasted_iota(I32, (tt, tt), 0)
    tc = lax.broadcasted_iota(I32, (tt, tt), 1)
    before = jnp.where(tr < tc, 1.0, 0.0).astype(BF16)
    seen = _dot(picked.astype(BF16), before) + base_ref[...]
    rank_ref[...] = jnp.concatenate(
        [jnp.sum(jnp.where(h, seen, 0.0), axis=0, keepdims=True) for h in hits], axis=0).astype(I32)
    base_ref[...] += jnp.sum(picked, axis=1, keepdims=True)
    cnt_ref[...] = base_ref[...].astype(I32)


def _outproj_route(mixed_d, mixed_s, x2, mod, w_out, g_ffn, wr_hi, wr_lo, e_bias, s):
    t, d = x2.shape
    half = mixed_d.shape[1]
    per_b = s // TT
    const = lambda shape: pl.BlockSpec(shape, lambda i: (0,) * len(shape))
    tok = lambda wd: pl.BlockSpec((TT, wd), lambda i: (i, 0))
    lane = pl.BlockSpec((TOP_K, TT), lambda i: (0, i))
    return pl.pallas_call(
        _route_kernel,
        out_shape=(jax.ShapeDtypeStruct((t, d), F32),
                   jax.ShapeDtypeStruct((t, d // 4), U32),
                   jax.ShapeDtypeStruct((t, d // 4), U32),
                   jax.ShapeDtypeStruct((TOP_K, t), I32),
                   jax.ShapeDtypeStruct((t, TOP_K), F32),
                   jax.ShapeDtypeStruct((TOP_K, t), I32),
                   jax.ShapeDtypeStruct((N_EXPERTS, 1), I32)),
        grid=(t // TT,),
        in_specs=[tok(half), tok(half), tok(d),
                  pl.BlockSpec((1, N_MOD, d), lambda i: (i // per_b, 0, 0)),
                  const(w_out.shape), const((1, d)), const(wr_hi.shape), const(wr_lo.shape),
                  const((N_EXPERTS, 1))],
        out_specs=(tok(d), tok(d // 4), tok(d // 4), lane, tok(TOP_K), lane, const((N_EXPERTS, 1))),
        scratch_shapes=[pltpu.VMEM((N_EXPERTS, 1), F32)],
        compiler_params=_cparams(("arbitrary",)),
        name="outproj_route",
    )(mixed_d, mixed_s, x2, mod, w_out, g_ffn, wr_hi, wr_lo, e_bias)


def _pos_kernel(idx_ref, rank_ref, pstart_ref, pos_ref):
    tt = idx_ref.shape[1]
    ei = lax.broadcasted_iota(I32, (N_EXPERTS, tt), 0)
    ps = pstart_ref[...]
    rows = []
    for k in range(TOP_K):
        hit = ei == idx_ref[k:k + 1, :]
        rows.append(jnp.sum(jnp.where(hit, ps, 0), axis=0, keepdims=True))
    pos_ref[...] = jnp.concatenate(rows, axis=0) + rank_ref[...]


def _slot_positions(idx_t, rank_t, pstart):
    k, t = idx_t.shape
    tp = 1024
    lane = pl.BlockSpec((k, tp), lambda i: (0, i))
    return pl.pallas_call(
        _pos_kernel,
        out_shape=jax.ShapeDtypeStruct((k, t), I32),
        grid=(t // tp,),
        in_specs=[lane, lane, pl.BlockSpec((N_EXPERTS, 1), lambda i: (0, 0))],
        out_specs=lane,
        compiler_params=_cparams(("parallel",)),
        name="slot_positions",
    )(idx_t, rank_t, pstart)


def _expert_kernel(be_ref, nv_ref, xa_ref, xb_ref, wgu_ref, wd_ref, ya_ref, yb_ref, wgu_b, wd_b):
    b = pl.program_id(0)
    nv = nv_ref[b]

    @pl.when(nv > 0)
    def _():
        prev = be_ref[jnp.maximum(b - 1, 0)]

        @pl.when((b == 0) | (be_ref[b] != prev))
        def _():
            wgu_b[...] = wgu_ref[0].astype(BF16)
            wd_b[...] = wd_ref[0].astype(BF16)

    for v in range(EXPERT_SUBS + 1):
        @pl.when(nv == v)
        def _(v=v):
            rows = v * EXPERT_ROWS
            if rows:
                x = _unpack_rows(xa_ref[:rows, :], xb_ref[:rows, :]).astype(BF16)
                gu = _dot(x, wgu_b[...])
                ff = gu.shape[1] // 2
                act = _silu(gu[:, :ff]) * gu[:, ff:]
                y = _dot(act.astype(BF16), wd_b[...])
                ya_ref[:rows, :], yb_ref[:rows, :] = _pack_rows(y)
            if rows < STEP_ROWS:
                ya_ref[rows:, :] = jnp.zeros((STEP_ROWS - rows, ya_ref.shape[1]), U32)
                yb_ref[rows:, :] = jnp.zeros((STEP_ROWS - rows, yb_ref.shape[1]), U32)


def _experts(block_e, n_valid, n_used, xs_a, xs_b, w_gu, w_down):
    p, dp = xs_a.shape
    _, d, f2 = w_gu.shape
    _, f, _ = w_down.shape
    row_spec = pl.BlockSpec((STEP_ROWS, dp), lambda b, be, nv: (b, 0))
    grid_spec = pltpu.PrefetchScalarGridSpec(
        num_scalar_prefetch=2,
        grid=(n_used,),
        in_specs=[row_spec, row_spec,
                  pl.BlockSpec((1, d, f2), lambda b, be, nv: (be[b], 0, 0)),
                  pl.BlockSpec((1, f, d), lambda b, be, nv: (be[b], 0, 0))],
        out_specs=(row_spec, row_spec),
        scratch_shapes=[pltpu.VMEM((d, f2), BF16), pltpu.VMEM((f, d), BF16)],
    )
    y_shape = jax.ShapeDtypeStruct((p, dp), U32)
    return pl.pallas_call(
        _expert_kernel,
        out_shape=(y_shape, y_shape),
        grid_spec=grid_spec,
        compiler_params=_cparams(("arbitrary",)),
        name="routed_experts",
    )(block_e, n_valid, xs_a, xs_b, w_gu, w_down)


def _shared_kernel(h2a_ref, h2b_ref, wsg_ref, wsd_ref, o_ref):
    h2 = _unpack_rows(h2a_ref[...], h2b_ref[...]).astype(BF16)
    gu = _dot(h2, wsg_ref[...])
    ff = gu.shape[1] // 2
    o_ref[...] = _dot((_silu(gu[:, :ff]) * gu[:, ff:]).astype(BF16), wsd_ref[...]).astype(BF16)


def _shared(h2a, h2b, w_sh_gu, w_sh_down):
    t, dq = h2a.shape
    d = w_sh_down.shape[1]
    const = lambda shape: pl.BlockSpec(shape, lambda i: (0,) * len(shape))
    tok = lambda wd: pl.BlockSpec((TT_SHARED, wd), lambda i: (i, 0))
    return pl.pallas_call(
        _shared_kernel,
        out_shape=jax.ShapeDtypeStruct((t, d), BF16),
        grid=(t // TT_SHARED,),
        in_specs=[tok(dq), tok(dq), const(w_sh_gu.shape), const(w_sh_down.shape)],
        out_specs=tok(d),
        compiler_params=_cparams(("parallel",)),
        name="shared_expert",
    )(h2a, h2b, w_sh_gu, w_sh_down)


def _combine_kernel(r1_ref, ysh_ref, yga_ref, ygb_ref, wn_ref, mod_ref, o_ref):
    wn = wn_ref[...]
    y = ysh_ref[...].astype(F32)
    for k in range(TOP_K):
        y = y + wn[:, k:k + 1] * _unpack_rows(yga_ref[k], ygb_ref[k])
    o_ref[...] = r1_ref[...] + mod_ref[0, 5:6, :] * y


def _combine(r1, y_sh, yg_a, yg_b, w_nat, mod, s):
    t, d = r1.shape
    per_b = s // TT
    tok = lambda wd: pl.BlockSpec((TT, wd), lambda i: (i, 0))
    return pl.pallas_call(
        _combine_kernel,
        out_shape=jax.ShapeDtypeStruct((t, d), F32),
        grid=(t // TT,),
        in_specs=[tok(d), tok(d),
                  pl.BlockSpec((TOP_K, TT, d // 4), lambda i: (0, i, 0)),
                  pl.BlockSpec((TOP_K, TT, d // 4), lambda i: (0, i, 0)),
                  tok(TOP_K),
                  pl.BlockSpec((1, N_MOD, d), lambda i: (i // per_b, 0, 0))],
        out_specs=tok(d),
        compiler_params=_cparams(("parallel",)),
        name="routed_combine",
    )(r1, y_sh, yg_a, yg_b, w_nat, mod)


def _sc_mesh():
    return plsc.VectorSubcoreMesh(core_axis_name="c", subcore_axis_name="s")


def _sc_dispatch(xa, xb, pos_t, p):
    t, dc = xa.shape
    k = pos_t.shape[0]
    out = jax.ShapeDtypeStruct((p, dc), xa.dtype)

    @functools.partial(pl.kernel, out_type=(out, out), mesh=_sc_mesh())
    def kern(xa_hbm, xb_hbm, pos_hbm, oa_hbm, ob_hbm):
        for src, dst in ((xa_hbm, oa_hbm), (xb_hbm, ob_hbm)):
            def body(x_vmem, i_vmem, dst=dst):
                pltpu.sync_copy(x_vmem, dst.at[i_vmem.at[0]])

            pltpu.emit_pipeline(
                body,
                grid=(t // SC_WINDOW, k),
                in_specs=[pl.BlockSpec((SC_WINDOW, dc), lambda i, kk: (i, 0)),
                          pl.BlockSpec((1, SC_WINDOW), lambda i, kk: (kk, i))],
                out_specs=[],
                core_axis_name=("c", "s"),
                dimension_semantics=(pltpu.PARALLEL, pltpu.ARBITRARY),
            )(src, pos_hbm)

    return kern(xa, xb, pos_t)


def _sc_gather(ya, yb, pos_row):
    n = pos_row.shape[1]
    dc = ya.shape[1]
    out = jax.ShapeDtypeStruct((n, dc), ya.dtype)

    @functools.partial(pl.kernel, out_type=(out, out), mesh=_sc_mesh())
    def kern(ya_hbm, yb_hbm, pos_hbm, oa_hbm, ob_hbm):
        for src, dst in ((ya_hbm, oa_hbm), (yb_hbm, ob_hbm)):
            def body(i_vmem, o_vmem, src=src):
                pltpu.sync_copy(src.at[i_vmem.at[0]], o_vmem)

            pltpu.emit_pipeline(
                body,
                grid=(n // SC_WINDOW,),
                in_specs=[pl.BlockSpec((1, SC_WINDOW), lambda i: (0, i))],
                out_specs=[pl.BlockSpec((SC_WINDOW, dc), lambda i: (i, 0))],
                core_axis_name=("c", "s"),
                dimension_semantics=(pltpu.PARALLEL,),
            )(pos_hbm, dst)

    return kern(ya, yb, pos_row)


def kernel(x, c, positions, w_ada, b_ada, g_mix, w_in, g_q, g_k, lam, g_sub, w_out, g_ffn, w_router,
           e_bias, w_sh_gu, w_sh_down, w_gu, w_down):
    b, s, d = x.shape
    assert w_ada.shape[0] == 1 and s % TS == 0 and d % 256 == 0 and (b * s) % TT_SHARED == 0
    t = b * s
    wdt = w_in.shape[2] // 6

    c_pad = jnp.zeros((8, d), F32).at[:b].set(c.astype(F32))
    mod = _ada(c_pad, w_ada[0], b_ada)[:b].reshape(b, N_MOD, d)

    sec = [w_in[0][:, i * wdt:(i + 1) * wdt] for i in range(6)]
    w_t = jnp.concatenate([sec[0], sec[1], sec[2], sec[3], sec[5]], axis=1).T.astype(BF16)
    w_sk = sec[4].astype(BF16)
    inv = (ROPE_THETA ** (-jnp.arange(0, HEAD_DIM, 2, dtype=F32) / HEAD_DIM)).reshape(HALF, 1)
    pos3 = positions.astype(I32).reshape(b, 1, s)

    qd, kd, vd, qs, ks, vs = _inproj(x, mod, g_mix, pos3, inv, g_q.reshape(HEAD_DIM, 1),
                                     g_k.reshape(HEAD_DIM, 1), w_t, w_sk)
    mixed_d = _diff_attention(qd, kd, vd, lam[0], g_sub.reshape(2 * HEAD_DIM, 1))
    mixed_s = _sb_attention(qs, ks, vs)

    wr_t = w_router[0].T
    wr_hi = wr_t.astype(BF16)
    wr_lo = (wr_t - wr_hi.astype(F32)).astype(BF16)
    r1, h2a, h2b, idx_t, w_nat, rank_t, counts = _outproj_route(
        mixed_d.reshape(t, wdt), mixed_s.reshape(t, wdt), x.reshape(t, d), mod,
        w_out[0].astype(BF16), g_ffn, wr_hi, wr_lo, e_bias.reshape(N_EXPERTS, 1), s)

    counts = counts.reshape(N_EXPERTS)
    padded = (counts + STEP_ROWS - 1) // STEP_ROWS * STEP_ROWS
    pend = jnp.cumsum(padded)
    pstart = (pend - padded).astype(I32)
    n_steps = t * TOP_K // STEP_ROWS + N_EXPERTS
    step_row0 = jnp.arange(n_steps, dtype=pend.dtype) * STEP_ROWS
    block_e = jnp.minimum(jnp.sum(pend[None, :] <= step_row0[:, None], axis=1), N_EXPERTS - 1).astype(I32)
    rows_left = counts[block_e] - (step_row0 - pstart[block_e])
    n_valid = jnp.where(step_row0 < pend[-1],
                        jnp.clip((rows_left + EXPERT_ROWS - 1) // EXPERT_ROWS, 0, EXPERT_SUBS), 0).astype(I32)
    n_used = (pend[-1] // STEP_ROWS).astype(I32)
    pos_t = _slot_positions(idx_t, rank_t, pstart.reshape(N_EXPERTS, 1))

    xs_a, xs_b = _sc_dispatch(h2a, h2b, pos_t, n_steps * STEP_ROWS)
    y_a, y_b = _experts(block_e, n_valid, n_used, xs_a, xs_b, w_gu[0], w_down[0])
    y_sh = _shared(h2a, h2b, w_sh_gu[0].astype(BF16), w_sh_down[0].astype(BF16))
    yg_a, yg_b = _sc_gather(y_a, y_b, pos_t.reshape(1, TOP_K * t))
    dq4 = d // 4
    out = _combine(r1, y_sh, yg_a.reshape(TOP_K, t, dq4), yg_b.reshape(TOP_K, t, dq4), w_nat, mod, s)
    return out.reshape(b, s, d).astype(x.dtype)
```

```python
import functools
import math

import jax
import jax.numpy as jnp
from jax import lax
from jax.experimental import pallas as pl
from jax.experimental.pallas import tpu as pltpu
from jax.experimental.pallas import tpu_sc as plsc

F32 = jnp.float32
BF16 = jnp.bfloat16
U32 = jnp.uint32
I32 = jnp.int32

HEAD_DIM = 64
HALF = HEAD_DIM // 2
ROPE_THETA = 10000.0
N_EXPERTS = 256
TOP_K = 8
N_GROUPS = 8
GROUP_SIZE = N_EXPERTS // N_GROUPS
TOPK_GROUPS = 4
EXPERT_FF = 256
ROUTED_SCALE = 2.5
EXPERT_ROWS = 128
EXPERT_SUBS = 5
STEP_ROWS = EXPERT_SUBS * EXPERT_ROWS
NORM_EPS = 1e-6
SB_LOG_FLOOR = -110.0
NSTREAM = 4
N_MOD = 6
LAMBDA_INIT = 0.8 - 0.6 * math.exp(-0.3 * 0)
Q_SCALE = HEAD_DIM ** -0.5
LOG2E = math.log2(math.e)
SUM_ROWS = 16
DV_ROWS = 2 * HEAD_DIM + SUM_ROWS

TS = 1024
TB = 256
TT = 256
TT_COMBINE = 512
TT_SHARED = 1024
HI_MASK = 0xFFFF0000
SIGN_BIT = 0x80000000
SC_WINDOW = 128

VMEM_LIMIT = 48 * 1024 * 1024


def _cparams(sem):
    return pltpu.CompilerParams(dimension_semantics=sem, vmem_limit_bytes=VMEM_LIMIT)


def _split_bf16(a):
    hi = a.astype(BF16)
    lo = (a - hi.astype(F32)).astype(BF16)
    return hi, lo


def _dot(a, b):
    return jnp.dot(a, b, preferred_element_type=F32)


def _dot_nt(a, b):
    return lax.dot_general(a, b, (((1,), (1,)), ((), ())), preferred_element_type=F32)


def _silu(x):
    return x / (1.0 + jnp.exp(-x))


def _pack_bf16_pair(a, b):
    ab = pltpu.bitcast(a.astype(BF16).astype(F32), U32)
    bb = pltpu.bitcast(b.astype(BF16).astype(F32), U32)
    return (ab & jnp.uint32(HI_MASK)) | (bb >> 16)


def _unpack_bf16_pair(p):
    a = pltpu.bitcast(p & jnp.uint32(HI_MASK), F32)
    b = pltpu.bitcast(p << 16, F32)
    return a, b


def _pack_rows(v):
    q = v.shape[1] // 4
    return (_pack_bf16_pair(v[:, :q], v[:, q:2 * q]),
            _pack_bf16_pair(v[:, 2 * q:3 * q], v[:, 3 * q:]))


def _unpack_rows(pa, pb):
    return jnp.concatenate(_unpack_bf16_pair(pa) + _unpack_bf16_pair(pb), axis=1)


def _ada_kernel(c_ref, w_ref, b_ref, o_ref):
    cond = _silu(c_ref[...])
    ch, cl = _split_bf16(cond)
    wh, wl = _split_bf16(w_ref[...])
    o_ref[...] = _dot(ch, wh) + _dot(ch, wl) + _dot(cl, wh) + b_ref[...]


def _ada(c_pad, w_ada, b_ada):
    d, n = w_ada.shape
    tn = n // 4
    return pl.pallas_call(
        _ada_kernel,
        out_shape=jax.ShapeDtypeStruct((c_pad.shape[0], n), F32),
        grid=(n // tn,),
        in_specs=[pl.BlockSpec((c_pad.shape[0], d), lambda j: (0, 0)),
                  pl.BlockSpec((d, tn), lambda j: (0, j)),
                  pl.BlockSpec((1, tn), lambda j: (0, j))],
        out_specs=pl.BlockSpec((c_pad.shape[0], tn), lambda j: (0, j)),
        compiler_params=_cparams(("arbitrary",)),
        name="ada_mod",
    )(c_pad, w_ada, b_ada)


def _norm_rope_t(t, g_col, cos_t, sin_t):
    outs = []
    for i in range(t.shape[0] // HEAD_DIM):
        blk = t[i * HEAD_DIM:(i + 1) * HEAD_DIM, :]
        ms = jnp.mean(blk * blk, axis=0, keepdims=True)
        y = blk * lax.rsqrt(ms + NORM_EPS) * g_col
        y1 = y[:HALF]
        y2 = y[HALF:]
        outs.append(y1 * cos_t - y2 * sin_t)
        outs.append(y2 * cos_t + y1 * sin_t)
    return jnp.concatenate(outs, axis=0)


def _inproj_kernel(x_ref, mod_ref, g_ref, pos_ref, inv_ref, gq_ref, gk_ref, wt_ref, wsk_ref,
                   qd_ref, kd_ref, vd_ref, qs_ref, ks_ref, vs_ref):
    x = x_ref[0]
    ms = jnp.mean(x * x, axis=-1, keepdims=True)
    sh = mod_ref[0, 0:1, :]
    sc = mod_ref[0, 1:2, :]
    h = x * lax.rsqrt(ms + NORM_EPS) * g_ref[...] * (1.0 + sc) + sh
    hb = h.astype(BF16)
    w = qd_ref.shape[1]

    def proj_t(sec):
        return _dot_nt(wt_ref[sec * w:(sec + 1) * w, :], hb)

    ang = inv_ref[...] * pos_ref[0].astype(F32)
    cos_t = jnp.cos(ang)
    sin_t = jnp.sin(ang)
    q_t = _norm_rope_t(proj_t(0), gq_ref[...], cos_t, sin_t) * (Q_SCALE * LOG2E)
    qd_ref[0] = q_t.astype(BF16)
    k_t = _norm_rope_t(proj_t(1), gk_ref[...], cos_t, sin_t)
    kd_ref[0] = k_t.T.astype(BF16)
    v_t = proj_t(2).astype(BF16)
    s_t = proj_t(4).astype(BF16)
    nh = vd_ref.shape[1]
    hw = vs_ref.shape[3]
    ones = jnp.ones((SUM_ROWS, TB), BF16)
    for hh in range(nh):
        for jj in range(vd_ref.shape[2]):
            vd_ref[0, hh, jj] = jnp.concatenate(
                [v_t[hh * hw:(hh + 1) * hw, jj * TB:(jj + 1) * TB], ones], axis=0)
            vs_ref[0, hh, jj] = s_t[hh * hw:(hh + 1) * hw, jj * TB:(jj + 1) * TB]
    qs_ref[0] = (proj_t(3) * Q_SCALE).astype(BF16)
    ks_ref[0] = _dot(hb, wsk_ref[...]).astype(BF16)


def _inproj(x, mod, g_mix, pos3, inv, g_q, g_k, w_t, w_sk):
    b, s, d = x.shape
    w = w_sk.shape[1]
    nh = w // 128
    nsb = TS // TB
    const = lambda shape: pl.BlockSpec(shape, lambda bi, si: (0,) * len(shape))
    t_spec = pl.BlockSpec((1, w, TS), lambda bi, si: (bi, 0, si))
    n_spec = pl.BlockSpec((1, TS, w), lambda bi, si: (bi, si, 0))
    v_spec = lambda rows: pl.BlockSpec((1, nh, nsb, rows, TB), lambda bi, si: (bi, 0, si, 0, 0))
    t_shape = jax.ShapeDtypeStruct((b, w, s), BF16)
    n_shape = jax.ShapeDtypeStruct((b, s, w), BF16)
    v_shape = lambda rows: jax.ShapeDtypeStruct((b, nh, s // TB, rows, TB), BF16)
    return pl.pallas_call(
        _inproj_kernel,
        out_shape=(t_shape, n_shape, v_shape(DV_ROWS), t_shape, n_shape, v_shape(128)),
        grid=(b, s // TS),
        in_specs=[pl.BlockSpec((1, TS, d), lambda bi, si: (bi, si, 0)),
                  pl.BlockSpec((1, N_MOD, d), lambda bi, si: (bi, 0, 0)),
                  const((1, d)),
                  pl.BlockSpec((1, 1, TS), lambda bi, si: (bi, 0, si)),
                  const((HALF, 1)), const((HEAD_DIM, 1)), const((HEAD_DIM, 1)),
                  const(w_t.shape), const(w_sk.shape)],
        out_specs=(t_spec, n_spec, v_spec(DV_ROWS), t_spec, n_spec, v_spec(128)),
        compiler_params=_cparams(("parallel", "parallel")),
        name="norm_inproj",
    )(x, mod, g_mix, pos3, inv, g_q, g_k, w_t, w_sk)


def _q_pair(q_t):
    row = lax.broadcasted_iota(I32, q_t.shape, 0)
    zero = jnp.zeros_like(q_t)
    return jnp.concatenate([jnp.where(row < HEAD_DIM, q_t, zero),
                            jnp.where(row >= HEAD_DIM, q_t, zero)], axis=1)


def _key_query_iotas(tk, tq):
    r = lax.broadcasted_iota(I32, (tk, 2 * tq), 0)
    c = lax.broadcasted_iota(I32, (tk, 2 * tq), 1)
    return r, jnp.where(c >= tq, c - tq, c)


def _diff_kernel(qt_ref, k_ref, vt_ref, lam_ref, gsub_ref, o_ref, m_ref, acc_ref, s0, s1, p0, p1, a0, a1):
    qi = pl.program_id(2)
    tq = qt_ref.shape[2]
    dv = 2 * HEAD_DIM
    qcats = [_q_pair(qt_ref[0, st * 128:(st + 1) * 128, :]) for st in range(NSTREAM)]
    m_ref[...] = jnp.full(m_ref.shape, -jnp.inf, F32)
    acc_ref[...] = jnp.zeros(acc_ref.shape, F32)
    s_slots, p_slots, a_slots = (s0, s1), (p0, p1), (a0, a1)
    p_slots[1][...] = jnp.zeros(p_slots[1].shape, BF16)
    a_slots[1][...] = jnp.ones(a_slots[1].shape, F32)

    def scores(j, slot):
        row0 = pl.multiple_of(j * TB, TB)
        for st in range(NSTREAM):
            s_slots[slot][st] = _dot(k_ref[0, pl.ds(row0, TB), st * 128:(st + 1) * 128], qcats[st])

    def softmax(slot, diagonal):
        for st in range(NSTREAM):
            s = s_slots[slot][st]
            if diagonal:
                r, c = _key_query_iotas(TB, tq)
                s = jnp.where(r <= c, s, -jnp.inf)
            m = m_ref[st]
            m_new = jnp.maximum(m, jnp.max(s, axis=0, keepdims=True))
            a_slots[slot][st] = jnp.exp2(m - m_new)
            p_slots[slot][st] = jnp.exp2(s - m_new).astype(BF16)
            m_ref[st] = m_new

    def weighted_values(j, slot):
        for st in range(NSTREAM):
            acc_ref[st] = (acc_ref[st] * a_slots[slot][st]
                           + _dot(vt_ref[0, st, j], p_slots[slot][st]))

    def trip(j, slot):
        scores(j + 1, 1 - slot)
        softmax(slot, False)
        weighted_values(jnp.maximum(j - 1, 0), 1 - slot)

    def body(i, carry):
        trip(2 * i, 0)
        trip(2 * i + 1, 1)
        return carry

    scores(0, 0)
    lax.fori_loop(0, qi // 2, body, 0)

    @pl.when(qi % 2 == 1)
    def _():
        trip(qi - 1, 0)
        softmax(1, True)
        weighted_values(qi - 1, 0)
        weighted_values(qi, 1)

    @pl.when(qi % 2 == 0)
    def _():
        softmax(0, True)
        weighted_values(jnp.maximum(qi - 1, 0), 1)
        weighted_values(qi, 0)

    lam = lam_ref[...]
    l1 = jnp.sum(lam[0:1] * lam[1:2], axis=-1, keepdims=True)
    l2 = jnp.sum(lam[2:3] * lam[3:4], axis=-1, keepdims=True)
    lmbda = jnp.exp(l1) - jnp.exp(l2) + LAMBDA_INIT
    for st in range(NSTREAM):
        acc = acc_ref[st, :dv, :]
        l = acc_ref[st, dv:dv + 1, :]
        o = acc[:, :tq] / l[:, :tq] - lmbda * (acc[:, tq:] / l[:, tq:])
        ms = jnp.mean(o * o, axis=0, keepdims=True)
        y = o * lax.rsqrt(ms + NORM_EPS) * gsub_ref[...] * (1.0 - LAMBDA_INIT)
        o_ref[0, :, st * 128:(st + 1) * 128] = y.T.astype(BF16)


def _attention_specs(b, w, s, v_rows):
    gw = NSTREAM * 128
    in_specs = [pl.BlockSpec((1, gw, TB), lambda bi, gi, qi: (bi, gi, qi)),
                pl.BlockSpec((1, s, gw), lambda bi, gi, qi: (bi, 0, gi)),
                pl.BlockSpec((1, NSTREAM, s // TB, v_rows, TB), lambda bi, gi, qi: (bi, gi, 0, 0, 0))]
    out_spec = pl.BlockSpec((1, TB, gw), lambda bi, gi, qi: (bi, qi, gi))
    return (b, w // gw, s // TB), in_specs, out_spec


def _diff_attention(q_t, k, v_t, lam, g_sub):
    b, w, s = q_t.shape
    grid, in_specs, out_spec = _attention_specs(b, w, s, DV_ROWS)
    return pl.pallas_call(
        _diff_kernel,
        out_shape=jax.ShapeDtypeStruct((b, s, w), BF16),
        grid=grid,
        in_specs=in_specs + [pl.BlockSpec(lam.shape, lambda bi, gi, qi: (0, 0)),
                             pl.BlockSpec(g_sub.shape, lambda bi, gi, qi: (0, 0))],
        out_specs=out_spec,
        scratch_shapes=[pltpu.VMEM((NSTREAM, 1, 2 * TB), F32), pltpu.VMEM((NSTREAM, DV_ROWS, 2 * TB), F32),
                        pltpu.VMEM((NSTREAM, TB, 2 * TB), F32), pltpu.VMEM((NSTREAM, TB, 2 * TB), F32),
                        pltpu.VMEM((NSTREAM, TB, 2 * TB), BF16), pltpu.VMEM((NSTREAM, TB, 2 * TB), BF16),
                        pltpu.VMEM((NSTREAM, 1, 2 * TB), F32), pltpu.VMEM((NSTREAM, 1, 2 * TB), F32)],
        compiler_params=_cparams(("parallel", "parallel", "parallel")),
        name="diff_attention",
    )(q_t, k, v_t, lam, g_sub)


def _sb_kernel(qt_ref, k_ref, vt_ref, o_ref, c_ref, acc_ref):
    qi = pl.program_id(2)
    tq = qt_ref.shape[2]
    qcats = [_q_pair(qt_ref[0, st * 128:(st + 1) * 128, :]) for st in range(NSTREAM)]
    ur = lax.broadcasted_iota(I32, (TB, TB), 0)
    uc = lax.broadcasted_iota(I32, (TB, TB), 1)
    u_neg = jnp.where(uc > ur, -1.0, 0.0).astype(BF16)
    c_ref[...] = jnp.zeros(c_ref.shape, F32)
    acc_ref[...] = jnp.zeros(acc_ref.shape, F32)

    def step(j, diagonal):
        row0 = pl.multiple_of(j * TB, TB)
        if diagonal:
            r, c = _key_query_iotas(TB, tq)
            valid = r < c
        zs = [_dot(k_ref[0, pl.ds(row0, TB), st * 128:(st + 1) * 128], qcats[st])
              for st in range(NSTREAM)]
        pre = []
        for st in range(NSTREAM):
            z = zs[st]
            neg_abs = pltpu.bitcast(pltpu.bitcast(z, U32) | jnp.uint32(SIGN_BIT), F32)
            sp = jnp.maximum(z, 0.0) + jnp.log(1.0 + jnp.exp(neg_abs))
            drop = jnp.where(valid, sp, 0.0) if diagonal else sp
            pre.append((z - sp, drop.astype(BF16), drop[0:1, :]))
        laters = [_dot(u_neg, pre[st][1]) for st in range(NSTREAM)]
        for st in range(NSTREAM):
            c_run = c_ref[st]
            w = jnp.exp(pre[st][0] + laters[st] + c_run)
            if diagonal:
                w = jnp.where(valid, w, 0.0)
            wb = w.astype(BF16)
            vj = vt_ref[0, st, j]
            acc_ref[st] += jnp.concatenate([_dot(vj[:HEAD_DIM], wb[:, :tq]),
                                            _dot(vj[HEAD_DIM:], wb[:, tq:])], axis=0)
            c_ref[st] = c_run + laters[st][0:1, :] - pre[st][2]

    def live():
        return jnp.max(c_ref[...]) > SB_LOG_FLOOR

    step(qi, True)

    def body(i):
        step(qi - 1 - i, False)
        return i + 1

    lax.while_loop(lambda i: (i < qi) & live(), body, jnp.int32(0))
    for st in range(NSTREAM):
        o_ref[0, :, st * 128:(st + 1) * 128] = acc_ref[st].T.astype(BF16)


def _sb_attention(q_t, k, v_t):
    b, w, s = q_t.shape
    grid, in_specs, out_spec = _attention_specs(b, w, s, 2 * HEAD_DIM)
    return pl.pallas_call(
        _sb_kernel,
        out_shape=jax.ShapeDtypeStruct((b, s, w), BF16),
        grid=grid,
        in_specs=in_specs,
        out_specs=out_spec,
        scratch_shapes=[pltpu.VMEM((NSTREAM, 1, 2 * TB), F32), pltpu.VMEM((NSTREAM, 2 * HEAD_DIM, TB), F32)],
        compiler_params=_cparams(("parallel", "parallel", "parallel")),
        name="sb_attention",
    )(q_t, k, v_t)


def _first_index_of_max(v, idx, size):
    m = jnp.max(v, axis=0, keepdims=True)
    first = jnp.min(jnp.where(v == m, idx, size), axis=0, keepdims=True)
    return m, first


def _route_kernel(md_ref, ms_ref, x_ref, mod_ref, wo_ref, g_ref, wrh_ref, wrl_ref, eb_ref,
                  r1_ref, h2a_ref, h2b_ref, idx_ref, wn_ref, rank_ref, cnt_ref, base_ref):
    i = pl.program_id(0)

    @pl.when(i == 0)
    def _():
        base_ref[...] = jnp.zeros_like(base_ref)

    half = md_ref.shape[1]
    attn = _dot(md_ref[...], wo_ref[:half, :]) + _dot(ms_ref[...], wo_ref[half:, :])
    r1 = x_ref[...] + mod_ref[0, 2:3, :] * attn
    r1_ref[...] = r1
    ms = jnp.mean(r1 * r1, axis=-1, keepdims=True)
    h2 = r1 * lax.rsqrt(ms + NORM_EPS) * g_ref[...] * (1.0 + mod_ref[0, 4:5, :]) + mod_ref[0, 3:4, :]
    h2a_ref[...], h2b_ref[...] = _pack_rows(h2)

    hh, hl = _split_bf16(h2)
    wh = wrh_ref[...]
    logits = _dot_nt(wh, hh) + _dot_nt(wh, hl) + _dot_nt(wrl_ref[...], hh)
    scores = 1.0 / (1.0 + jnp.exp(-logits))
    biased = scores + eb_ref[...]
    tt = scores.shape[1]

    gi = lax.broadcasted_iota(I32, (GROUP_SIZE, tt), 0)
    gs = []
    for g in range(N_GROUPS):
        blk = biased[g * GROUP_SIZE:(g + 1) * GROUP_SIZE, :]
        m1, first = _first_index_of_max(blk, gi, GROUP_SIZE)
        m2 = jnp.max(jnp.where(gi == first, -jnp.inf, blk), axis=0, keepdims=True)
        gs.append(m1 + m2)
    gscore = jnp.concatenate(gs, axis=0)
    gidx = lax.broadcasted_iota(I32, (N_GROUPS, tt), 0)
    gsel = jnp.zeros((N_GROUPS, tt), F32)
    for _ in range(TOPK_GROUPS):
        _, first = _first_index_of_max(gscore, gidx, N_GROUPS)
        hit = gidx == first
        gsel = jnp.where(hit, 1.0, gsel)
        gscore = jnp.where(hit, -jnp.inf, gscore)
    cand = jnp.concatenate(
        [jnp.where(gsel[g:g + 1, :] > 0.0, biased[g * GROUP_SIZE:(g + 1) * GROUP_SIZE, :], -jnp.inf)
         for g in range(N_GROUPS)], axis=0)

    ei = lax.broadcasted_iota(I32, (N_EXPERTS, tt), 0)
    picked = jnp.zeros((N_EXPERTS, tt), F32)
    idxs, ws, hits = [], [], []
    for _ in range(TOP_K):
        _, first = _first_index_of_max(cand, ei, N_EXPERTS)
        hit = ei == first
        idxs.append(first)
        ws.append(jnp.sum(jnp.where(hit, scores, 0.0), axis=0, keepdims=True))
        hits.append(hit)
        picked = jnp.where(hit, 1.0, picked)
        cand = jnp.where(hit, -jnp.inf, cand)
    idx_ref[...] = jnp.concatenate(idxs, axis=0)
    w_t = jnp.concatenate(ws, axis=0)
    w_t = w_t / jnp.sum(w_t, axis=0, keepdims=True) * ROUTED_SCALE
    pad = jnp.zeros((128 - TOP_K, tt), F32)
    wn_ref[...] = jnp.concatenate([w_t, pad], axis=0).T[:, :TOP_K]

    tr = lax.broadcasted_iota(I32, (tt, tt), 0)
    tc = lax.broadcasted_iota(I32, (tt, tt), 1)
    before = jnp.where(tr < tc, 1.0, 0.0).astype(BF16)
    seen = _dot(picked.astype(BF16), before) + base_ref[...]
    rank_ref[...] = jnp.concatenate(
        [jnp.sum(jnp.where(h, seen, 0.0), axis=0, keepdims=True) for h in hits], axis=0).astype(I32)
    base_ref[...] += jnp.sum(picked, axis=1, keepdims=True)
    cnt_ref[...] = base_ref[...].astype(I32)


def _outproj_route(mixed_d, mixed_s, x2, mod, w_out, g_ffn, wr_hi, wr_lo, e_bias, s):
    t, d = x2.shape
    half = mixed_d.shape[1]
    per_b = s // TT
    const = lambda shape: pl.BlockSpec(shape, lambda i: (0,) * len(shape))
    tok = lambda wd: pl.BlockSpec((TT, wd), lambda i: (i, 0))
    lane = pl.BlockSpec((TOP_K, TT), lambda i: (0, i))
    return pl.pallas_call(
        _route_kernel,
        out_shape=(jax.ShapeDtypeStruct((t, d), F32),
                   jax.ShapeDtypeStruct((t, d // 4), U32),
                   jax.ShapeDtypeStruct((t, d // 4), U32),
                   jax.ShapeDtypeStruct((TOP_K, t), I32),
                   jax.ShapeDtypeStruct((t, TOP_K), F32),
                   jax.ShapeDtypeStruct((TOP_K, t), I32),
                   jax.ShapeDtypeStruct((N_EXPERTS, 1), I32)),
        grid=(t // TT,),
        in_specs=[tok(half), tok(half), tok(d),
                  pl.BlockSpec((1, N_MOD, d), lambda i: (i // per_b, 0, 0)),
                  const(w_out.shape), const((1, d)), const(wr_hi.shape), const(wr_lo.shape),
                  const((N_EXPERTS, 1))],
        out_specs=(tok(d), tok(d // 4), tok(d // 4), lane, tok(TOP_K), lane, const((N_EXPERTS, 1))),
        scratch_shapes=[pltpu.VMEM((N_EXPERTS, 1), F32)],
        compiler_params=_cparams(("arbitrary",)),
        name="outproj_route",
    )(mixed_d, mixed_s, x2, mod, w_out, g_ffn, wr_hi, wr_lo, e_bias)


def _pos_kernel(idx_ref, rank_ref, pstart_ref, pos_ref):
    tt = idx_ref.shape[1]
    ei = lax.broadcasted_iota(I32, (N_EXPERTS, tt), 0)
    ps = pstart_ref[...]
    rows = []
    for k in range(TOP_K):
        hit = ei == idx_ref[k:k + 1, :]
        rows.append(jnp.sum(jnp.where(hit, ps, 0), axis=0, keepdims=True))
    pos_ref[...] = jnp.concatenate(rows, axis=0) + rank_ref[...]


def _slot_positions(idx_t, rank_t, pstart):
    k, t = idx_t.shape
    tp = 1024
    lane = pl.BlockSpec((k, tp), lambda i: (0, i))
    return pl.pallas_call(
        _pos_kernel,
        out_shape=jax.ShapeDtypeStruct((k, t), I32),
        grid=(t // tp,),
        in_specs=[lane, lane, pl.BlockSpec((N_EXPERTS, 1), lambda i: (0, 0))],
        out_specs=lane,
        compiler_params=_cparams(("parallel",)),
        name="slot_positions",
    )(idx_t, rank_t, pstart)


def _expert_kernel(be_ref, nv_ref, xa_ref, xb_ref, wgu_ref, wd_ref, ya_ref, yb_ref, wgu_b, wd_b):
    b = pl.program_id(0)
    nv = nv_ref[b]
    prev = be_ref[jnp.maximum(b - 1, 0)]

    @pl.when((b == 0) | (be_ref[b] != prev))
    def _():
        wgu_b[...] = wgu_ref[0].astype(BF16)
        wd_b[...] = wd_ref[0].astype(BF16)

    for v in range(1, EXPERT_SUBS + 1):
        @pl.when(nv == v)
        def _(v=v):
            rows = v * EXPERT_ROWS
            x = _unpack_rows(xa_ref[:rows, :], xb_ref[:rows, :]).astype(BF16)
            gu = _dot(x, wgu_b[...])
            ff = gu.shape[1] // 2
            act = _silu(gu[:, :ff]) * gu[:, ff:]
            y = _dot(act.astype(BF16), wd_b[...])
            ya_ref[:rows, :], yb_ref[:rows, :] = _pack_rows(y)
            if rows < STEP_ROWS:
                ya_ref[rows:, :] = jnp.zeros((STEP_ROWS - rows, ya_ref.shape[1]), U32)
                yb_ref[rows:, :] = jnp.zeros((STEP_ROWS - rows, yb_ref.shape[1]), U32)


def _experts(block_e, n_valid, n_used, xs_a, xs_b, w_gu, w_down):
    p, dp = xs_a.shape
    _, d, f2 = w_gu.shape
    _, f, _ = w_down.shape
    row_spec = pl.BlockSpec((STEP_ROWS, dp), lambda b, be, nv: (b, 0))
    grid_spec = pltpu.PrefetchScalarGridSpec(
        num_scalar_prefetch=2,
        grid=(n_used,),
        in_specs=[row_spec, row_spec,
                  pl.BlockSpec((1, d, f2), lambda b, be, nv: (be[b], 0, 0)),
                  pl.BlockSpec((1, f, d), lambda b, be, nv: (be[b], 0, 0))],
        out_specs=(row_spec, row_spec),
        scratch_shapes=[pltpu.VMEM((d, f2), BF16), pltpu.VMEM((f, d), BF16)],
    )
    y_shape = jax.ShapeDtypeStruct((p, dp), U32)
    return pl.pallas_call(
        _expert_kernel,
        out_shape=(y_shape, y_shape),
        grid_spec=grid_spec,
        compiler_params=_cparams(("arbitrary",)),
        name="routed_experts",
    )(block_e, n_valid, xs_a, xs_b, w_gu, w_down)


def _shared_kernel(h2a_ref, h2b_ref, wsg_ref, wsd_ref, o_ref):
    h2 = _unpack_rows(h2a_ref[...], h2b_ref[...]).astype(BF16)
    gu = _dot(h2, wsg_ref[...])
    ff = gu.shape[1] // 2
    o_ref[...] = _dot((_silu(gu[:, :ff]) * gu[:, ff:]).astype(BF16), wsd_ref[...]).astype(BF16)


def _shared(h2a, h2b, w_sh_gu, w_sh_down):
    t, dq = h2a.shape
    d = w_sh_down.shape[1]
    const = lambda shape: pl.BlockSpec(shape, lambda i: (0,) * len(shape))
    tok = lambda wd: pl.BlockSpec((TT_SHARED, wd), lambda i: (i, 0))
    return pl.pallas_call(
        _shared_kernel,
        out_shape=jax.ShapeDtypeStruct((t, d), BF16),
        grid=(t // TT_SHARED,),
        in_specs=[tok(dq), tok(dq), const(w_sh_gu.shape), const(w_sh_down.shape)],
        out_specs=tok(d),
        compiler_params=_cparams(("parallel",)),
        name="shared_expert",
    )(h2a, h2b, w_sh_gu, w_sh_down)


def _combine_kernel(r1_ref, ysh_ref, yga_ref, ygb_ref, wn_ref, mod_ref, o_ref):
    wn = wn_ref[...]
    y = ysh_ref[...].astype(F32)
    for k in range(TOP_K):
        y = y + wn[:, k:k + 1] * _unpack_rows(yga_ref[k], ygb_ref[k])
    o_ref[...] = r1_ref[...] + mod_ref[0, 5:6, :] * y


def _combine(r1, y_sh, yg_a, yg_b, w_nat, mod, s):
    t, d = r1.shape
    per_b = s // TT_COMBINE
    tok = lambda wd: pl.BlockSpec((TT_COMBINE, wd), lambda i: (i, 0))
    return pl.pallas_call(
        _combine_kernel,
        out_shape=jax.ShapeDtypeStruct((t, d), F32),
        grid=(t // TT_COMBINE,),
        in_specs=[tok(d), tok(d),
                  pl.BlockSpec((TOP_K, TT_COMBINE, d // 4), lambda i: (0, i, 0)),
                  pl.BlockSpec((TOP_K, TT_COMBINE, d // 4), lambda i: (0, i, 0)),
                  tok(TOP_K),
                  pl.BlockSpec((1, N_MOD, d), lambda i: (i // per_b, 0, 0))],
        out_specs=tok(d),
        compiler_params=_cparams(("parallel",)),
        name="routed_combine",
    )(r1, y_sh, yg_a, yg_b, w_nat, mod)


def _sc_mesh():
    return plsc.VectorSubcoreMesh(core_axis_name="c", subcore_axis_name="s")


def _sc_dispatch(xa, xb, pos_t, p):
    t, dc = xa.shape
    k = pos_t.shape[0]
    out = jax.ShapeDtypeStruct((p, dc), xa.dtype)

    @functools.partial(pl.kernel, out_type=(out, out), mesh=_sc_mesh())
    def kern(xa_hbm, xb_hbm, pos_hbm, oa_hbm, ob_hbm):
        for src, dst in ((xa_hbm, oa_hbm), (xb_hbm, ob_hbm)):
            def body(x_vmem, i_vmem, dst=dst):
                pltpu.sync_copy(x_vmem, dst.at[i_vmem.at[0]])

            pltpu.emit_pipeline(
                body,
                grid=(t // SC_WINDOW, k),
                in_specs=[pl.BlockSpec((SC_WINDOW, dc), lambda i, kk: (i, 0)),
                          pl.BlockSpec((1, SC_WINDOW), lambda i, kk: (kk, i))],
                out_specs=[],
                core_axis_name=("c", "s"),
                dimension_semantics=(pltpu.PARALLEL, pltpu.ARBITRARY),
            )(src, pos_hbm)

    return kern(xa, xb, pos_t)


def _sc_gather(ya, yb, pos_row):
    n = pos_row.shape[1]
    dc = ya.shape[1]
    out = jax.ShapeDtypeStruct((n, dc), ya.dtype)

    @functools.partial(pl.kernel, out_type=(out, out), mesh=_sc_mesh())
    def kern(ya_hbm, yb_hbm, pos_hbm, oa_hbm, ob_hbm):
        for src, dst in ((ya_hbm, oa_hbm), (yb_hbm, ob_hbm)):
            def body(i_vmem, o_vmem, src=src):
                pltpu.sync_copy(src.at[i_vmem.at[0]], o_vmem)

            pltpu.emit_pipeline(
                body,
                grid=(n // SC_WINDOW,),
                in_specs=[pl.BlockSpec((1, SC_WINDOW), lambda i: (0, i))],
                out_specs=[pl.BlockSpec((SC_WINDOW, dc), lambda i: (i, 0))],
                core_axis_name=("c", "s"),
                dimension_semantics=(pltpu.PARALLEL,),
            )(pos_hbm, dst)

    return kern(ya, yb, pos_row)


def kernel(x, c, positions, w_ada, b_ada, g_mix, w_in, g_q, g_k, lam, g_sub, w_out, g_ffn, w_router,
           e_bias, w_sh_gu, w_sh_down, w_gu, w_down):
    b, s, d = x.shape
    assert w_ada.shape[0] == 1 and s % TS == 0 and d % 256 == 0 and (b * s) % TT_SHARED == 0
    t = b * s
    wdt = w_in.shape[2] // 6

    c_pad = jnp.zeros((8, d), F32).at[:b].set(c.astype(F32))
    mod = _ada(c_pad, w_ada[0], b_ada)[:b].reshape(b, N_MOD, d)

    sec = [w_in[0][:, i * wdt:(i + 1) * wdt] for i in range(6)]
    w_t = jnp.concatenate([sec[0], sec[1], sec[2], sec[3], sec[5]], axis=1).T.astype(BF16)
    w_sk = sec[4].astype(BF16)
    inv = (ROPE_THETA ** (-jnp.arange(0, HEAD_DIM, 2, dtype=F32) / HEAD_DIM)).reshape(HALF, 1)
    pos3 = positions.astype(I32).reshape(b, 1, s)

    qd, kd, vd, qs, ks, vs = _inproj(x, mod, g_mix, pos3, inv, g_q.reshape(HEAD_DIM, 1),
                                     g_k.reshape(HEAD_DIM, 1), w_t, w_sk)
    mixed_d = _diff_attention(qd, kd, vd, lam[0], g_sub.reshape(2 * HEAD_DIM, 1))
    mixed_s = _sb_attention(qs, ks, vs)

    wr_t = w_router[0].T
    wr_hi = wr_t.astype(BF16)
    wr_lo = (wr_t - wr_hi.astype(F32)).astype(BF16)
    r1, h2a, h2b, idx_t, w_nat, rank_t, counts = _outproj_route(
        mixed_d.reshape(t, wdt), mixed_s.reshape(t, wdt), x.reshape(t, d), mod,
        w_out[0].astype(BF16), g_ffn, wr_hi, wr_lo, e_bias.reshape(N_EXPERTS, 1), s)

    counts = counts.reshape(N_EXPERTS)
    padded = (counts + STEP_ROWS - 1) // STEP_ROWS * STEP_ROWS
    pend = jnp.cumsum(padded)
    pstart = (pend - padded).astype(I32)
    n_steps = t * TOP_K // STEP_ROWS + N_EXPERTS
    step_row0 = jnp.arange(n_steps, dtype=pend.dtype) * STEP_ROWS
    block_e = jnp.minimum(jnp.sum(pend[None, :] <= step_row0[:, None], axis=1), N_EXPERTS - 1).astype(I32)
    rows_left = counts[block_e] - (step_row0 - pstart[block_e])
    n_valid = jnp.where(step_row0 < pend[-1],
                        jnp.clip((rows_left + EXPERT_ROWS - 1) // EXPERT_ROWS, 0, EXPERT_SUBS), 0).astype(I32)
    n_used = (pend[-1] // STEP_ROWS).astype(I32)
    pos_t = _slot_positions(idx_t, rank_t, pstart.reshape(N_EXPERTS, 1))

    xs_a, xs_b = _sc_dispatch(h2a, h2b, pos_t, n_steps * STEP_ROWS)
    y_a, y_b = _experts(block_e, n_valid, n_used, xs_a, xs_b, w_gu[0], w_down[0])
    y_sh = _shared(h2a, h2b, w_sh_gu[0].astype(BF16), w_sh_down[0].astype(BF16))
    yg_a, yg_b = _sc_gather(y_a, y_b, pos_t.reshape(1, TOP_K * t))
    dq4 = d // 4
    out = _combine(r1, y_sh, yg_a.reshape(TOP_K, t, dq4), yg_b.reshape(TOP_K, t, dq4), w_nat, mod, s)
    return out.reshape(b, s, d).astype(x.dtype)
```

```python
import functools
import math

import jax
import jax.numpy as jnp
from jax import lax
from jax.experimental import pallas as pl
from jax.experimental.pallas import tpu as pltpu
from jax.experimental.pallas import tpu_sc as plsc

F32 = jnp.float32
BF16 = jnp.bfloat16
U32 = jnp.uint32
I32 = jnp.int32

HEAD_DIM = 64
HALF = HEAD_DIM // 2
ROPE_THETA = 10000.0
N_EXPERTS = 256
TOP_K = 8
N_GROUPS = 8
GROUP_SIZE = N_EXPERTS // N_GROUPS
TOPK_GROUPS = 4
EXPERT_FF = 256
ROUTED_SCALE = 2.5
EXPERT_ROWS = 128
EXPERT_SUBS = 5
STEP_ROWS = EXPERT_SUBS * EXPERT_ROWS
NORM_EPS = 1e-6
SB_LOG_FLOOR = -110.0
NSTREAM = 4
N_MOD = 6
LAMBDA_INIT = 0.8 - 0.6 * math.exp(-0.3 * 0)
Q_SCALE = HEAD_DIM ** -0.5
LOG2E = math.log2(math.e)
SUM_ROWS = 16
DV_ROWS = 2 * HEAD_DIM + SUM_ROWS

TS = 1024
TB = 256
TT = 1024
TT_COMBINE = 512
TT_SHARED = 2048
HI_MASK = 0xFFFF0000
SIGN_BIT = 0x80000000
SC_WINDOW = 128

VMEM_LIMIT = 48 * 1024 * 1024


def _cparams(sem):
    return pltpu.CompilerParams(dimension_semantics=sem, vmem_limit_bytes=VMEM_LIMIT)


def _split_bf16(a):
    hi = a.astype(BF16)
    lo = (a - hi.astype(F32)).astype(BF16)
    return hi, lo


def _dot(a, b):
    return jnp.dot(a, b, preferred_element_type=F32)


def _dot_nt(a, b):
    return lax.dot_general(a, b, (((1,), (1,)), ((), ())), preferred_element_type=F32)


def _silu(x):
    return x / (1.0 + jnp.exp(-x))


def _pack_bf16_pair(a, b):
    ab = pltpu.bitcast(a.astype(BF16).astype(F32), U32)
    bb = pltpu.bitcast(b.astype(BF16).astype(F32), U32)
    return (ab & jnp.uint32(HI_MASK)) | (bb >> 16)


def _unpack_bf16_pair(p):
    a = pltpu.bitcast(p & jnp.uint32(HI_MASK), F32)
    b = pltpu.bitcast(p << 16, F32)
    return a, b


def _pack_rows(v):
    q = v.shape[1] // 4
    return (_pack_bf16_pair(v[:, :q], v[:, q:2 * q]),
            _pack_bf16_pair(v[:, 2 * q:3 * q], v[:, 3 * q:]))


def _unpack_rows(pa, pb):
    return jnp.concatenate(_unpack_bf16_pair(pa) + _unpack_bf16_pair(pb), axis=1)


def _ada_kernel(c_ref, w_ref, b_ref, o_ref):
    cond = _silu(c_ref[...])
    ch, cl = _split_bf16(cond)
    wh, wl = _split_bf16(w_ref[...])
    o_ref[...] = _dot(ch, wh) + _dot(ch, wl) + _dot(cl, wh) + b_ref[...]


def _ada(c_pad, w_ada, b_ada):
    d, n = w_ada.shape
    tn = n // 4
    return pl.pallas_call(
        _ada_kernel,
        out_shape=jax.ShapeDtypeStruct((c_pad.shape[0], n), F32),
        grid=(n // tn,),
        in_specs=[pl.BlockSpec((c_pad.shape[0], d), lambda j: (0, 0)),
                  pl.BlockSpec((d, tn), lambda j: (0, j)),
                  pl.BlockSpec((1, tn), lambda j: (0, j))],
        out_specs=pl.BlockSpec((c_pad.shape[0], tn), lambda j: (0, j)),
        compiler_params=_cparams(("arbitrary",)),
        name="ada_mod",
    )(c_pad, w_ada, b_ada)


def _norm_rope_t(t, g_col, cos_t, sin_t):
    outs = []
    for i in range(t.shape[0] // HEAD_DIM):
        blk = t[i * HEAD_DIM:(i + 1) * HEAD_DIM, :]
        ms = jnp.mean(blk * blk, axis=0, keepdims=True)
        y = blk * lax.rsqrt(ms + NORM_EPS) * g_col
        y1 = y[:HALF]
        y2 = y[HALF:]
        outs.append(y1 * cos_t - y2 * sin_t)
        outs.append(y2 * cos_t + y1 * sin_t)
    return jnp.concatenate(outs, axis=0)


def _inproj_kernel(x_ref, mod_ref, g_ref, pos_ref, inv_ref, gq_ref, gk_ref, wt_ref, wsk_ref,
                   qd_ref, kd_ref, vd_ref, qs_ref, ks_ref, vs_ref):
    x = x_ref[0]
    ms = jnp.mean(x * x, axis=-1, keepdims=True)
    sh = mod_ref[0, 0:1, :]
    sc = mod_ref[0, 1:2, :]
    h = x * lax.rsqrt(ms + NORM_EPS) * g_ref[...] * (1.0 + sc) + sh
    hb = h.astype(BF16)
    w = qd_ref.shape[1]

    def proj_t(sec):
        return _dot_nt(wt_ref[sec * w:(sec + 1) * w, :], hb)

    ang = inv_ref[...] * pos_ref[0].astype(F32)
    cos_t = jnp.cos(ang)
    sin_t = jnp.sin(ang)
    q_t = _norm_rope_t(proj_t(0), gq_ref[...], cos_t, sin_t) * (Q_SCALE * LOG2E)
    qd_ref[0] = q_t.astype(BF16)
    k_t = _norm_rope_t(proj_t(1), gk_ref[...], cos_t, sin_t)
    kd_ref[0] = k_t.T.astype(BF16)
    v_t = proj_t(2).astype(BF16)
    s_t = proj_t(4).astype(BF16)
    nh = vd_ref.shape[1]
    hw = vs_ref.shape[3]
    ones = jnp.ones((SUM_ROWS, TB), BF16)
    for hh in range(nh):
        for jj in range(vd_ref.shape[2]):
            vd_ref[0, hh, jj] = jnp.concatenate(
                [v_t[hh * hw:(hh + 1) * hw, jj * TB:(jj + 1) * TB], ones], axis=0)
            vs_ref[0, hh, jj] = s_t[hh * hw:(hh + 1) * hw, jj * TB:(jj + 1) * TB]
    qs_ref[0] = (proj_t(3) * Q_SCALE).astype(BF16)
    ks_ref[0] = _dot(hb, wsk_ref[...]).astype(BF16)


def _inproj(x, mod, g_mix, pos3, inv, g_q, g_k, w_t, w_sk):
    b, s, d = x.shape
    w = w_sk.shape[1]
    nh = w // 128
    nsb = TS // TB
    const = lambda shape: pl.BlockSpec(shape, lambda bi, si: (0,) * len(shape))
    t_spec = pl.BlockSpec((1, w, TS), lambda bi, si: (bi, 0, si))
    n_spec = pl.BlockSpec((1, TS, w), lambda bi, si: (bi, si, 0))
    v_spec = lambda rows: pl.BlockSpec((1, nh, nsb, rows, TB), lambda bi, si: (bi, 0, si, 0, 0))
    t_shape = jax.ShapeDtypeStruct((b, w, s), BF16)
    n_shape = jax.ShapeDtypeStruct((b, s, w), BF16)
    v_shape = lambda rows: jax.ShapeDtypeStruct((b, nh, s // TB, rows, TB), BF16)
    return pl.pallas_call(
        _inproj_kernel,
        out_shape=(t_shape, n_shape, v_shape(DV_ROWS), t_shape, n_shape, v_shape(128)),
        grid=(b, s // TS),
        in_specs=[pl.BlockSpec((1, TS, d), lambda bi, si: (bi, si, 0)),
                  pl.BlockSpec((1, N_MOD, d), lambda bi, si: (bi, 0, 0)),
                  const((1, d)),
                  pl.BlockSpec((1, 1, TS), lambda bi, si: (bi, 0, si)),
                  const((HALF, 1)), const((HEAD_DIM, 1)), const((HEAD_DIM, 1)),
                  const(w_t.shape), const(w_sk.shape)],
        out_specs=(t_spec, n_spec, v_spec(DV_ROWS), t_spec, n_spec, v_spec(128)),
        compiler_params=_cparams(("parallel", "parallel")),
        name="norm_inproj",
    )(x, mod, g_mix, pos3, inv, g_q, g_k, w_t, w_sk)


def _q_pair(q_t):
    row = lax.broadcasted_iota(I32, q_t.shape, 0)
    zero = jnp.zeros_like(q_t)
    return jnp.concatenate([jnp.where(row < HEAD_DIM, q_t, zero),
                            jnp.where(row >= HEAD_DIM, q_t, zero)], axis=1)


def _key_query_iotas(tk, tq):
    r = lax.broadcasted_iota(I32, (tk, 2 * tq), 0)
    c = lax.broadcasted_iota(I32, (tk, 2 * tq), 1)
    return r, jnp.where(c >= tq, c - tq, c)


def _diff_kernel(qt_ref, k_ref, vt_ref, lam_ref, gsub_ref, o_ref, m_ref, acc_ref, s0, s1, p0, p1, a0, a1):
    qi = pl.program_id(2)
    tq = qt_ref.shape[2]
    dv = 2 * HEAD_DIM
    qcats = [_q_pair(qt_ref[0, st * 128:(st + 1) * 128, :]) for st in range(NSTREAM)]
    m_ref[...] = jnp.full(m_ref.shape, -jnp.inf, F32)
    acc_ref[...] = jnp.zeros(acc_ref.shape, F32)
    s_slots, p_slots, a_slots = (s0, s1), (p0, p1), (a0, a1)
    p_slots[1][...] = jnp.zeros(p_slots[1].shape, BF16)
    a_slots[1][...] = jnp.ones(a_slots[1].shape, F32)

    def scores(j, slot):
        row0 = pl.multiple_of(j * TB, TB)
        for st in range(NSTREAM):
            s_slots[slot][st] = _dot(k_ref[0, pl.ds(row0, TB), st * 128:(st + 1) * 128], qcats[st])

    def softmax(slot, diagonal):
        for st in range(NSTREAM):
            s = s_slots[slot][st]
            if diagonal:
                r, c = _key_query_iotas(TB, tq)
                s = jnp.where(r <= c, s, -jnp.inf)
            m = m_ref[st]
            m_new = jnp.maximum(m, jnp.max(s, axis=0, keepdims=True))
            a_slots[slot][st] = jnp.exp2(m - m_new)
            p_slots[slot][st] = jnp.exp2(s - m_new).astype(BF16)
            m_ref[st] = m_new

    def weighted_values(j, slot):
        for st in range(NSTREAM):
            acc_ref[st] = (acc_ref[st] * a_slots[slot][st]
                           + _dot(vt_ref[0, st, j], p_slots[slot][st]))

    def trip(j, slot):
        scores(j + 1, 1 - slot)
        softmax(slot, False)
        weighted_values(jnp.maximum(j - 1, 0), 1 - slot)

    def body(i, carry):
        trip(2 * i, 0)
        trip(2 * i + 1, 1)
        return carry

    scores(0, 0)
    lax.fori_loop(0, qi // 2, body, 0)

    @pl.when(qi % 2 == 1)
    def _():
        trip(qi - 1, 0)
        softmax(1, True)
        weighted_values(qi - 1, 0)
        weighted_values(qi, 1)

    @pl.when(qi % 2 == 0)
    def _():
        softmax(0, True)
        weighted_values(jnp.maximum(qi - 1, 0), 1)
        weighted_values(qi, 0)

    lam = lam_ref[...]
    l1 = jnp.sum(lam[0:1] * lam[1:2], axis=-1, keepdims=True)
    l2 = jnp.sum(lam[2:3] * lam[3:4], axis=-1, keepdims=True)
    lmbda = jnp.exp(l1) - jnp.exp(l2) + LAMBDA_INIT
    for st in range(NSTREAM):
        acc = acc_ref[st, :dv, :]
        l = acc_ref[st, dv:dv + 1, :]
        o = acc[:, :tq] / l[:, :tq] - lmbda * (acc[:, tq:] / l[:, tq:])
        ms = jnp.mean(o * o, axis=0, keepdims=True)
        y = o * lax.rsqrt(ms + NORM_EPS) * gsub_ref[...] * (1.0 - LAMBDA_INIT)
        o_ref[0, :, st * 128:(st + 1) * 128] = y.T.astype(BF16)


def _attention_specs(b, w, s, v_rows):
    gw = NSTREAM * 128
    in_specs = [pl.BlockSpec((1, gw, TB), lambda bi, gi, qi: (bi, gi, qi)),
                pl.BlockSpec((1, s, gw), lambda bi, gi, qi: (bi, 0, gi)),
                pl.BlockSpec((1, NSTREAM, s // TB, v_rows, TB), lambda bi, gi, qi: (bi, gi, 0, 0, 0))]
    out_spec = pl.BlockSpec((1, TB, gw), lambda bi, gi, qi: (bi, qi, gi))
    return (b, w // gw, s // TB), in_specs, out_spec


def _diff_attention(q_t, k, v_t, lam, g_sub):
    b, w, s = q_t.shape
    grid, in_specs, out_spec = _attention_specs(b, w, s, DV_ROWS)
    return pl.pallas_call(
        _diff_kernel,
        out_shape=jax.ShapeDtypeStruct((b, s, w), BF16),
        grid=grid,
        in_specs=in_specs + [pl.BlockSpec(lam.shape, lambda bi, gi, qi: (0, 0)),
                             pl.BlockSpec(g_sub.shape, lambda bi, gi, qi: (0, 0))],
        out_specs=out_spec,
        scratch_shapes=[pltpu.VMEM((NSTREAM, 1, 2 * TB), F32), pltpu.VMEM((NSTREAM, DV_ROWS, 2 * TB), F32),
                        pltpu.VMEM((NSTREAM, TB, 2 * TB), F32), pltpu.VMEM((NSTREAM, TB, 2 * TB), F32),
                        pltpu.VMEM((NSTREAM, TB, 2 * TB), BF16), pltpu.VMEM((NSTREAM, TB, 2 * TB), BF16),
                        pltpu.VMEM((NSTREAM, 1, 2 * TB), F32), pltpu.VMEM((NSTREAM, 1, 2 * TB), F32)],
        compiler_params=_cparams(("parallel", "parallel", "parallel")),
        name="diff_attention",
    )(q_t, k, v_t, lam, g_sub)


def _sb_kernel(qt_ref, k_ref, vt_ref, o_ref, c_ref, acc_ref):
    qi = pl.program_id(2)
    tq = qt_ref.shape[2]
    qcats = [_q_pair(qt_ref[0, st * 128:(st + 1) * 128, :]) for st in range(NSTREAM)]
    ur = lax.broadcasted_iota(I32, (TB, TB), 0)
    uc = lax.broadcasted_iota(I32, (TB, TB), 1)
    u_neg = jnp.where(uc > ur, -1.0, 0.0).astype(BF16)
    c_ref[...] = jnp.zeros(c_ref.shape, F32)
    acc_ref[...] = jnp.zeros(acc_ref.shape, F32)

    def step(j, diagonal):
        row0 = pl.multiple_of(j * TB, TB)
        if diagonal:
            r, c = _key_query_iotas(TB, tq)
            valid = r < c
        zs = [_dot(k_ref[0, pl.ds(row0, TB), st * 128:(st + 1) * 128], qcats[st])
              for st in range(NSTREAM)]
        pre = []
        for st in range(NSTREAM):
            z = zs[st]
            neg_abs = pltpu.bitcast(pltpu.bitcast(z, U32) | jnp.uint32(SIGN_BIT), F32)
            sp = jnp.maximum(z, 0.0) + jnp.log(1.0 + jnp.exp(neg_abs))
            drop = jnp.where(valid, sp, 0.0) if diagonal else sp
            pre.append((z - sp, drop.astype(BF16), drop[0:1, :]))
        laters = [_dot(u_neg, pre[st][1]) for st in range(NSTREAM)]
        for st in range(NSTREAM):
            c_run = c_ref[st]
            w = jnp.exp(pre[st][0] + laters[st] + c_run)
            if diagonal:
                w = jnp.where(valid, w, 0.0)
            wb = w.astype(BF16)
            vj = vt_ref[0, st, j]
            acc_ref[st] += jnp.concatenate([_dot(vj[:HEAD_DIM], wb[:, :tq]),
                                            _dot(vj[HEAD_DIM:], wb[:, tq:])], axis=0)
            c_ref[st] = c_run + laters[st][0:1, :] - pre[st][2]

    def live():
        return jnp.max(c_ref[...]) > SB_LOG_FLOOR

    step(qi, True)

    def body(i):
        step(qi - 1 - i, False)
        return i + 1

    lax.while_loop(lambda i: (i < qi) & live(), body, jnp.int32(0))
    for st in range(NSTREAM):
        o_ref[0, :, st * 128:(st + 1) * 128] = acc_ref[st].T.astype(BF16)


def _sb_attention(q_t, k, v_t):
    b, w, s = q_t.shape
    grid, in_specs, out_spec = _attention_specs(b, w, s, 2 * HEAD_DIM)
    return pl.pallas_call(
        _sb_kernel,
        out_shape=jax.ShapeDtypeStruct((b, s, w), BF16),
        grid=grid,
        in_specs=in_specs,
        out_specs=out_spec,
        scratch_shapes=[pltpu.VMEM((NSTREAM, 1, 2 * TB), F32), pltpu.VMEM((NSTREAM, 2 * HEAD_DIM, TB), F32)],
        compiler_params=_cparams(("parallel", "parallel", "parallel")),
        name="sb_attention",
    )(q_t, k, v_t)


def _first_index_of_max(v, idx, size):
    m = jnp.max(v, axis=0, keepdims=True)
    first = jnp.min(jnp.where(v == m, idx, size), axis=0, keepdims=True)
    return m, first


def _route_kernel(md_ref, ms_ref, x_ref, mod_ref, wo_ref, g_ref, wrh_ref, wrl_ref, eb_ref,
                  r1_ref, h2a_ref, h2b_ref, idx_ref, wn_ref, rank_ref, cnt_ref, base_ref):
    i = pl.program_id(0)

    @pl.when(i == 0)
    def _():
        base_ref[...] = jnp.zeros_like(base_ref)

    half = md_ref.shape[1]
    attn = _dot(md_ref[...], wo_ref[:half, :]) + _dot(ms_ref[...], wo_ref[half:, :])
    r1 = x_ref[...] + mod_ref[0, 2:3, :] * attn
    r1_ref[...] = r1
    ms = jnp.mean(r1 * r1, axis=-1, keepdims=True)
    h2 = r1 * lax.rsqrt(ms + NORM_EPS) * g_ref[...] * (1.0 + mod_ref[0, 4:5, :]) + mod_ref[0, 3:4, :]
    h2a_ref[...], h2b_ref[...] = _pack_rows(h2)

    hh, hl = _split_bf16(h2)
    wh = wrh_ref[...]
    logits = _dot_nt(wh, hh) + _dot_nt(wh, hl) + _dot_nt(wrl_ref[...], hh)
    scores = 1.0 / (1.0 + jnp.exp(-logits))
    biased = scores + eb_ref[...]
    tt = scores.shape[1]

    gi = lax.broadcasted_iota(I32, (GROUP_SIZE, tt), 0)
    gs = []
    for g in range(N_GROUPS):
        blk = biased[g * GROUP_SIZE:(g + 1) * GROUP_SIZE, :]
        m1, first = _first_index_of_max(blk, gi, GROUP_SIZE)
        m2 = jnp.max(jnp.where(gi == first, -jnp.inf, blk), axis=0, keepdims=True)
        gs.append(m1 + m2)
    gscore = jnp.concatenate(gs, axis=0)
    gidx = lax.broadcasted_iota(I32, (N_GROUPS, tt), 0)
    gsel = jnp.zeros((N_GROUPS, tt), F32)
    for _ in range(TOPK_GROUPS):
        _, first = _first_index_of_max(gscore, gidx, N_GROUPS)
        hit = gidx == first
        gsel = jnp.where(hit, 1.0, gsel)
        gscore = jnp.where(hit, -jnp.inf, gscore)
    cand = jnp.concatenate(
        [jnp.where(gsel[g:g + 1, :] > 0.0, biased[g * GROUP_SIZE:(g + 1) * GROUP_SIZE, :], -jnp.inf)
         for g in range(N_GROUPS)], axis=0)

    ei = lax.broadcasted_iota(I32, (N_EXPERTS, tt), 0)
    picked = jnp.zeros((N_EXPERTS, tt), F32)
    idxs, ws, hits = [], [], []
    for _ in range(TOP_K):
        _, first = _first_index_of_max(cand, ei, N_EXPERTS)
        hit = ei == first
        idxs.append(first)
        ws.append(jnp.sum(jnp.where(hit, scores, 0.0), axis=0, keepdims=True))
        hits.append(hit)
        picked = jnp.where(hit, 1.0, picked)
        cand = jnp.where(hit, -jnp.inf, cand)
    idx_ref[...] = jnp.concatenate(idxs, axis=0)
    w_t = jnp.concatenate(ws, axis=0)
    w_t = w_t / jnp.sum(w_t, axis=0, keepdims=True) * ROUTED_SCALE
    pad = jnp.zeros((128 - TOP_K, tt), F32)
    wn_ref[...] = jnp.concatenate([w_t, pad], axis=0).T[:, :TOP_K]

    tr = lax.broadcasted_iota(I32, (tt, tt), 0)
    tc = lax.broadcasted_iota(I32, (tt, tt), 1)
    before = jnp.where(tr < tc, 1.0, 0.0).astype(BF16)
    seen = _dot(picked.astype(BF16), before) + base_ref[...]
    rank_ref[...] = jnp.concatenate(
        [jnp.sum(jnp.where(h, seen, 0.0), axis=0, keepdims=True) for h in hits], axis=0).astype(I32)
    base_ref[...] += jnp.sum(picked, axis=1, keepdims=True)
    cnt_ref[...] = base_ref[...].astype(I32)


def _outproj_route(mixed_d, mixed_s, x2, mod, w_out, g_ffn, wr_hi, wr_lo, e_bias, s):
    t, d = x2.shape
    half = mixed_d.shape[1]
    per_b = s // TT
    const = lambda shape: pl.BlockSpec(shape, lambda i: (0,) * len(shape))
    tok = lambda wd: pl.BlockSpec((TT, wd), lambda i: (i, 0))
    lane = pl.BlockSpec((TOP_K, TT), lambda i: (0, i))
    return pl.pallas_call(
        _route_kernel,
        out_shape=(jax.ShapeDtypeStruct((t, d), F32),
                   jax.ShapeDtypeStruct((t, d // 4), U32),
                   jax.ShapeDtypeStruct((t, d // 4), U32),
                   jax.ShapeDtypeStruct((TOP_K, t), I32),
                   jax.ShapeDtypeStruct((t, TOP_K), F32),
                   jax.ShapeDtypeStruct((TOP_K, t), I32),
                   jax.ShapeDtypeStruct((N_EXPERTS, 1), I32)),
        grid=(t // TT,),
        in_specs=[tok(half), tok(half), tok(d),
                  pl.BlockSpec((1, N_MOD, d), lambda i: (i // per_b, 0, 0)),
                  const(w_out.shape), const((1, d)), const(wr_hi.shape), const(wr_lo.shape),
                  const((N_EXPERTS, 1))],
        out_specs=(tok(d), tok(d // 4), tok(d // 4), lane, tok(TOP_K), lane, const((N_EXPERTS, 1))),
        scratch_shapes=[pltpu.VMEM((N_EXPERTS, 1), F32)],
        compiler_params=_cparams(("arbitrary",)),
        name="outproj_route",
    )(mixed_d, mixed_s, x2, mod, w_out, g_ffn, wr_hi, wr_lo, e_bias)


def _pos_kernel(idx_ref, rank_ref, pstart_ref, pos_ref):
    tt = idx_ref.shape[1]
    ei = lax.broadcasted_iota(I32, (N_EXPERTS, tt), 0)
    ps = pstart_ref[...]
    rows = []
    for k in range(TOP_K):
        hit = ei == idx_ref[k:k + 1, :]
        rows.append(jnp.sum(jnp.where(hit, ps, 0), axis=0, keepdims=True))
    pos_ref[...] = jnp.concatenate(rows, axis=0) + rank_ref[...]


def _slot_positions(idx_t, rank_t, pstart):
    k, t = idx_t.shape
    tp = min(4096, t)
    lane = pl.BlockSpec((k, tp), lambda i: (0, i))
    return pl.pallas_call(
        _pos_kernel,
        out_shape=jax.ShapeDtypeStruct((k, t), I32),
        grid=(t // tp,),
        in_specs=[lane, lane, pl.BlockSpec((N_EXPERTS, 1), lambda i: (0, 0))],
        out_specs=lane,
        compiler_params=_cparams(("parallel",)),
        name="slot_positions",
    )(idx_t, rank_t, pstart)


def _expert_kernel(be_ref, nv_ref, xa_ref, xb_ref, wgu_ref, wd_ref, ya_ref, yb_ref, wgu_b, wd_b):
    b = pl.program_id(0)
    nv = nv_ref[b]
    prev = be_ref[jnp.maximum(b - 1, 0)]

    @pl.when((b == 0) | (be_ref[b] != prev))
    def _():
        wgu_b[...] = wgu_ref[0].astype(BF16)
        wd_b[...] = wd_ref[0].astype(BF16)

    for v in range(1, EXPERT_SUBS + 1):
        @pl.when(nv == v)
        def _(v=v):
            rows = v * EXPERT_ROWS
            x = _unpack_rows(xa_ref[:rows, :], xb_ref[:rows, :]).astype(BF16)
            gu = _dot(x, wgu_b[...])
            ff = gu.shape[1] // 2
            act = _silu(gu[:, :ff]) * gu[:, ff:]
            y = _dot(act.astype(BF16), wd_b[...])
            ya_ref[:rows, :], yb_ref[:rows, :] = _pack_rows(y)
            if rows < STEP_ROWS:
                ya_ref[rows:, :] = jnp.zeros((STEP_ROWS - rows, ya_ref.shape[1]), U32)
                yb_ref[rows:, :] = jnp.zeros((STEP_ROWS - rows, yb_ref.shape[1]), U32)


def _experts(block_e, n_valid, n_used, xs_a, xs_b, w_gu, w_down):
    p, dp = xs_a.shape
    _, d, f2 = w_gu.shape
    _, f, _ = w_down.shape
    row_spec = pl.BlockSpec((STEP_ROWS, dp), lambda b, be, nv: (b, 0))
    grid_spec = pltpu.PrefetchScalarGridSpec(
        num_scalar_prefetch=2,
        grid=(n_used,),
        in_specs=[row_spec, row_spec,
                  pl.BlockSpec((1, d, f2), lambda b, be, nv: (be[b], 0, 0)),
                  pl.BlockSpec((1, f, d), lambda b, be, nv: (be[b], 0, 0))],
        out_specs=(row_spec, row_spec),
        scratch_shapes=[pltpu.VMEM((d, f2), BF16), pltpu.VMEM((f, d), BF16)],
    )
    y_shape = jax.ShapeDtypeStruct((p, dp), U32)
    return pl.pallas_call(
        _expert_kernel,
        out_shape=(y_shape, y_shape),
        grid_spec=grid_spec,
        compiler_params=_cparams(("arbitrary",)),
        name="routed_experts",
    )(block_e, n_valid, xs_a, xs_b, w_gu, w_down)


def _shared_kernel(h2a_ref, h2b_ref, wsg_ref, wsd_ref, o_ref):
    h2 = _unpack_rows(h2a_ref[...], h2b_ref[...]).astype(BF16)
    gu = _dot(h2, wsg_ref[...])
    ff = gu.shape[1] // 2
    o_ref[...] = _dot((_silu(gu[:, :ff]) * gu[:, ff:]).astype(BF16), wsd_ref[...]).astype(BF16)


def _shared(h2a, h2b, w_sh_gu, w_sh_down):
    t, dq = h2a.shape
    d = w_sh_down.shape[1]
    const = lambda shape: pl.BlockSpec(shape, lambda i: (0,) * len(shape))
    tile = min(TT_SHARED, t)
    assert t % tile == 0
    tok = lambda wd: pl.BlockSpec((tile, wd), lambda i: (i, 0))
    return pl.pallas_call(
        _shared_kernel,
        out_shape=jax.ShapeDtypeStruct((t, d), BF16),
        grid=(t // tile,),
        in_specs=[tok(dq), tok(dq), const(w_sh_gu.shape), const(w_sh_down.shape)],
        out_specs=tok(d),
        compiler_params=_cparams(("parallel",)),
        name="shared_expert",
    )(h2a, h2b, w_sh_gu, w_sh_down)


def _combine_kernel(r1_ref, ysh_ref, yga_ref, ygb_ref, wn_ref, mod_ref, o_ref):
    wn = wn_ref[...]
    y = ysh_ref[...].astype(F32)
    for k in range(TOP_K):
        y = y + wn[:, k:k + 1] * _unpack_rows(yga_ref[k], ygb_ref[k])
    o_ref[...] = r1_ref[...] + mod_ref[0, 5:6, :] * y


def _combine(r1, y_sh, yg_a, yg_b, w_nat, mod, s):
    t, d = r1.shape
    per_b = s // TT_COMBINE
    tok = lambda wd: pl.BlockSpec((TT_COMBINE, wd), lambda i: (i, 0))
    return pl.pallas_call(
        _combine_kernel,
        out_shape=jax.ShapeDtypeStruct((t, d), F32),
        grid=(t // TT_COMBINE,),
        in_specs=[tok(d), tok(d),
                  pl.BlockSpec((TOP_K, TT_COMBINE, d // 4), lambda i: (0, i, 0)),
                  pl.BlockSpec((TOP_K, TT_COMBINE, d // 4), lambda i: (0, i, 0)),
                  tok(TOP_K),
                  pl.BlockSpec((1, N_MOD, d), lambda i: (i // per_b, 0, 0))],
        out_specs=tok(d),
        compiler_params=_cparams(("parallel",)),
        name="routed_combine",
    )(r1, y_sh, yg_a, yg_b, w_nat, mod)


def _sc_mesh():
    return plsc.VectorSubcoreMesh(core_axis_name="c", subcore_axis_name="s")


def _sc_dispatch(xa, xb, pos_t, p):
    t, dc = xa.shape
    k = pos_t.shape[0]
    out = jax.ShapeDtypeStruct((p, dc), xa.dtype)

    @functools.partial(pl.kernel, out_type=(out, out), mesh=_sc_mesh())
    def kern(xa_hbm, xb_hbm, pos_hbm, oa_hbm, ob_hbm):
        for src, dst in ((xa_hbm, oa_hbm), (xb_hbm, ob_hbm)):
            def body(x_vmem, i_vmem, dst=dst):
                pltpu.sync_copy(x_vmem, dst.at[i_vmem.at[0]])

            pltpu.emit_pipeline(
                body,
                grid=(t // SC_WINDOW, k),
                in_specs=[pl.BlockSpec((SC_WINDOW, dc), lambda i, kk: (i, 0)),
                          pl.BlockSpec((1, SC_WINDOW), lambda i, kk: (kk, i))],
                out_specs=[],
                core_axis_name=("c", "s"),
                dimension_semantics=(pltpu.PARALLEL, pltpu.ARBITRARY),
            )(src, pos_hbm)

    return kern(xa, xb, pos_t)


def _sc_gather(ya, yb, pos_row):
    n = pos_row.shape[1]
    dc = ya.shape[1]
    out = jax.ShapeDtypeStruct((n, dc), ya.dtype)

    @functools.partial(pl.kernel, out_type=(out, out), mesh=_sc_mesh())
    def kern(ya_hbm, yb_hbm, pos_hbm, oa_hbm, ob_hbm):
        for src, dst in ((ya_hbm, oa_hbm), (yb_hbm, ob_hbm)):
            def body(i_vmem, o_vmem, src=src):
                pltpu.sync_copy(src.at[i_vmem.at[0]], o_vmem)

            pltpu.emit_pipeline(
                body,
                grid=(n // SC_WINDOW,),
                in_specs=[pl.BlockSpec((1, SC_WINDOW), lambda i: (0, i))],
                out_specs=[pl.BlockSpec((SC_WINDOW, dc), lambda i: (i, 0))],
                core_axis_name=("c", "s"),
                dimension_semantics=(pltpu.PARALLEL,),
            )(pos_hbm, dst)

    return kern(ya, yb, pos_row)


def kernel(x, c, positions, w_ada, b_ada, g_mix, w_in, g_q, g_k, lam, g_sub, w_out, g_ffn, w_router,
           e_bias, w_sh_gu, w_sh_down, w_gu, w_down):
    b, s, d = x.shape
    assert w_ada.shape[0] == 1 and s % TS == 0 and d % 256 == 0
    t = b * s
    wdt = w_in.shape[2] // 6

    c_pad = jnp.zeros((8, d), F32).at[:b].set(c.astype(F32))
    mod = _ada(c_pad, w_ada[0], b_ada)[:b].reshape(b, N_MOD, d)

    sec = [w_in[0][:, i * wdt:(i + 1) * wdt] for i in range(6)]
    w_t = jnp.concatenate([sec[0], sec[1], sec[2], sec[3], sec[5]], axis=1).T.astype(BF16)
    w_sk = sec[4].astype(BF16)
    inv = (ROPE_THETA ** (-jnp.arange(0, HEAD_DIM, 2, dtype=F32) / HEAD_DIM)).reshape(HALF, 1)
    pos3 = positions.astype(I32).reshape(b, 1, s)

    qd, kd, vd, qs, ks, vs = _inproj(x, mod, g_mix, pos3, inv, g_q.reshape(HEAD_DIM, 1),
                                     g_k.reshape(HEAD_DIM, 1), w_t, w_sk)
    mixed_d = _diff_attention(qd, kd, vd, lam[0], g_sub.reshape(2 * HEAD_DIM, 1))
    mixed_s = _sb_attention(qs, ks, vs)

    wr_t = w_router[0].T
    wr_hi = wr_t.astype(BF16)
    wr_lo = (wr_t - wr_hi.astype(F32)).astype(BF16)
    r1, h2a, h2b, idx_t, w_nat, rank_t, counts = _outproj_route(
        mixed_d.reshape(t, wdt), mixed_s.reshape(t, wdt), x.reshape(t, d), mod,
        w_out[0].astype(BF16), g_ffn, wr_hi, wr_lo, e_bias.reshape(N_EXPERTS, 1), s)

    counts = counts.reshape(N_EXPERTS)
    padded = (counts + STEP_ROWS - 1) // STEP_ROWS * STEP_ROWS
    pend = jnp.cumsum(padded)
    pstart = (pend - padded).astype(I32)
    n_steps = t * TOP_K // STEP_ROWS + N_EXPERTS
    step_row0 = jnp.arange(n_steps, dtype=pend.dtype) * STEP_ROWS
    block_e = jnp.minimum(jnp.sum(pend[None, :] <= step_row0[:, None], axis=1), N_EXPERTS - 1).astype(I32)
    rows_left = counts[block_e] - (step_row0 - pstart[block_e])
    n_valid = jnp.where(step_row0 < pend[-1],
                        jnp.clip((rows_left + EXPERT_ROWS - 1) // EXPERT_ROWS, 0, EXPERT_SUBS), 0).astype(I32)
    n_used = (pend[-1] // STEP_ROWS).astype(I32)
    pos_t = _slot_positions(idx_t, rank_t, pstart.reshape(N_EXPERTS, 1))

    xs_a, xs_b = _sc_dispatch(h2a, h2b, pos_t, n_steps * STEP_ROWS)
    y_a, y_b = _experts(block_e, n_valid, n_used, xs_a, xs_b, w_gu[0], w_down[0])
    y_sh = _shared(h2a, h2b, w_sh_gu[0].astype(BF16), w_sh_down[0].astype(BF16))
    yg_a, yg_b = _sc_gather(y_a, y_b, pos_t.reshape(1, TOP_K * t))
    dq4 = d // 4
    out = _combine(r1, y_sh, yg_a.reshape(TOP_K, t, dq4), yg_b.reshape(TOP_K, t, dq4), w_nat, mod, s)
    return out.reshape(b, s, d).astype(x.dtype)
```
